```python
import math, functools
import jax, jax.numpy as jnp
from jax import lax
import numpy as np


D_MODEL = 1024
BATCH = 8
SEQ = 4096
DEPTH = 2

GRID_W = 64
CTX_LEN = 256
HEAD_DIM = 64
ROPE_THETA = 10000.0
NORM_EPS = 1e-6
Q_BLOCK = 128
MIX_WIDTH = D_MODEL

A_HEADS = (MIX_WIDTH // 2) // (2 * HEAD_DIM)
A_QK = A_HEADS * 2 * HEAD_DIM
A_VDIM = 2 * HEAD_DIM
A_WIDTH = A_HEADS * A_VDIM
B_HEADS = (MIX_WIDTH // 2) // HEAD_DIM
B_WIDTH = B_HEADS * HEAD_DIM
NA_ROWS = 8
NA_COLS = 16
EVEN_IN = 2 * A_QK + A_WIDTH + 3 * B_WIDTH

C_HEADS = (MIX_WIDTH // 2) // HEAD_DIM
C_KV_HEADS = 2
C_REP = C_HEADS // C_KV_HEADS
C_Q = C_HEADS * HEAD_DIM
C_KV = C_KV_HEADS * HEAD_DIM
WINDOW = 128
D_HEADDIM = 64
D_INNER = MIX_WIDTH // 2
D_HEADS = D_INNER // D_HEADDIM
D_GROUPS = 2
D_HPG = D_HEADS // D_GROUPS
D_STATE = 128
D_CONV = 5
D_XBC = D_INNER + 2 * D_GROUPS * D_STATE
SSD_CHUNK = 128
ODD_IN = C_Q + 2 * C_KV + D_XBC + D_INNER + 2 * D_HEADS

N_EXPERTS = 16
EXPERT_FF = D_MODEL
CAPACITY_FACTOR = 2

kernel_name = 'hybrid_diffusion_trunk_diffattn_natten_swa_ssd_ecmoe'


def rmsnorm(u, g):
    uf = u.astype(jnp.float32)
    y = uf * lax.rsqrt(jnp.mean(uf * uf, axis=-1, keepdims=True) + NORM_EPS)
    return (y * g.astype(jnp.float32)).astype(u.dtype)


def split_cols(p, sizes):
    return jnp.split(p, np.cumsum(sizes)[:-1].tolist(), axis=-1)


def ada_modulation(cond, w_mod, b_mod):
    mod = jax.nn.silu(cond) @ w_mod + b_mod
    return jnp.split(mod[..., None, :], 6, axis=-1)


def axial_rope_tables(n):
    t = jnp.arange(n, dtype=jnp.int32)
    row = (t // GRID_W).astype(jnp.float32)
    col = (t % GRID_W).astype(jnp.float32)
    n_freq = HEAD_DIM // 4
    inv_freq = ROPE_THETA ** (-jnp.arange(n_freq, dtype=jnp.float32) / n_freq)
    ang = jnp.concatenate([row[:, None] * inv_freq, col[:, None] * inv_freq], axis=-1)
    return jnp.cos(ang), jnp.sin(ang)


def apply_rope(u, cos, sin):
    uf = u.astype(jnp.float32).reshape(u.shape[:-1] + (HEAD_DIM // 2, 2))
    u1, u2 = uf[..., 0], uf[..., 1]
    cs, sn = cos[:, None, :], sin[:, None, :]
    out = jnp.stack([u1 * cs - u2 * sn, u1 * sn + u2 * cs], axis=-1)
    return out.reshape(u.shape).astype(u.dtype)


def differential_attention(q, k, v, lam):
    scale = q.shape[-1] ** -0.5
    s = jnp.einsum('bqhmd,bkhmd->bhmqk', q, k).astype(jnp.float32) * scale
    p = jax.nn.softmax(s, axis=-1)
    a = (p[:, :, 0] - lam * p[:, :, 1]).astype(v.dtype)
    return jnp.einsum('bhqk,bkhe->bqhe', a, v)


def context_attention(q, k, v):
    b, m, _, d = q.shape
    s = jnp.einsum('bqhd,bkhd->bhqk', q, k).astype(jnp.float32) * d ** -0.5
    p = jax.nn.softmax(s, axis=-1).astype(v.dtype)
    return jnp.einsum('bhqk,bkhd->bqhd', p, v).reshape(b, m, -1)


def neighbourhood_attention(q, k, v, k_ctx, v_ctx, rpb):
    b, n, nh, d = q.shape
    rows = n // GRID_W
    kh, kw = min(NA_ROWS, rows), NA_COLS
    qg = q.reshape(b, rows, GRID_W, nh, d)
    kg = k.reshape(b, rows, GRID_W, nh, d)
    vg = v.reshape(b, rows, GRID_W, nh, d)
    r_idx = jnp.arange(rows)
    row_start = jnp.clip(r_idx - kh // 2, 0, rows - kh)
    w_idx = jnp.arange(GRID_W)
    col_idx = jnp.clip(w_idx - kw // 2, 0, GRID_W - kw)[:, None] + jnp.arange(kw)
    rel_c = col_idx - w_idx[:, None] + NA_COLS - 1
    scale = d ** -0.5
    n_keys = kh * kw

    def row_block(args):
        q_r, r, rs = args
        k_rows = lax.dynamic_slice_in_dim(kg, rs, kh, axis=1)
        v_rows = lax.dynamic_slice_in_dim(vg, rs, kh, axis=1)
        k_nb = k_rows[:, :, col_idx].transpose(0, 2, 1, 3, 4, 5).reshape(b, GRID_W, n_keys, nh, d)
        v_nb = v_rows[:, :, col_idx].transpose(0, 2, 1, 3, 4, 5).reshape(b, GRID_W, n_keys, nh, d)
        rel_r = rs + jnp.arange(kh) - r + NA_ROWS - 1
        bias = rpb[:, rel_r[None, :, None], rel_c[:, None, :]].reshape(nh, GRID_W, n_keys)
        s_nb = jnp.einsum('bwhd,bwkhd->bhwk', q_r, k_nb).astype(jnp.float32) * scale + bias.astype(jnp.float32)
        s_cx = jnp.einsum('bwhd,bmhd->bhwm', q_r, k_ctx).astype(jnp.float32) * scale
        p = jax.nn.softmax(jnp.concatenate([s_nb, s_cx], axis=-1), axis=-1).astype(v.dtype)
        return (jnp.einsum('bhwk,bwkhd->bwhd', p[..., :n_keys], v_nb)
                + jnp.einsum('bhwm,bmhd->bwhd', p[..., n_keys:], v_ctx))

    o = lax.map(row_block, (qg.transpose(1, 0, 2, 3, 4), r_idx, row_start))
    return o.transpose(1, 0, 2, 3, 4).reshape(b, n, nh * d)


def window_attention_with_sink(q, k, v, k_ctx, v_ctx, sinks):
    b, n, g, r, d = q.shape
    m = k_ctx.shape[1]
    nb = n // Q_BLOCK
    side = -(-WINDOW // Q_BLOCK)
    span = (2 * side + 1) * Q_BLOCK
    pad = ((0, 0), (side * Q_BLOCK, side * Q_BLOCK), (0, 0), (0, 0))
    kp = jnp.pad(k, pad)
    vp = jnp.pad(v, pad)
    sink = jnp.broadcast_to(sinks.astype(jnp.float32).reshape(1, g, r, 1, 1), (b, g, r, Q_BLOCK, 1))
    scale = d ** -0.5
    q_blocks = q.reshape(b, nb, Q_BLOCK, g, r, d).swapaxes(0, 1)

    def block(args):
        qb, blk = args
        kb = lax.dynamic_slice_in_dim(kp, blk * Q_BLOCK, span, axis=1)
        vb = lax.dynamic_slice_in_dim(vp, blk * Q_BLOCK, span, axis=1)
        qpos = blk * Q_BLOCK + jnp.arange(Q_BLOCK)
        kpos = (blk - side) * Q_BLOCK + jnp.arange(span)
        valid = (jnp.abs(qpos[:, None] - kpos[None, :]) <= WINDOW) & (kpos >= 0) & (kpos < n)
        s_loc = jnp.einsum('bqgrd,bkgd->bgrqk', qb, kb).astype(jnp.float32) * scale
        s_loc = jnp.where(valid, s_loc, -jnp.inf)
        s_ctx = jnp.einsum('bqgrd,bmgd->bgrqm', qb, k_ctx).astype(jnp.float32) * scale
        p = jax.nn.softmax(jnp.concatenate([s_loc, s_ctx, sink], axis=-1), axis=-1).astype(v.dtype)
        return (jnp.einsum('bgrqk,bkgd->bqgrd', p[..., :span], vb)
                + jnp.einsum('bgrqm,bmgd->bqgrd', p[..., span:span + m], v_ctx))

    o = lax.map(block, (q_blocks, jnp.arange(nb)))
    return o.swapaxes(0, 1).reshape(b, n, g * r * d)


def sink_attention(q, k, v, sinks):
    b, m, g, r, d = q.shape
    s = jnp.einsum('bqgrd,bkgd->bgrqk', q, k).astype(jnp.float32) * d ** -0.5
    sk = jnp.broadcast_to(sinks.astype(jnp.float32).reshape(1, g, r, 1, 1), (b, g, r, m, 1))
    p = jax.nn.softmax(jnp.concatenate([s, sk], axis=-1), axis=-1)[..., :m].astype(v.dtype)
    return jnp.einsum('bgrqk,bkgd->bqgrd', p, v).reshape(b, m, g * r * d)


def depthwise_conv_silu(u, w, bias):
    y = lax.conv_general_dilated(u, w.astype(u.dtype), window_strides=(1,),
                                 padding=((D_CONV // 2, D_CONV // 2),),
                                 dimension_numbers=('NWC', 'WIO', 'NWC'),
                                 feature_group_count=u.shape[-1])
    return jax.nn.silu(y + bias)


def ssd_prep(xbc, dt_raw, conv_w, conv_b, dt_bias_f, dt_bias_b):
    b, L, _ = xbc.shape
    u = depthwise_conv_silu(xbc, conv_w, conv_b).astype(jnp.float32)
    xs, bm, cm = split_cols(u, (D_INNER, D_GROUPS * D_STATE, D_GROUPS * D_STATE))
    xs = xs.reshape(b, L, D_GROUPS, D_HPG, D_HEADDIM)
    bm = bm.reshape(b, L, D_GROUPS, D_STATE)
    cm = cm.reshape(b, L, D_GROUPS, D_STATE)
    dt_raw = dt_raw.astype(jnp.float32)
    dtf = jax.nn.softplus(dt_raw[..., :D_HEADS] + dt_bias_f.astype(jnp.float32)).reshape(b, L, D_GROUPS, D_HPG)
    dtb = jax.nn.softplus(dt_raw[..., D_HEADS:] + dt_bias_b.astype(jnp.float32)).reshape(b, L, D_GROUPS, D_HPG)
    return xs, bm, cm, dtf, dtb


def ssd_chunked(x, dt, a, bm, cm, init_state, want_y):
    b, L, g, r, p = x.shape
    nst = bm.shape[-1]
    nc = L // SSD_CHUNK
    x = x.reshape(b, nc, SSD_CHUNK, g, r, p)
    dt = dt.reshape(b, nc, SSD_CHUNK, g, r)
    bm = bm.reshape(b, nc, SSD_CHUNK, g, nst)
    cm = cm.reshape(b, nc, SSD_CHUNK, g, nst)
    cum = jnp.cumsum(dt * a, axis=2)
    last = cum[:, :, -1]
    xdt = x * dt[..., None]
    states = jnp.einsum('bcsgn,bcsgrp->bcgrpn', bm, xdt * jnp.exp(last[:, :, None] - cum)[..., None])

    def step(carry, inp):
        st, dec = inp
        return carry * dec[..., None, None] + st, carry

    final, prev = lax.scan(step, init_state, (states.swapaxes(0, 1), jnp.exp(last).swapaxes(0, 1)))
    if not want_y:
        return None, final
    prev = prev.swapaxes(0, 1)
    diff = cum[:, :, :, None] - cum[:, :, None, :]
    causal = jnp.tril(jnp.ones((SSD_CHUNK, SSD_CHUNK), dtype=bool))[:, :, None, None]
    lmat = jnp.exp(jnp.where(causal, diff, -jnp.inf))
    cb = jnp.einsum('bclgn,bcsgn->bclsg', cm, bm)
    y_diag = jnp.einsum('bclsgr,bcsgrp->bclgrp', cb[..., None] * lmat, xdt)
    y_off = jnp.einsum('bclgn,bcgrpn->bclgrp', cm, prev) * jnp.exp(cum)[..., None]
    return (y_diag + y_off).reshape(b, L, g, r, p), final


def ssd_output(y, xs, z, d_skip, gnorm_g):
    b, L = y.shape[:2]
    y = y + d_skip.astype(jnp.float32).reshape(D_GROUPS, D_HPG)[:, :, None] * xs
    y = y.reshape(b, L, D_INNER) * jax.nn.silu(z.astype(jnp.float32))
    y = y.reshape(b, L, D_GROUPS, D_INNER // D_GROUPS)
    y = y * lax.rsqrt(jnp.mean(y * y, axis=-1, keepdims=True) + NORM_EPS)
    return y.reshape(b, L, D_INNER) * gnorm_g.astype(jnp.float32)


def expert_choice_ffn(h, w_router, w_gate, w_up, w_down):
    b, n, _ = h.shape
    cap = CAPACITY_FACTOR * n // N_EXPERTS
    aff = jax.nn.softmax(jnp.einsum('bnd,de->bne', h, w_router).astype(jnp.float32), axis=-1)
    gate, idx = lax.top_k(jnp.swapaxes(aff, 1, 2), cap)
    bidx = jnp.arange(b)[:, None, None]
    xg = h[bidx, idx]
    hid = jax.nn.silu(jnp.einsum('becd,edf->becf', xg, w_gate)) * jnp.einsum('becd,edf->becf', xg, w_up)
    out = jnp.einsum('becf,efd->becd', hid, w_down) * gate[..., None].astype(h.dtype)
    return jnp.zeros_like(h).at[bidx, idx].add(out)


def mixer_even(h, hc, ctx_out, *, layer_idx, w_in, w_out, lam_q1, lam_k1, lam_q2, lam_k2, subln_g, rpb):
    b, n, _ = h.shape
    m = hc.shape[1]
    sizes = (A_QK, A_QK, A_WIDTH, B_WIDTH, B_WIDTH, B_WIDTH)
    qa, ka, va, qb, kb, vb = split_cols(h @ w_in, sizes)
    qa_c, ka_c, va_c, qb_c, kb_c, vb_c = split_cols(hc @ w_in, sizes)
    cos, sin = axial_rope_tables(n)
    qa = apply_rope(qa.reshape(b, n, 2 * A_HEADS, HEAD_DIM), cos, sin).reshape(b, n, A_HEADS, 2, HEAD_DIM)
    ka = apply_rope(ka.reshape(b, n, 2 * A_HEADS, HEAD_DIM), cos, sin).reshape(b, n, A_HEADS, 2, HEAD_DIM)
    va = va.reshape(b, n, A_HEADS, A_VDIM)
    ka_c = ka_c.reshape(b, m, A_HEADS, 2, HEAD_DIM)
    va_c = va_c.reshape(b, m, A_HEADS, A_VDIM)
    lam_init = 0.8 - 0.6 * math.exp(-0.3 * layer_idx)
    lam = (jnp.exp(jnp.sum(lam_q1.astype(jnp.float32) * lam_k1.astype(jnp.float32)))
           - jnp.exp(jnp.sum(lam_q2.astype(jnp.float32) * lam_k2.astype(jnp.float32))) + lam_init)
    k_all = jnp.concatenate([ka, ka_c], axis=1)
    v_all = jnp.concatenate([va, va_c], axis=1)
    q_blocks = qa.reshape(b, n // Q_BLOCK, Q_BLOCK, A_HEADS, 2, HEAD_DIM).swapaxes(0, 1)
    o_a = lax.map(lambda qblk: differential_attention(qblk, k_all, v_all, lam), q_blocks)
    o_a = o_a.swapaxes(0, 1).reshape(b, n, A_HEADS, A_VDIM)
    o_a = (rmsnorm(o_a, subln_g) * (1.0 - lam_init)).reshape(b, n, A_WIDTH)
    kb_c = kb_c.reshape(b, m, B_HEADS, HEAD_DIM)
    vb_c = vb_c.reshape(b, m, B_HEADS, HEAD_DIM)
    o_b = neighbourhood_attention(qb.reshape(b, n, B_HEADS, HEAD_DIM), kb.reshape(b, n, B_HEADS, HEAD_DIM),
                                  vb.reshape(b, n, B_HEADS, HEAD_DIM), kb_c, vb_c, rpb)
    out = jnp.concatenate([o_a, o_b], axis=-1) @ w_out
    if not ctx_out:
        return out, None
    oc_a = differential_attention(qa_c.reshape(b, m, A_HEADS, 2, HEAD_DIM), ka_c, va_c, lam)
    oc_a = (rmsnorm(oc_a, subln_g) * (1.0 - lam_init)).reshape(b, m, A_WIDTH)
    oc_b = context_attention(qb_c.reshape(b, m, B_HEADS, HEAD_DIM), kb_c, vb_c)
    out_c = jnp.concatenate([oc_a, oc_b], axis=-1) @ w_out
    return out, out_c


def mixer_odd(h, hc, ctx_out, *, w_in, w_out, sinks, conv_w, conv_b, a_log_f, a_log_b,
              dt_bias_f, dt_bias_b, d_skip, gnorm_g):
    b, n, _ = h.shape
    m = hc.shape[1]
    sizes = (C_Q, C_KV, C_KV, D_XBC, D_INNER, 2 * D_HEADS)
    q, k, v, xbc, z, dt = split_cols(h @ w_in, sizes)
    q_c, k_c, v_c, xbc_c, z_c, dt_c = split_cols(hc @ w_in, sizes)
    cos, sin = axial_rope_tables(n)
    q = apply_rope(q.reshape(b, n, C_HEADS, HEAD_DIM), cos, sin).reshape(b, n, C_KV_HEADS, C_REP, HEAD_DIM)
    k = apply_rope(k.reshape(b, n, C_KV_HEADS, HEAD_DIM), cos, sin)
    v = v.reshape(b, n, C_KV_HEADS, HEAD_DIM)
    k_c = k_c.reshape(b, m, C_KV_HEADS, HEAD_DIM)
    v_c = v_c.reshape(b, m, C_KV_HEADS, HEAD_DIM)
    o_c = window_attention_with_sink(q, k, v, k_c, v_c, sinks)
    a_f = -jnp.exp(a_log_f.astype(jnp.float32)).reshape(D_GROUPS, D_HPG)
    a_b = -jnp.exp(a_log_b.astype(jnp.float32)).reshape(D_GROUPS, D_HPG)
    xs, bm, cm, dtf, dtb = ssd_prep(xbc, dt, conv_w, conv_b, dt_bias_f, dt_bias_b)
    xs_c, bm_c, cm_c, dtf_c, dtb_c = ssd_prep(xbc_c, dt_c, conv_w, conv_b, dt_bias_f, dt_bias_b)
    flip = lambda u: jnp.flip(u, axis=1)
    s0 = jnp.zeros((b, D_GROUPS, D_HPG, D_HEADDIM, D_STATE), jnp.float32)
    yc_f, sc_f = ssd_chunked(xs_c, dtf_c, a_f, bm_c, cm_c, s0, ctx_out)
    yc_b, sc_b = ssd_chunked(flip(xs_c), flip(dtb_c), a_b, flip(bm_c), flip(cm_c), s0, ctx_out)
    y_f, _ = ssd_chunked(xs, dtf, a_f, bm, cm, sc_f, True)
    y_b, _ = ssd_chunked(flip(xs), flip(dtb), a_b, flip(bm), flip(cm), sc_b, True)
    o_d = ssd_output(y_f + flip(y_b), xs, z, d_skip, gnorm_g).astype(h.dtype)
    out = jnp.concatenate([o_c, o_d], axis=-1) @ w_out
    if not ctx_out:
        return out, None
    oc_c = sink_attention(q_c.reshape(b, m, C_KV_HEADS, C_REP, HEAD_DIM), k_c, v_c, sinks)
    oc_d = ssd_output(yc_f + flip(yc_b), xs_c, z_c, d_skip, gnorm_g).astype(h.dtype)
    out_c = jnp.concatenate([oc_c, oc_d], axis=-1) @ w_out
    return out, out_c


def trunk_layer(x, ctx, c, c_ctx, common, mixer, ctx_out):
    w_mod, b_mod, g_pre_mix, g_post_mix, g_pre_ffn, g_post_ffn, w_router, w_gate, w_up, w_down = common
    sh1, sc1, gt1, sh2, sc2, gt2 = ada_modulation(c, w_mod, b_mod)
    csh1, csc1, cgt1, csh2, csc2, cgt2 = ada_modulation(c_ctx, w_mod, b_mod)
    h = rmsnorm(x, g_pre_mix) * (1 + sc1) + sh1
    hc = rmsnorm(ctx, g_pre_mix) * (1 + csc1) + csh1
    mix, mix_c = mixer(h, hc, ctx_out)
    x = x + gt1 * rmsnorm(mix, g_post_mix)
    h = rmsnorm(x, g_pre_ffn) * (1 + sc2) + sh2
    x = x + gt2 * rmsnorm(expert_choice_ffn(h, w_router, w_gate, w_up, w_down), g_post_ffn)
    if ctx_out:
        ctx = ctx + cgt1 * rmsnorm(mix_c, g_post_mix)
        hc = rmsnorm(ctx, g_pre_ffn) * (1 + csc2) + csh2
        ctx = ctx + cgt2 * rmsnorm(expert_choice_ffn(hc, w_router, w_gate, w_up, w_down), g_post_ffn)
    return x, ctx


def setup_inputs(seed: int = 0) -> dict:
    key = jax.random.key(seed)
    keys = list(jax.random.split(key, 64))
    d = D_MODEL

    def normal(shape, std):
        return jax.random.normal(keys.pop(), shape, jnp.float32) * std

    def gain(n):
        return 1.0 + normal((n,), 0.05)

    def dt_bias_init():
        u = jax.random.uniform(keys.pop(), (D_HEADS,), jnp.float32)
        dt = jnp.exp(u * (math.log(0.1) - math.log(0.001)) + math.log(0.001))
        return dt + jnp.log(-jnp.expm1(-dt))

    p = {}
    p['x'] = normal((BATCH, SEQ, d), 1.0)
    p['c'] = normal((BATCH, d), 1.0)
    p['ctx'] = normal((BATCH, CTX_LEN, d), 1.0)
    p['c_ctx'] = normal((d,), 1.0)

    def add_common(pre, in_width):
        p[pre + 'w_mod'] = normal((d, 6 * d), 0.5 * d ** -0.5)
        p[pre + 'b_mod'] = normal((6 * d,), 0.02)
        for name in ('g_pre_mix', 'g_post_mix', 'g_pre_ffn', 'g_post_ffn'):
            p[pre + name] = gain(d)
        p[pre + 'w_in'] = normal((d, in_width), d ** -0.5)
        p[pre + 'w_out'] = normal((MIX_WIDTH, d), MIX_WIDTH ** -0.5)

    def add_moe(pre):
        p[pre + 'w_router'] = normal((d, N_EXPERTS), d ** -0.5)
        p[pre + 'w_gate'] = normal((N_EXPERTS, d, EXPERT_FF), d ** -0.5)
        p[pre + 'w_up'] = normal((N_EXPERTS, d, EXPERT_FF), d ** -0.5)
        p[pre + 'w_down'] = normal((N_EXPERTS, EXPERT_FF, d), EXPERT_FF ** -0.5)

    add_common('l0_', EVEN_IN)
    p['l0_lam_q1'] = normal((HEAD_DIM,), 0.1)
    p['l0_lam_k1'] = normal((HEAD_DIM,), 0.1)
    p['l0_lam_q2'] = normal((HEAD_DIM,), 0.1)
    p['l0_lam_k2'] = normal((HEAD_DIM,), 0.1)
    p['l0_subln_g'] = gain(A_VDIM)
    p['l0_rpb'] = normal((B_HEADS, 2 * NA_ROWS - 1, 2 * NA_COLS - 1), 0.1)
    add_moe('l0_')

    add_common('l1_', ODD_IN)
    p['l1_sinks'] = normal((C_HEADS,), 0.5)
    p['l1_conv_w'] = normal((D_CONV, 1, D_XBC), D_CONV ** -0.5)
    p['l1_conv_b'] = normal((D_XBC,), 0.02)
    p['l1_a_log_f'] = jnp.log(jax.random.uniform(keys.pop(), (D_HEADS,), jnp.float32, 1.0, 16.0))
    p['l1_a_log_b'] = jnp.log(jax.random.uniform(keys.pop(), (D_HEADS,), jnp.float32, 1.0, 16.0))
    p['l1_dt_bias_f'] = dt_bias_init()
    p['l1_dt_bias_b'] = dt_bias_init()
    p['l1_d_skip'] = 1.0 + normal((D_HEADS,), 0.1)
    p['l1_gnorm_g'] = gain(D_INNER)
    add_moe('l1_')
    return p


def reference(x, c, ctx, c_ctx,
              l0_w_mod, l0_b_mod, l0_g_pre_mix, l0_g_post_mix, l0_g_pre_ffn, l0_g_post_ffn,
              l0_w_in, l0_w_out, l0_lam_q1, l0_lam_k1, l0_lam_q2, l0_lam_k2, l0_subln_g, l0_rpb,
              l0_w_router, l0_w_gate, l0_w_up, l0_w_down,
              l1_w_mod, l1_b_mod, l1_g_pre_mix, l1_g_post_mix, l1_g_pre_ffn, l1_g_post_ffn,
              l1_w_in, l1_w_out, l1_sinks, l1_conv_w, l1_conv_b, l1_a_log_f, l1_a_log_b,
              l1_dt_bias_f, l1_dt_bias_b, l1_d_skip, l1_gnorm_g,
              l1_w_router, l1_w_gate, l1_w_up, l1_w_down):
    commons = [
        (l0_w_mod, l0_b_mod, l0_g_pre_mix, l0_g_post_mix, l0_g_pre_ffn, l0_g_post_ffn,
         l0_w_router, l0_w_gate, l0_w_up, l0_w_down),
        (l1_w_mod, l1_b_mod, l1_g_pre_mix, l1_g_post_mix, l1_g_pre_ffn, l1_g_post_ffn,
         l1_w_router, l1_w_gate, l1_w_up, l1_w_down),
    ]
    mixers = [
        functools.partial(mixer_even, layer_idx=0, w_in=l0_w_in, w_out=l0_w_out,
                          lam_q1=l0_lam_q1, lam_k1=l0_lam_k1, lam_q2=l0_lam_q2, lam_k2=l0_lam_k2,
                          subln_g=l0_subln_g, rpb=l0_rpb),
        functools.partial(mixer_odd, w_in=l1_w_in, w_out=l1_w_out, sinks=l1_sinks,
                          conv_w=l1_conv_w, conv_b=l1_conv_b, a_log_f=l1_a_log_f, a_log_b=l1_a_log_b,
                          dt_bias_f=l1_dt_bias_f, dt_bias_b=l1_dt_bias_b, d_skip=l1_d_skip,
                          gnorm_g=l1_gnorm_g),
    ]
    for i in range(DEPTH):
        x, ctx = trunk_layer(x, ctx, c, c_ctx, commons[i], mixers[i], i < DEPTH - 1)
    return x
```

```python
import functools
import math

import jax
import jax.numpy as jnp
from jax import lax
from jax.experimental import pallas as pl
from jax.experimental.pallas import tpu as pltpu

F32 = jnp.float32
BF16 = jnp.bfloat16
I32 = jnp.int32

HEAD_DIM = 64
LANES = 128
SUBLANES = 8
GRID_W = 64
ROPE_THETA = 10000.0
NORM_EPS = 1e-6
NA_ROWS = 8
NA_COLS = 16
WINDOW = 128
Q_BLOCK = 128
SSD_CHUNK = 128
D_CONV = 5
N_EXPERTS = 16
CAPACITY_FACTOR = 2
NEG_BIG = -1e30
VMEM_LIMIT = 56 * 1024 * 1024
HIGHEST = lax.Precision.HIGHEST
NT_DIMS = (((1,), (1,)), ((), ()))


def _params(sem, vmem=VMEM_LIMIT):
    return pltpu.CompilerParams(dimension_semantics=sem, vmem_limit_bytes=vmem)


def _rms(u, g):
    return u * lax.rsqrt(jnp.mean(u * u, axis=-1, keepdims=True) + NORM_EPS) * g


def _silu(u):
    return u * (1.0 / (1.0 + jnp.exp(-u)))


def _softplus(u):
    return jnp.maximum(u, 0.0) + jnp.log(1.0 + jnp.exp(-jnp.abs(u)))


def _dot(a, b):
    return jnp.dot(a, b, preferred_element_type=F32)


def _dot_nt(a, b):
    return lax.dot_general(a, b, NT_DIMS, preferred_element_type=F32)


def _half_masks(shape):
    lane = lax.broadcasted_iota(I32, shape, 1)
    return (lane % LANES) < HEAD_DIM


def _mod_kernel(c_ref, w_ref, b_ref, o_ref):
    s = _silu(c_ref[...]).astype(BF16)
    o_ref[...] = _dot(s, w_ref[...].astype(BF16)) + b_ref[...]


def _modulation(cond, w_mod, b_mod):
    rows, d = cond.shape
    n_out = w_mod.shape[1]
    tn = 1024
    return pl.pallas_call(
        _mod_kernel,
        out_shape=jax.ShapeDtypeStruct((rows, n_out), F32),
        grid=(n_out // tn,),
        in_specs=[pl.BlockSpec((rows, d), lambda j: (0, 0)),
                  pl.BlockSpec((d, tn), lambda j: (0, j)),
                  pl.BlockSpec((1, tn), lambda j: (0, j))],
        out_specs=pl.BlockSpec((rows, tn), lambda j: (0, j)),
        compiler_params=_params(("parallel",)),
        name="modulation",
    )(cond, w_mod, b_mod.reshape(1, n_out))


def _rope_block(u, cos, sin_signed, first_half):
    width = u.shape[1]
    fwd = pltpu.roll(u, HEAD_DIM // 2, axis=1)
    bwd = pltpu.roll(u, width - HEAD_DIM // 2, axis=1)
    return u * cos + jnp.where(first_half, bwd, fwd) * sin_signed


def _proj_kernel(*refs, seg_widths, rope_cols, has_t, col_chunk):
    x_ref, g_ref, sc_ref, sh_ref, w_ref = refs[:5]
    pos = 5
    if rope_cols:
        cos_ref, sin_ref = refs[pos:pos + 2]
        pos += 2
    if has_t:
        wt_ref = refs[pos]
        pos += 1
    out_refs = refs[pos:]
    h = (_rms(x_ref[0], g_ref[...]) * sc_ref[0] + sh_ref[0]).astype(BF16)
    if rope_cols:
        cos = cos_ref[...]
        sin_signed = sin_ref[...]
        first_half = (lax.broadcasted_iota(I32, cos.shape, 1) % HEAD_DIM) < HEAD_DIM // 2
    col = 0
    for o_ref, width in zip(out_refs, seg_widths):
        for c0 in range(0, width, col_chunk):
            cw = min(col_chunk, width - c0)
            acc = _dot(h, w_ref[:, col + c0:col + c0 + cw])
            if col + c0 < rope_cols:
                acc = jnp.concatenate(
                    [_rope_block(acc[:, k:k + LANES], cos, sin_signed, first_half)
                     for k in range(0, cw, LANES)], axis=1)
            o_ref[0, :, c0:c0 + cw] = acc.astype(o_ref.dtype)
        col += width
    if has_t:
        out_refs[-1][0] = _dot_nt(wt_ref[...], h)


def _project(x, g, scale1p, shift, w, segs, rope=None, wt=None):
    b, n, d = x.shape
    tm = min(512, n)
    bm = scale1p.shape[0]
    mod_map = (lambda i, j: (i, 0, 0)) if bm > 1 else (lambda i, j: (0, 0, 0))
    in_specs = [pl.BlockSpec((1, tm, d), lambda i, j: (i, j, 0)),
                pl.BlockSpec((1, d), lambda i, j: (0, 0)),
                pl.BlockSpec((1, 1, d), mod_map),
                pl.BlockSpec((1, 1, d), mod_map),
                pl.BlockSpec(w.shape, lambda i, j: (0, 0))]
    args = [x, g.reshape(1, d), scale1p, shift, w]
    rope_cols = 0
    if rope is not None:
        rope_cols, cos, sin_signed = rope
        in_specs += [pl.BlockSpec((tm, LANES), lambda i, j: (j, 0))] * 2
        args += [cos, sin_signed]
    if wt is not None:
        in_specs.append(pl.BlockSpec(wt.shape, lambda i, j: (0, 0)))
        args.append(wt)
    out_shape = [jax.ShapeDtypeStruct((b, n, wd), dt) for wd, dt in segs]
    out_specs = [pl.BlockSpec((1, tm, wd), lambda i, j: (i, j, 0)) for wd, _ in segs]
    if wt is not None:
        out_shape.append(jax.ShapeDtypeStruct((b, wt.shape[0], n), F32))
        out_specs.append(pl.BlockSpec((1, wt.shape[0], tm), lambda i, j: (i, 0, j)))
    kern = functools.partial(_proj_kernel, seg_widths=tuple(wd for wd, _ in segs),
                             rope_cols=rope_cols, has_t=wt is not None, col_chunk=512)
    return pl.pallas_call(
        kern, out_shape=out_shape, grid=(b, n // tm), in_specs=in_specs, out_specs=out_specs,
        compiler_params=_params(("parallel", "parallel")), name="in_proj",
    )(*args)


def _diff_kernel(*refs, seg_lens, tk, out_scale):
    lam_ref, q_ref, g_ref = refs[:3]
    n_seg = len(seg_lens)
    kv_refs = refs[3:3 + 2 * n_seg]
    o_ref, m_ref, acc_ref, s0_ref, s1_ref = refs[3 + 2 * n_seg:]
    qf = q_ref[0].astype(F32)
    first = _half_masks(qf.shape)
    qs = (jnp.where(first, qf, 0.0).astype(BF16), jnp.where(first, 0.0, qf).astype(BF16))
    m_ref[...] = jnp.full(m_ref.shape, NEG_BIG, F32)
    acc_ref[...] = jnp.zeros(acc_ref.shape, F32)

    def scores(kc):
        return [_dot_nt(qs[t], kc) for t in range(2)]

    def absorb(sc, vc):
        v_ones = jnp.concatenate([vc, jnp.ones(vc.shape, BF16)], axis=1)
        for t in range(2):
            m_old = m_ref[t]
            m_new = jnp.maximum(m_old, jnp.max(sc[t], axis=1, keepdims=True))
            p = jnp.exp2(sc[t] - m_new).astype(BF16)
            acc_ref[t] = jnp.exp2(m_old - m_new) * acc_ref[t] + _dot(p, v_ones)
            m_ref[t] = m_new

    def put(buf, sc):
        buf[0] = sc[0]
        buf[1] = sc[1]

    main = max(range(n_seg), key=lambda i: seg_lens[i])
    n_chunks = seg_lens[main] // tk if seg_lens[main] >= 2 * tk else 1
    if n_chunks > 1:
        assert n_chunks % 2 == 0 and seg_lens[main] == n_chunks * tk
        k_ref, v_ref = kv_refs[2 * main], kv_refs[2 * main + 1]
        rows = lambda ref, c: ref[0, pl.ds(pl.multiple_of(c * tk, tk), tk), :]
        put(s0_ref, scores(rows(k_ref, 0)))
    for s_i in range(n_seg):
        if s_i != main or n_chunks == 1:
            absorb(scores(kv_refs[2 * s_i][0]), kv_refs[2 * s_i + 1][0])
    if n_chunks > 1:
        def pair(c, last):
            put(s1_ref, scores(rows(k_ref, c + 1)))
            absorb([s0_ref[0], s0_ref[1]], rows(v_ref, c))
            if not last:
                put(s0_ref, scores(rows(k_ref, c + 2)))
            absorb([s1_ref[0], s1_ref[1]], rows(v_ref, c + 1))

        def body(j, carry):
            pair(2 * j, False)
            return carry

        lax.fori_loop(0, n_chunks // 2 - 1, body, 0)
        pair(n_chunks - 2, True)
    ratio0 = acc_ref[0, :, :LANES] / acc_ref[0, :, LANES:]
    ratio1 = acc_ref[1, :, :LANES] / acc_ref[1, :, LANES:]
    o = ratio0 - lam_ref[0] * ratio1
    o_ref[0] = (_rms(o, g_ref[...]) * out_scale).astype(o_ref.dtype)


def _diff_attention(q, kv_segs, lam, subln_g, out_scale):
    b, nq, width = q.shape
    heads = width // LANES
    tq = min(512, nq)
    in_specs = [pl.BlockSpec(memory_space=pltpu.SMEM),
                pl.BlockSpec((1, tq, LANES), lambda i, h, j: (i, j, h)),
                pl.BlockSpec((1, LANES), lambda i, h, j: (0, 0))]
    args = [lam.reshape(1).astype(F32), q, subln_g.reshape(1, LANES).astype(F32)]
    seg_lens = []
    for k, v in kv_segs:
        length = k.shape[1]
        seg_lens.append(length)
        in_specs += [pl.BlockSpec((1, length, LANES), lambda i, h, j: (i, 0, h))] * 2
        args += [k, v]
    tk = 512
    kern = functools.partial(_diff_kernel, seg_lens=tuple(seg_lens), tk=tk, out_scale=out_scale)
    return pl.pallas_call(
        kern, out_shape=jax.ShapeDtypeStruct((b, nq, width), BF16),
        grid=(b, heads, nq // tq), in_specs=in_specs,
        out_specs=pl.BlockSpec((1, tq, LANES), lambda i, h, j: (i, j, h)),
        scratch_shapes=[pltpu.VMEM((2, tq, 1), F32), pltpu.VMEM((2, tq, 2 * LANES), F32),
                        pltpu.VMEM((2, tq, tk), F32), pltpu.VMEM((2, tq, tk), F32)],
        compiler_params=_params(("parallel", "parallel", "parallel")), name="diff_attention",
    )(*args)


def _na_row_start(r, rows):
    return jnp.clip(r - NA_ROWS // 2, 0, rows - NA_ROWS)


def _natten_kernel(q_ref, k_ref, v_ref, kc_ref, vc_ref, bias_ref, o_ref, *, rows):
    r = pl.program_id(1)
    start = pl.multiple_of(_na_row_start(r, rows) * GRID_W, GRID_W)
    n_keys = NA_ROWS * GRID_W
    q = q_ref[0].astype(F32)
    kwin = k_ref[0, pl.ds(start, n_keys), :]
    vwin = v_ref[0, pl.ds(start, n_keys), :]
    kc = kc_ref[0]
    vc = vc_ref[0]
    width = q.shape[1]
    first = _half_masks((GRID_W, LANES))
    for p in range(width // LANES):
        cols = slice(p * LANES, (p + 1) * LANES)
        q_pair = q[:, cols]
        k_pair, v_pair, kc_pair, vc_pair = kwin[:, cols], vwin[:, cols], kc[:, cols], vc[:, cols]
        outs = []
        for sub in range(2):
            qm = jnp.where(first if sub == 0 else ~first, q_pair, 0.0).astype(BF16)
            s_nb = _dot_nt(qm, k_pair) + bias_ref[0, 2 * p + sub]
            s_cx = _dot_nt(qm, kc_pair)
            m = jnp.maximum(jnp.max(s_nb, axis=1, keepdims=True), jnp.max(s_cx, axis=1, keepdims=True))
            p_nb = jnp.exp(s_nb - m)
            p_cx = jnp.exp(s_cx - m)
            denom = jnp.sum(p_nb, axis=1, keepdims=True) + jnp.sum(p_cx, axis=1, keepdims=True)
            o = _dot(p_nb.astype(BF16), v_pair) + _dot(p_cx.astype(BF16), vc_pair)
            outs.append(o / denom)
        o_ref[0, :, cols] = jnp.where(first, outs[0], outs[1]).astype(o_ref.dtype)


def _na_bias_table(rpb):
    heads = rpb.shape[0]
    w = jnp.arange(GRID_W)
    kc = jnp.arange(GRID_W)
    col_start = jnp.clip(w - NA_COLS // 2, 0, GRID_W - NA_COLS)
    valid = (kc[None, :] >= col_start[:, None]) & (kc[None, :] < col_start[:, None] + NA_COLS)
    pad = GRID_W - NA_COLS
    padded = jnp.pad(rpb.astype(F32), ((0, 0), (0, 0), (pad, pad)))
    cols = jnp.stack([padded[:, :, GRID_W - 1 - wi:2 * GRID_W - 1 - wi] for wi in range(GRID_W)], axis=2)
    tab = jnp.stack([cols[:, NA_ROWS - 1 - di:2 * NA_ROWS - 1 - di] for di in range(NA_ROWS)], axis=0)
    tab = jnp.where(valid[None, None, None], tab, NEG_BIG)
    return tab.transpose(0, 1, 3, 2, 4).reshape(NA_ROWS, heads, GRID_W, NA_ROWS * GRID_W)


def _neighbourhood_attention(q, k, v, kc, vc, rpb):
    b, n, width = q.shape
    m = kc.shape[1]
    rows = n // GRID_W
    bias = _na_bias_table(rpb)
    heads = rpb.shape[0]
    kern = functools.partial(_natten_kernel, rows=rows)
    return pl.pallas_call(
        kern, out_shape=jax.ShapeDtypeStruct((b, n, width), BF16),
        grid=(b, rows),
        in_specs=[pl.BlockSpec((1, GRID_W, width), lambda i, r: (i, r, 0)),
                  pl.BlockSpec((1, n, width), lambda i, r: (i, 0, 0)),
                  pl.BlockSpec((1, n, width), lambda i, r: (i, 0, 0)),
                  pl.BlockSpec((1, m, width), lambda i, r: (i, 0, 0)),
                  pl.BlockSpec((1, m, width), lambda i, r: (i, 0, 0)),
                  pl.BlockSpec((1, heads, GRID_W, NA_ROWS * GRID_W),
                               lambda i, r: (r - _na_row_start(r, rows), 0, 0, 0))],
        out_specs=pl.BlockSpec((1, GRID_W, width), lambda i, r: (i, r, 0)),
        compiler_params=_params(("parallel", "parallel")), name="neighbourhood_attention",
    )(q, k, v, kc, vc, bias)


def _ctx_attn_kernel(q_ref, k_ref, v_ref, o_ref):
    q = q_ref[0].astype(F32)
    k = k_ref[0]
    v = v_ref[0]
    first = _half_masks((q.shape[0], LANES))
    for p in range(q.shape[1] // LANES):
        cols = slice(p * LANES, (p + 1) * LANES)
        outs = []
        for sub in range(2):
            qm = jnp.where(first if sub == 0 else ~first, q[:, cols], 0.0).astype(BF16)
            s = _dot_nt(qm, k[:, cols])
            e = jnp.exp(s - jnp.max(s, axis=1, keepdims=True))
            outs.append(_dot(e.astype(BF16), v[:, cols]) / jnp.sum(e, axis=1, keepdims=True))
        o_ref[0, :, cols] = jnp.where(first, outs[0], outs[1]).astype(o_ref.dtype)


def _context_attention(q, k, v):
    b, m, width = q.shape
    spec = pl.BlockSpec((1, m, width), lambda i: (i, 0, 0))
    return pl.pallas_call(
        _ctx_attn_kernel, out_shape=jax.ShapeDtypeStruct((b, m, width), BF16),
        grid=(b,), in_specs=[spec, spec, spec], out_specs=spec,
        compiler_params=_params(("parallel",)), name="context_attention",
    )(q, k, v)


def _swa_kernel(sink_ref, q_ref, k_ref, v_ref, kc_ref, vc_ref, o_ref, *, n_blocks):
    blk = pl.program_id(1)
    span = 3 * Q_BLOCK
    start = pl.multiple_of(jnp.clip(blk - 1, 0, n_blocks - 3) * Q_BLOCK, Q_BLOCK)
    q = q_ref[0].astype(F32)
    kwin = k_ref[0, pl.ds(start, span), :]
    vwin = v_ref[0, pl.ds(start, span), :]
    kc = kc_ref[0]
    vc = vc_ref[0]
    qpos = blk * Q_BLOCK + lax.broadcasted_iota(I32, (Q_BLOCK, span), 0)
    kpos = start + lax.broadcasted_iota(I32, (Q_BLOCK, span), 1)
    valid = jnp.abs(qpos - kpos) <= WINDOW
    first = _half_masks((Q_BLOCK, LANES))
    pairs = q.shape[1] // LANES
    pairs_per_group = pairs // (k_ref.shape[2] // LANES)
    for p in range(pairs):
        cols = slice(p * LANES, (p + 1) * LANES)
        g = p // pairs_per_group
        gcols = slice(g * LANES, (g + 1) * LANES)
        outs = []
        for sub in range(2):
            qm = jnp.where(first if sub == 0 else ~first, q[:, cols], 0.0).astype(BF16)
            s_loc = jnp.where(valid, _dot_nt(qm, kwin[:, gcols]), NEG_BIG)
            s_cx = _dot_nt(qm, kc[:, gcols])
            sink = sink_ref[2 * p + sub]
            m = jnp.maximum(jnp.maximum(jnp.max(s_loc, axis=1, keepdims=True),
                                        jnp.max(s_cx, axis=1, keepdims=True)), sink)
            p_loc = jnp.exp(s_loc - m)
            p_cx = jnp.exp(s_cx - m)
            denom = (jnp.sum(p_loc, axis=1, keepdims=True) + jnp.sum(p_cx, axis=1, keepdims=True)
                     + jnp.exp(sink - m))
            o = _dot(p_loc.astype(BF16), vwin[:, gcols]) + _dot(p_cx.astype(BF16), vc[:, gcols])
            outs.append(o / denom)
        o_ref[0, :, cols] = jnp.where(first, outs[0], outs[1]).astype(o_ref.dtype)


def _window_attention(q, k2, v2, kc2, vc2, sinks):
    b, n, width = q.shape
    m = kc2.shape[1]
    kvw = k2.shape[2]
    n_blocks = n // Q_BLOCK
    kern = functools.partial(_swa_kernel, n_blocks=n_blocks)
    return pl.pallas_call(
        kern, out_shape=jax.ShapeDtypeStruct((b, n, width), BF16),
        grid=(b, n_blocks),
        in_specs=[pl.BlockSpec(memory_space=pltpu.SMEM),
                  pl.BlockSpec((1, Q_BLOCK, width), lambda i, j: (i, j, 0)),
                  pl.BlockSpec((1, n, kvw), lambda i, j: (i, 0, 0)),
                  pl.BlockSpec((1, n, kvw), lambda i, j: (i, 0, 0)),
                  pl.BlockSpec((1, m, kvw), lambda i, j: (i, 0, 0)),
                  pl.BlockSpec((1, m, kvw), lambda i, j: (i, 0, 0))],
        out_specs=pl.BlockSpec((1, Q_BLOCK, width), lambda i, j: (i, j, 0)),
        compiler_params=_params(("parallel", "parallel")), name="window_attention",
    )(sinks.astype(F32), q, k2, v2, kc2, vc2)


def _conv_kernel(x_ref, prev_ref, next_ref, w_ref, b_ref, o_ref):
    i = pl.program_id(1)
    tm = x_ref.shape[1]
    keep_prev = jnp.where(i > 0, 1.0, 0.0)
    keep_next = jnp.where(i < pl.num_programs(1) - 1, 1.0, 0.0)
    xe = jnp.concatenate([prev_ref[0] * keep_prev, x_ref[0], next_ref[0] * keep_next], axis=0)
    acc = jnp.zeros((tm, x_ref.shape[2]), F32) + b_ref[...]
    for k in range(D_CONV):
        off = 8 - D_CONV // 2 + k
        acc = acc + w_ref[k:k + 1, :] * xe[off:off + tm, :]
    o_ref[0] = _silu(acc)


def _conv_silu(x, conv_w, conv_b):
    b, length, ch = x.shape
    tm = min(512, length)
    per = tm // 8
    last = length // 8 - 1
    return pl.pallas_call(
        _conv_kernel, out_shape=jax.ShapeDtypeStruct((b, length, ch), F32),
        grid=(b, length // tm),
        in_specs=[pl.BlockSpec((1, tm, ch), lambda i, j: (i, j, 0)),
                  pl.BlockSpec((1, 8, ch), lambda i, j: (i, jnp.maximum(j * per - 1, 0), 0)),
                  pl.BlockSpec((1, 8, ch), lambda i, j: (i, jnp.minimum((j + 1) * per, last), 0)),
                  pl.BlockSpec((D_CONV, ch), lambda i, j: (0, 0)),
                  pl.BlockSpec((1, ch), lambda i, j: (0, 0))],
        out_specs=pl.BlockSpec((1, tm, ch), lambda i, j: (i, j, 0)),
        compiler_params=_params(("parallel", "parallel")), name="conv_silu",
    )(x, x, x, conv_w.reshape(D_CONV, ch).astype(F32), conv_b.reshape(1, ch).astype(F32))


def _ssd_kernel(*refs, reverse, finalize, d_inner, d_state, n_groups):
    u_ref, dt_ref, dtt_ref, a_ref, at_ref, bias_ref, biast_ref, init_ref = refs[:8]
    pos = 8
    if finalize:
        z_ref, yb_ref, skip_ref, gn_ref = refs[pos:pos + 4]
        pos += 4
    y_ref, fin_ref, st_ref = refs[pos:pos + 3]
    c = pl.program_id(1)

    @pl.when(c == 0)
    def _():
        st_ref[...] = init_ref[0]

    q_len = SSD_CHUNK
    u = u_ref[0]
    xs = u[:, :d_inner]
    dt = _softplus(dt_ref[0] + bias_ref[...])
    dtt = _softplus(dtt_ref[0] + biast_ref[...])
    da = dt * a_ref[...]
    dat = dtt * at_ref[...]
    row = lax.broadcasted_iota(I32, (q_len, q_len), 0)
    col = lax.broadcasted_iota(I32, (q_len, q_len), 1)
    lower = row >= col
    upper = row <= col
    causal = upper if reverse else lower
    tri_col = causal.astype(F32)
    tri_row = (lower if reverse else upper).astype(F32)
    q_col = jnp.dot(tri_col, da, precision=HIGHEST, preferred_element_type=F32)
    q_row = jnp.dot(dat, tri_row, precision=HIGHEST, preferred_element_type=F32)
    total = jnp.sum(da, axis=0, keepdims=True)
    first = _half_masks((q_len, LANES))
    heads_per_group = (d_inner // HEAD_DIM) // n_groups
    pairs_per_group = heads_per_group // 2
    y_pairs = []
    for g in range(n_groups):
        b_g = u[:, d_inner + g * d_state:d_inner + (g + 1) * d_state]
        c_g = u[:, d_inner + (n_groups + g) * d_state:d_inner + (n_groups + g + 1) * d_state]
        c_bf = c_g.astype(BF16)
        cb = _dot_nt(c_bf, b_g.astype(BF16))
        b_t = b_g.T.astype(BF16)
        for pp in range(pairs_per_group):
            p = g * pairs_per_group + pp
            h0, h1 = 2 * p, 2 * p + 1
            x_pair = xs[:, p * LANES:(p + 1) * LANES]
            dt_pair = jnp.where(first, dt[:, h0:h0 + 1], dt[:, h1:h1 + 1])
            qc_pair = jnp.where(first, q_col[:, h0:h0 + 1], q_col[:, h1:h1 + 1])
            tot_pair = jnp.where(first[:1], total[:, h0:h0 + 1], total[:, h1:h1 + 1])
            xdt = x_pair * dt_pair
            xdt_bf = xdt.astype(BF16)
            diag = []
            for h in (h0, h1):
                diff = q_col[:, h:h + 1] - q_row[h:h + 1, :]
                lmat = jnp.exp(jnp.where(causal, diff, NEG_BIG))
                diag.append(_dot((cb * lmat).astype(BF16), xdt_bf))
            y_diag = jnp.where(first, diag[0], diag[1])
            state = st_ref[p]
            y_off = _dot(c_bf, state.astype(BF16)) * jnp.exp(qc_pair)
            y_pairs.append(y_diag + y_off)
            carry_in = (xdt * jnp.exp(tot_pair - qc_pair)).astype(BF16)
            st_ref[p] = jnp.exp(tot_pair) * state + _dot(b_t, carry_in)
    y = jnp.concatenate(y_pairs, axis=1)
    if finalize:
        y = y + yb_ref[0] + skip_ref[...] * xs
        y = y * _silu(z_ref[0])
        gw = d_inner // n_groups
        y = jnp.concatenate(
            [y[:, g * gw:(g + 1) * gw] * lax.rsqrt(
                jnp.mean(y[:, g * gw:(g + 1) * gw] ** 2, axis=1, keepdims=True) + NORM_EPS)
             for g in range(n_groups)], axis=1)
        y = y * gn_ref[...]
    y_ref[0] = y.astype(y_ref.dtype)

    @pl.when(c == pl.num_programs(1) - 1)
    def _():
        fin_ref[0] = st_ref[...]


def _ssd_scan(u, dt, dtt, a, dt_bias, init, *, reverse, d_inner, d_state, n_groups, final=None):
    b, length, ch = u.shape
    heads = dt.shape[2]
    nc = length // SSD_CHUNK
    pairs = heads // 2
    cidx = (lambda j: nc - 1 - j) if reverse else (lambda j: j)
    small = lambda shape: pl.BlockSpec(shape, lambda i, j: (0,) * len(shape))
    in_specs = [pl.BlockSpec((1, SSD_CHUNK, ch), lambda i, j: (i, cidx(j), 0)),
                pl.BlockSpec((1, SSD_CHUNK, heads), lambda i, j: (i, cidx(j), 0)),
                pl.BlockSpec((1, heads, SSD_CHUNK), lambda i, j: (i, 0, cidx(j))),
                small((1, heads)), small((heads, 1)), small((1, heads)), small((heads, 1)),
                pl.BlockSpec((1, pairs, d_state, LANES), lambda i, j: (i, 0, 0, 0))]
    args = [u, dt, dtt, a.reshape(1, heads), a.reshape(heads, 1),
            dt_bias.reshape(1, heads).astype(F32), dt_bias.reshape(heads, 1).astype(F32), init]
    out_dtype = F32
    if final is not None:
        z, yb, d_skip, gnorm_g = final
        in_specs += [pl.BlockSpec((1, SSD_CHUNK, d_inner), lambda i, j: (i, cidx(j), 0)),
                     pl.BlockSpec((1, SSD_CHUNK, d_inner), lambda i, j: (i, cidx(j), 0)),
                     small((1, d_inner)), small((1, d_inner))]
        args += [z, yb, jnp.repeat(d_skip.astype(F32), HEAD_DIM).reshape(1, d_inner),
                 gnorm_g.reshape(1, d_inner).astype(F32)]
        out_dtype = BF16
    kern = functools.partial(_ssd_kernel, reverse=reverse, finalize=final is not None,
                             d_inner=d_inner, d_state=d_state, n_groups=n_groups)
    return pl.pallas_call(
        kern,
        out_shape=[jax.ShapeDtypeStruct((b, length, d_inner), out_dtype),
                   jax.ShapeDtypeStruct((b, pairs, d_state, LANES), F32)],
        grid=(b, nc), in_specs=in_specs,
        out_specs=[pl.BlockSpec((1, SSD_CHUNK, d_inner), lambda i, j: (i, cidx(j), 0)),
                   pl.BlockSpec((1, pairs, d_state, LANES), lambda i, j: (i, 0, 0, 0))],
        scratch_shapes=[pltpu.VMEM((pairs, d_state, LANES), F32)],
        compiler_params=_params(("parallel", "arbitrary")), name="ssd_scan",
    )(*args)


def _outproj_kernel(oa_ref, ob_ref, w_ref, x_ref, gpost_ref, gate_ref, gpre_ref, sc_ref, sh_ref, wr_ref,
                    xo_ref, h_ref, aff_ref):
    half = oa_ref.shape[2]
    mix = _dot(oa_ref[0], w_ref[:half, :]) + _dot(ob_ref[0], w_ref[half:, :])
    x_new = x_ref[0] + gate_ref[0] * _rms(mix, gpost_ref[...])
    xo_ref[0] = x_new
    h = _rms(x_new, gpre_ref[...]) * sc_ref[0] + sh_ref[0]
    h_ref[0] = h
    logits = lax.dot_general(wr_ref[...], h, NT_DIMS, precision=HIGHEST, preferred_element_type=F32)
    e = jnp.exp(logits - jnp.max(logits, axis=0, keepdims=True))
    aff_ref[0] = e / jnp.sum(e, axis=0, keepdims=True)


def _out_project(oa, ob, w_out, x, g_post, gate, g_pre, scale1p, shift, w_router):
    b, n, d = x.shape
    half = oa.shape[2]
    n_exp = w_router.shape[1]
    tm = min(256, n)
    bm = gate.shape[0]
    mod_map = (lambda i, j: (i, 0, 0)) if bm > 1 else (lambda i, j: (0, 0, 0))
    row = pl.BlockSpec((1, d), lambda i, j: (0, 0))
    mod = pl.BlockSpec((1, 1, d), mod_map)
    tile = pl.BlockSpec((1, tm, d), lambda i, j: (i, j, 0))
    act = pl.BlockSpec((1, tm, half), lambda i, j: (i, j, 0))
    return pl.pallas_call(
        _outproj_kernel,
        out_shape=[jax.ShapeDtypeStruct((b, n, d), F32), jax.ShapeDtypeStruct((b, n, d), F32),
                   jax.ShapeDtypeStruct((b, n_exp, n), F32)],
        grid=(b, n // tm),
        in_specs=[act, act, pl.BlockSpec((2 * half, d), lambda i, j: (0, 0)), tile, row, mod, row, mod, mod,
                  pl.BlockSpec((n_exp, d), lambda i, j: (0, 0))],
        out_specs=[tile, tile, pl.BlockSpec((1, n_exp, tm), lambda i, j: (i, 0, j))],
        compiler_params=_params(("parallel", "parallel")), name="out_proj_router",
    )(oa, ob, w_out, x, g_post.reshape(1, d), gate, g_pre.reshape(1, d), scale1p, shift,
      w_router.T.astype(F32))


def _topk_kernel(aff_ref, idx_ref, gate_ref, work_ref, key_ref, *, cap, slot_block):
    n_exp, n = aff_ref.shape[1], aff_ref.shape[2]
    n_blk = n // LANES
    aff = aff_ref[0]

    def search(i, thr):
        cand = thr | jnp.left_shift(jnp.int32(1), 30 - i)
        cnt = jnp.sum(jnp.where(aff_ref[0] >= pltpu.bitcast(cand, F32), 1.0, 0.0), axis=1, keepdims=True)
        return jnp.where(cnt >= cap, cand, thr)

    thr = lax.fori_loop(0, 31, search, jnp.zeros((n_exp, 1), I32))
    above = aff >= pltpu.bitcast(thr + 1, F32)
    tied = (aff >= pltpu.bitcast(thr, F32)) & ~above
    need = cap - jnp.sum(jnp.where(above, 1.0, 0.0), axis=1, keepdims=True)

    r = lax.broadcasted_iota(I32, (LANES, LANES), 0)
    c = lax.broadcasted_iota(I32, (LANES, LANES), 1)
    incl = (r <= c).astype(BF16)

    def exclusive_prefix(mask_f32):
        offset = jnp.zeros((n_exp, 1), F32)
        for j in range(n_blk):
            blk = mask_f32[:, j * LANES:(j + 1) * LANES]
            run = _dot(blk.astype(BF16), incl)
            work_ref[:, j * LANES:(j + 1) * LANES] = run - blk + offset
            offset = offset + jnp.sum(blk, axis=1, keepdims=True)

    exclusive_prefix(jnp.where(tied, 1.0, 0.0))
    sel = above | (tied & (work_ref[...] < need))
    sel_f = jnp.where(sel, 1.0, 0.0)
    gate_ref[0] = jnp.where(sel, aff, 0.0)
    exclusive_prefix(sel_f)
    key = jnp.where(sel, work_ref[...], -1.0)
    for j in range(n_blk):
        key_ref[j] = key[:, j * LANES:(j + 1) * LANES]

    lane = lax.broadcasted_iota(I32, (slot_block, LANES), 1).astype(F32)
    slot = lax.broadcasted_iota(I32, (slot_block, LANES), 0).astype(F32)

    for e in range(n_exp):
        for sb in range(cap // slot_block):
            want = slot + float(sb * slot_block)

            def per_block(j, acc, e=e, want=want):
                key_row = key_ref[j, e:e + 1, :]
                tok = lane + (j * LANES).astype(F32)
                return acc + jnp.where(key_row == want, tok, 0.0)

            acc = lax.fori_loop(0, n_blk, per_block, jnp.zeros((slot_block, LANES), F32))
            idx_ref[0, e, sb * slot_block:(sb + 1) * slot_block, :] = (
                jnp.sum(acc, axis=1, keepdims=True).astype(I32))


def _route(aff_t):
    b, n_exp, n = aff_t.shape
    cap = CAPACITY_FACTOR * n // n_exp
    slot_block = min(LANES, cap)
    kern = functools.partial(_topk_kernel, cap=cap, slot_block=slot_block)
    idx, gate = pl.pallas_call(
        kern,
        out_shape=[jax.ShapeDtypeStruct((b, n_exp, cap, 1), I32), jax.ShapeDtypeStruct((b, n_exp, n), F32)],
        grid=(b,),
        in_specs=[pl.BlockSpec((1, n_exp, n), lambda i: (i, 0, 0))],
        out_specs=[pl.BlockSpec((1, n_exp, cap, 1), lambda i: (i, 0, 0, 0)),
                   pl.BlockSpec((1, n_exp, n), lambda i: (i, 0, 0))],
        scratch_shapes=[pltpu.VMEM((n_exp, n), F32), pltpu.VMEM((n // LANES, n_exp, LANES), F32)],
        compiler_params=_params(("parallel",)), name="expert_choice_route",
    )(aff_t)
    return idx.reshape(b * n_exp, 1, cap), gate.reshape(b * n_exp, 1, n)


def _moe_kernel(idx_ref, h_ref, wg_ref, wu_ref, wd_ref, y_ref, xg_ref, *, cap):
    sub = lax.broadcasted_iota(I32, (SUBLANES, h_ref.shape[2]), 0)

    def gather(i8, carry):
        base = pl.multiple_of(i8 * SUBLANES, SUBLANES)
        tile = jnp.zeros(sub.shape, F32)
        for u in range(SUBLANES):
            t = idx_ref[0, 0, base + u]
            r = t & (SUBLANES - 1)
            rows = h_ref[0, pl.ds(pl.multiple_of(t - r, SUBLANES), SUBLANES), :]
            rolled = pltpu.roll(rows, (u - r) & (SUBLANES - 1), axis=0)
            tile = jnp.where(sub == u, rolled, tile)
        xg_ref[pl.ds(base, SUBLANES), :] = tile
        return carry

    lax.fori_loop(0, cap // SUBLANES, gather, 0)
    xg = xg_ref[...].astype(BF16)
    gate = _dot(xg, wg_ref[0])
    up = _dot(xg, wu_ref[0])
    hid = (_silu(gate) * up).astype(BF16)
    y_ref[0, 0] = _dot(hid, wd_ref[0])


def _combine_kernel(idx_ref, gate_ref, y_ref, acc_ref, *, cap):
    @pl.when(pl.program_id(1) == 0)
    def _():
        acc_ref[...] = jnp.zeros_like(acc_ref)

    sub = lax.broadcasted_iota(I32, (SUBLANES, acc_ref.shape[2]), 0)

    def scatter(i8, carry):
        base = pl.multiple_of(i8 * SUBLANES, SUBLANES)
        y_tile = y_ref[0, 0, pl.ds(base, SUBLANES), :]
        for u0 in range(0, SUBLANES, 4):
            pending = []
            for u in range(u0, u0 + 4):
                t = idx_ref[0, 0, base + u]
                g = gate_ref[0, 0, t]
                r = t & (SUBLANES - 1)
                dst = acc_ref.at[0, pl.ds(pl.multiple_of(t - r, SUBLANES), SUBLANES), :]
                rolled = pltpu.roll(y_tile, (r - u) & (SUBLANES - 1), axis=0)
                pending.append((dst, dst[...] + g * rolled, sub == r))
            for dst, new, mask in pending:
                pltpu.store(dst, new, mask=mask)
        return carry

    lax.fori_loop(0, cap // SUBLANES, scatter, 0)


def _expert_ffn(h, idx, gate, wg, wu, wd):
    b, n, d = h.shape
    n_exp, _, ff = wg.shape
    cap = idx.shape[2]
    smem = lambda width: pl.BlockSpec((1, 1, width), lambda i, e: (i * n_exp + e, 0, 0), memory_space=pltpu.SMEM)
    y = pl.pallas_call(
        functools.partial(_moe_kernel, cap=cap),
        out_shape=jax.ShapeDtypeStruct((b, n_exp, cap, d), F32),
        grid=(b, n_exp),
        in_specs=[smem(cap),
                  pl.BlockSpec((1, n, d), lambda i, e: (i, 0, 0), pipeline_mode=pl.Buffered(1)),
                  pl.BlockSpec((1, d, ff), lambda i, e: (e, 0, 0)),
                  pl.BlockSpec((1, d, ff), lambda i, e: (e, 0, 0)),
                  pl.BlockSpec((1, ff, d), lambda i, e: (e, 0, 0))],
        out_specs=pl.BlockSpec((1, 1, cap, d), lambda i, e: (i, e, 0, 0)),
        scratch_shapes=[pltpu.VMEM((cap, d), F32)],
        compiler_params=_params(("parallel", "arbitrary")), name="expert_ffn",
    )(idx, h, wg, wu, wd)
    return pl.pallas_call(
        functools.partial(_combine_kernel, cap=cap),
        out_shape=jax.ShapeDtypeStruct((b, n, d), F32),
        grid=(b, n_exp),
        in_specs=[smem(cap), smem(n), pl.BlockSpec((1, 1, cap, d), lambda i, e: (i, e, 0, 0))],
        out_specs=pl.BlockSpec((1, n, d), lambda i, e: (i, 0, 0)),
        compiler_params=_params(("parallel", "arbitrary")), name="expert_combine",
    )(idx, gate, y)


def _residual_kernel(x_ref, f_ref, g_ref, gate_ref, o_ref):
    o_ref[0] = x_ref[0] + gate_ref[0] * _rms(f_ref[0], g_ref[...])


def _gated_residual(x, f, g, gate):
    b, n, d = x.shape
    tm = min(512, n)
    bm = gate.shape[0]
    mod_map = (lambda i, j: (i, 0, 0)) if bm > 1 else (lambda i, j: (0, 0, 0))
    tile = pl.BlockSpec((1, tm, d), lambda i, j: (i, j, 0))
    return pl.pallas_call(
        _residual_kernel, out_shape=jax.ShapeDtypeStruct((b, n, d), F32),
        grid=(b, n // tm),
        in_specs=[tile, tile, pl.BlockSpec((1, d), lambda i, j: (0, 0)), pl.BlockSpec((1, 1, d), mod_map)],
        out_specs=tile,
        compiler_params=_params(("parallel", "parallel")), name="gated_residual",
    )(x, f, g.reshape(1, d), gate)


def _rope_tables(n):
    t = jnp.arange(n, dtype=I32)
    row = (t // GRID_W).astype(F32)
    col = (t % GRID_W).astype(F32)
    n_freq = HEAD_DIM // 4
    inv_freq = ROPE_THETA ** (-jnp.arange(n_freq, dtype=F32) / n_freq)
    ang = jnp.concatenate([row[:, None] * inv_freq, col[:, None] * inv_freq], axis=-1)
    cos, sin = jnp.cos(ang), jnp.sin(ang)
    cos_t = jnp.tile(jnp.concatenate([cos, cos], axis=-1), (1, LANES // HEAD_DIM))
    sin_t = jnp.tile(jnp.concatenate([-sin, sin], axis=-1), (1, LANES // HEAD_DIM))
    return cos_t, sin_t


def _deinterleave_heads(w):
    d, width = w.shape
    w = w.reshape(d, width // HEAD_DIM, HEAD_DIM // 2, 2)
    return jnp.concatenate([w[..., 0], w[..., 1]], axis=-1).reshape(d, width)


def _dup_heads(w):
    d, width = w.shape
    w = w.reshape(d, width // HEAD_DIM, 1, HEAD_DIM)
    return jnp.concatenate([w, w], axis=2).reshape(d, 2 * width)


def _mod_split(mod_rows):
    sh1, sc1, gt1, sh2, sc2, gt2 = jnp.split(mod_rows[:, None, :], 6, axis=-1)
    return sh1, 1.0 + sc1, gt1, sh2, 1.0 + sc2, gt2


def _moe_block(x_mid, h2, aff_t, g_post_ffn, gt2, wg, wu, wd):
    idx, gate = _route(aff_t)
    ffn = _expert_ffn(h2, idx, gate, wg, wu, wd)
    return _gated_residual(x_mid, ffn, g_post_ffn, gt2)


def kernel(x, c, ctx, c_ctx, l0_w_mod, l0_b_mod, l0_g_pre_mix, l0_g_post_mix, l0_g_pre_ffn, l0_g_post_ffn, l0_w_in, l0_w_out, l0_lam_q1, l0_lam_k1, l0_lam_q2, l0_lam_k2, l0_subln_g, l0_rpb, l0_w_router, l0_w_gate, l0_w_up, l0_w_down, l1_w_mod, l1_b_mod, l1_g_pre_mix, l1_g_post_mix, l1_g_pre_ffn, l1_g_post_ffn, l1_w_in, l1_w_out, l1_sinks, l1_conv_w, l1_conv_b, l1_a_log_f, l1_a_log_b, l1_dt_bias_f, l1_dt_bias_b, l1_d_skip, l1_gnorm_g, l1_w_router, l1_w_gate, l1_w_up, l1_w_down):
    b, n, d = x.shape
    m = ctx.shape[1]
    half = d // 2
    scale = HEAD_DIM ** -0.5
    cos_t, sin_t = _rope_tables(n)
    cond = jnp.zeros((16, d), F32).at[:b].set(c).at[b].set(c_ctx)

    mod = _modulation(cond, l0_w_mod, l0_b_mod)
    sh1, sc1, gt1, sh2, sc2, gt2 = _mod_split(mod[:b])
    csh1, csc1, cgt1, csh2, csc2, cgt2 = _mod_split(mod[b:b + 1])

    a_qk = half
    wq, wk, rest = l0_w_in[:, :a_qk], l0_w_in[:, a_qk:2 * a_qk], l0_w_in[:, 2 * a_qk:]
    qb_cols = slice(half, 2 * half)
    rest = rest.at[:, qb_cols].multiply(scale)
    w_in0 = jnp.concatenate([_deinterleave_heads(wq) * (scale * math.log2(math.e)), _deinterleave_heads(wk), rest],
                            axis=1).astype(BF16)
    segs0 = [(half, BF16)] * 6
    qa, ka, va, qb, kb, vb = _project(x, l0_g_pre_mix, sc1, sh1, w_in0, segs0, rope=(2 * a_qk, cos_t, sin_t))
    qa_c, ka_c, va_c, qb_c, kb_c, vb_c = _project(ctx, l0_g_pre_mix, csc1, csh1, w_in0, segs0)

    lam_init = 0.8 - 0.6 * math.exp(-0.3 * 0)
    lam = (jnp.exp(jnp.sum(l0_lam_q1.astype(F32) * l0_lam_k1.astype(F32)))
           - jnp.exp(jnp.sum(l0_lam_q2.astype(F32) * l0_lam_k2.astype(F32))) + lam_init)
    o_a = _diff_attention(qa, [(ka, va), (ka_c, va_c)], lam, l0_subln_g, 1.0 - lam_init)
    o_b = _neighbourhood_attention(qb, kb, vb, kb_c, vb_c, l0_rpb)
    oc_a = _diff_attention(qa_c, [(ka_c, va_c)], lam, l0_subln_g, 1.0 - lam_init)
    oc_b = _context_attention(qb_c, kb_c, vb_c)

    w_out0 = l0_w_out.astype(BF16)
    wg0, wu0, wd0 = l0_w_gate.astype(BF16), l0_w_up.astype(BF16), l0_w_down.astype(BF16)
    x_mid, h2, aff_t = _out_project(o_a, o_b, w_out0, x, l0_g_post_mix, gt1, l0_g_pre_ffn, sc2, sh2, l0_w_router)
    x1 = _moe_block(x_mid, h2, aff_t, l0_g_post_ffn, gt2, wg0, wu0, wd0)
    c_mid, hc2, aff_c = _out_project(oc_a, oc_b, w_out0, ctx, l0_g_post_mix, cgt1, l0_g_pre_ffn, csc2, csh2,
                                     l0_w_router)
    ctx1 = _moe_block(c_mid, hc2, aff_c, l0_g_post_ffn, cgt2, wg0, wu0, wd0)

    mod = _modulation(cond, l1_w_mod, l1_b_mod)
    sh1, sc1, gt1, sh2, sc2, gt2 = _mod_split(mod[:b])
    csh1, csc1, _, _, _, _ = _mod_split(mod[b:b + 1])

    c_q = half
    d_inner = half
    d_heads = l1_a_log_f.shape[0]
    d_xbc = l1_conv_w.shape[2]
    c_kv = (l1_w_in.shape[1] - c_q - d_xbc - d_inner - 2 * d_heads) // 2
    n_groups = 2
    d_state = (d_xbc - d_inner) // (2 * n_groups)
    wq = l1_w_in[:, :c_q]
    wk = l1_w_in[:, c_q:c_q + c_kv]
    wv = l1_w_in[:, c_q + c_kv:c_q + 2 * c_kv]
    o_x = c_q + 2 * c_kv
    w_x = l1_w_in[:, o_x:o_x + d_xbc + d_inner]
    w_dt = l1_w_in[:, o_x + d_xbc + d_inner:]
    w_kv = jnp.concatenate([_dup_heads(_deinterleave_heads(wk)), _dup_heads(wv), w_x, w_dt], axis=1)
    w_lat = jnp.concatenate([_deinterleave_heads(wq) * scale, w_kv], axis=1).astype(BF16)
    w_ctx = w_kv.astype(BF16)
    w_dt_t = w_dt.T.astype(BF16)
    segs_lat = [(c_q, BF16), (2 * c_kv, BF16), (2 * c_kv, BF16), (d_xbc, F32), (d_inner, F32), (2 * d_heads, F32)]
    q, k2, v2, xbc, z, dt, dtt = _project(x1, l1_g_pre_mix, sc1, sh1, w_lat, segs_lat,
                                          rope=(c_q + 2 * c_kv, cos_t, sin_t), wt=w_dt_t)
    k2_c, v2_c, xbc_c, _, dt_c, dtt_c = _project(ctx1, l1_g_pre_mix, csc1, csh1, w_ctx, segs_lat[1:], wt=w_dt_t)

    o_c = _window_attention(q, k2, v2, k2_c, v2_c, l1_sinks)

    a_f = -jnp.exp(l1_a_log_f.astype(F32))
    a_b = -jnp.exp(l1_a_log_b.astype(F32))
    u = _conv_silu(xbc, l1_conv_w, l1_conv_b)
    u_c = _conv_silu(xbc_c, l1_conv_w, l1_conv_b)
    ssd = functools.partial(_ssd_scan, d_inner=d_inner, d_state=d_state, n_groups=n_groups)
    zero_state = jnp.zeros((b, d_heads // 2, d_state, LANES), F32)
    _, sc_f = ssd(u_c, dt_c[..., :d_heads], dtt_c[:, :d_heads], a_f, l1_dt_bias_f, zero_state, reverse=False)
    _, sc_b = ssd(u_c, dt_c[..., d_heads:], dtt_c[:, d_heads:], a_b, l1_dt_bias_b, zero_state, reverse=True)
    y_b, _ = ssd(u, dt[..., d_heads:], dtt[:, d_heads:], a_b, l1_dt_bias_b, sc_b, reverse=True)
    o_d, _ = ssd(u, dt[..., :d_heads], dtt[:, :d_heads], a_f, l1_dt_bias_f, sc_f, reverse=False,
                 final=(z, y_b, l1_d_skip, l1_gnorm_g))

    x_mid, h2, aff_t = _out_project(o_c, o_d, l1_w_out.astype(BF16), x1, l1_g_post_mix, gt1, l1_g_pre_ffn, sc2, sh2,
                                    l1_w_router)
    return _moe_block(x_mid, h2, aff_t, l1_g_post_ffn, gt2, l1_w_gate.astype(BF16), l1_w_up.astype(BF16),
                      l1_w_down.astype(BF16))
```

```python
import functools
import math

import jax
import jax.numpy as jnp
from jax import lax
from jax.experimental import pallas as pl
from jax.experimental.pallas import tpu as pltpu

F32 = jnp.float32
BF16 = jnp.bfloat16
I32 = jnp.int32

HEAD_DIM = 64
LANES = 128
SUBLANES = 8
GRID_W = 64
ROPE_THETA = 10000.0
NORM_EPS = 1e-6
NA_ROWS = 8
NA_COLS = 16
WINDOW = 128
Q_BLOCK = 128
SSD_CHUNK = 128
D_CONV = 5
N_EXPERTS = 16
CAPACITY_FACTOR = 2
NEG_BIG = -1e30
VMEM_LIMIT = 56 * 1024 * 1024
HIGHEST = lax.Precision.HIGHEST
NT_DIMS = (((1,), (1,)), ((), ()))


def _params(sem, vmem=VMEM_LIMIT):
    return pltpu.CompilerParams(dimension_semantics=sem, vmem_limit_bytes=vmem)


def _rms(u, g):
    return u * lax.rsqrt(jnp.mean(u * u, axis=-1, keepdims=True) + NORM_EPS) * g


def _silu(u):
    return u * (1.0 / (1.0 + jnp.exp(-u)))


def _softplus(u):
    return jnp.maximum(u, 0.0) + jnp.log(1.0 + jnp.exp(-jnp.abs(u)))


def _dot(a, b):
    return jnp.dot(a, b, preferred_element_type=F32)


def _dot_nt(a, b):
    return lax.dot_general(a, b, NT_DIMS, preferred_element_type=F32)


def _half_masks(shape):
    lane = lax.broadcasted_iota(I32, shape, 1)
    return (lane % LANES) < HEAD_DIM


def _mod_kernel(c_ref, w_ref, b_ref, o_ref):
    s = _silu(c_ref[...]).astype(BF16)
    o_ref[...] = _dot(s, w_ref[...].astype(BF16)) + b_ref[...]


def _modulation(cond, w_mod, b_mod):
    rows, d = cond.shape
    n_out = w_mod.shape[1]
    tn = 1024
    return pl.pallas_call(
        _mod_kernel,
        out_shape=jax.ShapeDtypeStruct((rows, n_out), F32),
        grid=(n_out // tn,),
        in_specs=[pl.BlockSpec((rows, d), lambda j: (0, 0)),
                  pl.BlockSpec((d, tn), lambda j: (0, j)),
                  pl.BlockSpec((1, tn), lambda j: (0, j))],
        out_specs=pl.BlockSpec((rows, tn), lambda j: (0, j)),
        compiler_params=_params(("parallel",)),
        name="modulation",
    )(cond, w_mod, b_mod.reshape(1, n_out))


def _rope_block(u, cos, sin_signed, first_half):
    width = u.shape[1]
    fwd = pltpu.roll(u, HEAD_DIM // 2, axis=1)
    bwd = pltpu.roll(u, width - HEAD_DIM // 2, axis=1)
    return u * cos + jnp.where(first_half, bwd, fwd) * sin_signed


def _proj_kernel(*refs, seg_widths, rope_cols, has_t, col_chunk):
    x_ref, g_ref, sc_ref, sh_ref, w_ref = refs[:5]
    pos = 5
    if rope_cols:
        cos_ref, sin_ref = refs[pos:pos + 2]
        pos += 2
    if has_t:
        wt_ref = refs[pos]
        pos += 1
    out_refs = refs[pos:]
    h = (_rms(x_ref[0], g_ref[...]) * sc_ref[0] + sh_ref[0]).astype(BF16)
    if rope_cols:
        cos = cos_ref[...]
        sin_signed = sin_ref[...]
        first_half = (lax.broadcasted_iota(I32, cos.shape, 1) % HEAD_DIM) < HEAD_DIM // 2
    col = 0
    for o_ref, width in zip(out_refs, seg_widths):
        for c0 in range(0, width, col_chunk):
            cw = min(col_chunk, width - c0)
            acc = _dot(h, w_ref[:, col + c0:col + c0 + cw])
            if col + c0 < rope_cols:
                acc = jnp.concatenate(
                    [_rope_block(acc[:, k:k + LANES], cos, sin_signed, first_half)
                     for k in range(0, cw, LANES)], axis=1)
            o_ref[0, :, c0:c0 + cw] = acc.astype(o_ref.dtype)
        col += width
    if has_t:
        out_refs[-1][0] = _dot_nt(wt_ref[...], h)


def _project(x, g, scale1p, shift, w, segs, rope=None, wt=None):
    b, n, d = x.shape
    tm = min(512, n)
    bm = scale1p.shape[0]
    mod_map = (lambda i, j: (i, 0, 0)) if bm > 1 else (lambda i, j: (0, 0, 0))
    in_specs = [pl.BlockSpec((1, tm, d), lambda i, j: (i, j, 0)),
                pl.BlockSpec((1, d), lambda i, j: (0, 0)),
                pl.BlockSpec((1, 1, d), mod_map),
                pl.BlockSpec((1, 1, d), mod_map),
                pl.BlockSpec(w.shape, lambda i, j: (0, 0))]
    args = [x, g.reshape(1, d), scale1p, shift, w]
    rope_cols = 0
    if rope is not None:
        rope_cols, cos, sin_signed = rope
        in_specs += [pl.BlockSpec((tm, LANES), lambda i, j: (j, 0))] * 2
        args += [cos, sin_signed]
    if wt is not None:
        in_specs.append(pl.BlockSpec(wt.shape, lambda i, j: (0, 0)))
        args.append(wt)
    out_shape = [jax.ShapeDtypeStruct((b, n, wd), dt) for wd, dt in segs]
    out_specs = [pl.BlockSpec((1, tm, wd), lambda i, j: (i, j, 0)) for wd, _ in segs]
    if wt is not None:
        out_shape.append(jax.ShapeDtypeStruct((b, wt.shape[0], n), F32))
        out_specs.append(pl.BlockSpec((1, wt.shape[0], tm), lambda i, j: (i, 0, j)))
    kern = functools.partial(_proj_kernel, seg_widths=tuple(wd for wd, _ in segs),
                             rope_cols=rope_cols, has_t=wt is not None, col_chunk=512)
    return pl.pallas_call(
        kern, out_shape=out_shape, grid=(b, n // tm), in_specs=in_specs, out_specs=out_specs,
        compiler_params=_params(("parallel", "parallel")), name="in_proj",
    )(*args)


def _diff_kernel(*refs, seg_lens, tk, out_scale):
    lam_ref, q_ref, g_ref = refs[:3]
    n_seg = len(seg_lens)
    kv_refs = refs[3:3 + 2 * n_seg]
    o_ref, m_ref, acc_ref, s0_ref, s1_ref = refs[3 + 2 * n_seg:]
    qf = q_ref[0].astype(F32)
    first = _half_masks(qf.shape)
    qs = (jnp.where(first, qf, 0.0).astype(BF16), jnp.where(first, 0.0, qf).astype(BF16))
    m_ref[...] = jnp.full(m_ref.shape, NEG_BIG, F32)
    acc_ref[...] = jnp.zeros(acc_ref.shape, F32)

    def scores(kc):
        return [_dot_nt(qs[t], kc) for t in range(2)]

    def absorb(sc, vc):
        v_ones = jnp.concatenate([vc, jnp.ones(vc.shape, BF16)], axis=1)
        for t in range(2):
            m_old = m_ref[t]
            m_new = jnp.maximum(m_old, jnp.max(sc[t], axis=1, keepdims=True))
            p = jnp.exp2(sc[t] - m_new).astype(BF16)
            acc_ref[t] = jnp.exp2(m_old - m_new) * acc_ref[t] + _dot(p, v_ones)
            m_ref[t] = m_new

    def put(buf, sc):
        buf[0] = sc[0]
        buf[1] = sc[1]

    main = max(range(n_seg), key=lambda i: seg_lens[i])
    n_chunks = seg_lens[main] // tk if seg_lens[main] >= 2 * tk else 1
    if n_chunks > 1:
        assert n_chunks % 2 == 0 and seg_lens[main] == n_chunks * tk
        k_ref, v_ref = kv_refs[2 * main], kv_refs[2 * main + 1]
        rows = lambda ref, c: ref[0, pl.ds(pl.multiple_of(c * tk, tk), tk), :]
        put(s0_ref, scores(rows(k_ref, 0)))
    for s_i in range(n_seg):
        if s_i != main or n_chunks == 1:
            absorb(scores(kv_refs[2 * s_i][0]), kv_refs[2 * s_i + 1][0])
    if n_chunks > 1:
        def pair(c, last):
            put(s1_ref, scores(rows(k_ref, c + 1)))
            absorb([s0_ref[0], s0_ref[1]], rows(v_ref, c))
            if not last:
                put(s0_ref, scores(rows(k_ref, c + 2)))
            absorb([s1_ref[0], s1_ref[1]], rows(v_ref, c + 1))

        def body(j, carry):
            pair(2 * j, False)
            return carry

        lax.fori_loop(0, n_chunks // 2 - 1, body, 0)
        pair(n_chunks - 2, True)
    ratio0 = acc_ref[0, :, :LANES] / acc_ref[0, :, LANES:]
    ratio1 = acc_ref[1, :, :LANES] / acc_ref[1, :, LANES:]
    o = ratio0 - lam_ref[0] * ratio1
    o_ref[0] = (_rms(o, g_ref[...]) * out_scale).astype(o_ref.dtype)


def _diff_attention(q, kv_segs, lam, subln_g, out_scale):
    b, nq, width = q.shape
    heads = width // LANES
    tq = min(512, nq)
    in_specs = [pl.BlockSpec(memory_space=pltpu.SMEM),
                pl.BlockSpec((1, tq, LANES), lambda i, h, j: (i, j, h)),
                pl.BlockSpec((1, LANES), lambda i, h, j: (0, 0))]
    args = [lam.reshape(1).astype(F32), q, subln_g.reshape(1, LANES).astype(F32)]
    seg_lens = []
    for k, v in kv_segs:
        length = k.shape[1]
        seg_lens.append(length)
        in_specs += [pl.BlockSpec((1, length, LANES), lambda i, h, j: (i, 0, h))] * 2
        args += [k, v]
    tk = 512
    kern = functools.partial(_diff_kernel, seg_lens=tuple(seg_lens), tk=tk, out_scale=out_scale)
    return pl.pallas_call(
        kern, out_shape=jax.ShapeDtypeStruct((b, nq, width), BF16),
        grid=(b, heads, nq // tq), in_specs=in_specs,
        out_specs=pl.BlockSpec((1, tq, LANES), lambda i, h, j: (i, j, h)),
        scratch_shapes=[pltpu.VMEM((2, tq, 1), F32), pltpu.VMEM((2, tq, 2 * LANES), F32),
                        pltpu.VMEM((2, tq, tk), F32), pltpu.VMEM((2, tq, tk), F32)],
        compiler_params=_params(("parallel", "parallel", "parallel")), name="diff_attention",
    )(*args)


NA_BLOCK_ROWS = 4
NA_WIN_ROWS = NA_BLOCK_ROWS + NA_ROWS


def _na_window_start(blk, rows):
    return jnp.clip(blk * NA_BLOCK_ROWS - NA_ROWS // 2, 0, rows - NA_WIN_ROWS)


def _natten_kernel(q_ref, k_ref, v_ref, kc_ref, vc_ref, bias_ref, o_ref, *, rows):
    start = pl.multiple_of(_na_window_start(pl.program_id(1), rows) * GRID_W, GRID_W)
    n_keys = NA_WIN_ROWS * GRID_W
    nq = NA_BLOCK_ROWS * GRID_W
    q = q_ref[0].astype(F32)
    kwin = k_ref[0, pl.ds(start, n_keys), :]
    vwin = v_ref[0, pl.ds(start, n_keys), :]
    kc = kc_ref[0]
    vc = vc_ref[0]
    first = _half_masks((nq, LANES))
    for p in range(q.shape[1] // LANES):
        cols = slice(p * LANES, (p + 1) * LANES)
        qs = jnp.concatenate([jnp.where(first, q[:, cols], 0.0), jnp.where(first, 0.0, q[:, cols])],
                             axis=0).astype(BF16)
        s_nb = _dot_nt(qs, kwin[:, cols]) + jnp.concatenate([bias_ref[0, 2 * p], bias_ref[0, 2 * p + 1]], axis=0)
        s_cx = _dot_nt(qs, kc[:, cols])
        m = jnp.maximum(jnp.max(s_nb, axis=1, keepdims=True), jnp.max(s_cx, axis=1, keepdims=True))
        p_nb = jnp.exp2(s_nb - m).astype(BF16)
        p_cx = jnp.exp2(s_cx - m).astype(BF16)
        v_nb = jnp.concatenate([vwin[:, cols], jnp.ones((n_keys, LANES), BF16)], axis=1)
        v_cx = jnp.concatenate([vc[:, cols], jnp.ones((vc.shape[0], LANES), BF16)], axis=1)
        acc = _dot(p_nb, v_nb) + _dot(p_cx, v_cx)
        ratio = acc[:, :LANES] / acc[:, LANES:]
        o_ref[0, :, cols] = jnp.where(first, ratio[:nq], ratio[nq:]).astype(o_ref.dtype)


def _na_bias_table(rpb, rows):
    heads = rpb.shape[0]
    w = jnp.arange(GRID_W)
    kc = jnp.arange(GRID_W)
    col_start = jnp.clip(w - NA_COLS // 2, 0, GRID_W - NA_COLS)
    valid = (kc[None, :] >= col_start[:, None]) & (kc[None, :] < col_start[:, None] + NA_COLS)
    pad = GRID_W - NA_COLS
    padded = jnp.pad(rpb.astype(F32) * math.log2(math.e), ((0, 0), (0, 0), (pad, pad)))
    cols = jnp.stack([padded[:, :, GRID_W - 1 - wi:2 * GRID_W - 1 - wi] for wi in range(GRID_W)], axis=2)
    cols = jnp.where(valid[None, None], cols, NEG_BIG)
    masked = jnp.full((heads, GRID_W, GRID_W), NEG_BIG, F32)
    cases = []
    for first_row in (0, NA_BLOCK_ROWS, rows - NA_BLOCK_ROWS):
        win0 = min(max(first_row - NA_ROWS // 2, 0), rows - NA_WIN_ROWS)
        per_q = []
        for qr in range(NA_BLOCK_ROWS):
            r = first_row + qr
            rs = min(max(r - NA_ROWS // 2, 0), rows - NA_ROWS)
            per_k = [cols[:, win0 + kr - r + NA_ROWS - 1] if rs <= win0 + kr < rs + NA_ROWS else masked
                     for kr in range(NA_WIN_ROWS)]
            per_q.append(jnp.stack(per_k, axis=2))
        cases.append(jnp.stack(per_q, axis=1))
    return jnp.stack(cases).reshape(3, heads, NA_BLOCK_ROWS * GRID_W, NA_WIN_ROWS * GRID_W)


def _neighbourhood_attention(q, k, v, kc, vc, rpb):
    b, n, width = q.shape
    m = kc.shape[1]
    rows = n // GRID_W
    n_blk = rows // NA_BLOCK_ROWS
    assert rows % NA_BLOCK_ROWS == 0 and rows >= NA_WIN_ROWS + NA_BLOCK_ROWS
    bias = _na_bias_table(rpb, rows)
    heads = rpb.shape[0]
    nq = NA_BLOCK_ROWS * GRID_W
    kern = functools.partial(_natten_kernel, rows=rows)
    return pl.pallas_call(
        kern, out_shape=jax.ShapeDtypeStruct((b, n, width), BF16),
        grid=(b, n_blk),
        in_specs=[pl.BlockSpec((1, nq, width), lambda i, r: (i, r, 0)),
                  pl.BlockSpec((1, n, width), lambda i, r: (i, 0, 0)),
                  pl.BlockSpec((1, n, width), lambda i, r: (i, 0, 0)),
                  pl.BlockSpec((1, m, width), lambda i, r: (i, 0, 0)),
                  pl.BlockSpec((1, m, width), lambda i, r: (i, 0, 0)),
                  pl.BlockSpec((1, heads, nq, NA_WIN_ROWS * GRID_W),
                               lambda i, r: (jnp.where(r == 0, 0, jnp.where(r == n_blk - 1, 2, 1)), 0, 0, 0))],
        out_specs=pl.BlockSpec((1, nq, width), lambda i, r: (i, r, 0)),
        compiler_params=_params(("parallel", "parallel")), name="neighbourhood_attention",
    )(q, k, v, kc, vc, bias)


def _ctx_attn_kernel(q_ref, k_ref, v_ref, o_ref):
    q = q_ref[0].astype(F32)
    k = k_ref[0]
    v = v_ref[0]
    first = _half_masks((q.shape[0], LANES))
    for p in range(q.shape[1] // LANES):
        cols = slice(p * LANES, (p + 1) * LANES)
        outs = []
        for sub in range(2):
            qm = jnp.where(first if sub == 0 else ~first, q[:, cols], 0.0).astype(BF16)
            s = _dot_nt(qm, k[:, cols])
            e = jnp.exp2(s - jnp.max(s, axis=1, keepdims=True))
            outs.append(_dot(e.astype(BF16), v[:, cols]) / jnp.sum(e, axis=1, keepdims=True))
        o_ref[0, :, cols] = jnp.where(first, outs[0], outs[1]).astype(o_ref.dtype)


def _context_attention(q, k, v):
    b, m, width = q.shape
    spec = pl.BlockSpec((1, m, width), lambda i: (i, 0, 0))
    return pl.pallas_call(
        _ctx_attn_kernel, out_shape=jax.ShapeDtypeStruct((b, m, width), BF16),
        grid=(b,), in_specs=[spec, spec, spec], out_specs=spec,
        compiler_params=_params(("parallel",)), name="context_attention",
    )(q, k, v)


def _swa_kernel(sink_ref, q_ref, k_ref, v_ref, kc_ref, vc_ref, o_ref, *, n_blocks):
    blk = pl.program_id(1)
    span = 3 * Q_BLOCK
    start = pl.multiple_of(jnp.clip(blk - 1, 0, n_blocks - 3) * Q_BLOCK, Q_BLOCK)
    q = q_ref[0].astype(F32)
    kwin = k_ref[0, pl.ds(start, span), :]
    vwin = v_ref[0, pl.ds(start, span), :]
    kc = kc_ref[0]
    vc = vc_ref[0]
    first = _half_masks((Q_BLOCK, LANES))
    pairs = q.shape[1] // LANES
    n_groups = k_ref.shape[2] // LANES
    pairs_per_group = pairs // n_groups
    stack = 2 * pairs_per_group
    row = lax.broadcasted_iota(I32, (stack * Q_BLOCK, span), 0)
    qpos = blk * Q_BLOCK + (row & (Q_BLOCK - 1))
    kpos = start + lax.broadcasted_iota(I32, (stack * Q_BLOCK, span), 1)
    valid = jnp.abs(qpos - kpos) <= WINDOW
    head_of_row = lax.broadcasted_iota(I32, (stack * Q_BLOCK, 1), 0) >> (Q_BLOCK.bit_length() - 1)
    for g in range(n_groups):
        gcols = slice(g * LANES, (g + 1) * LANES)
        parts = []
        for p in range(g * pairs_per_group, (g + 1) * pairs_per_group):
            q_pair = q[:, p * LANES:(p + 1) * LANES]
            parts += [jnp.where(first, q_pair, 0.0), jnp.where(first, 0.0, q_pair)]
        qs = jnp.concatenate(parts, axis=0).astype(BF16)
        sink = jnp.zeros((stack * Q_BLOCK, 1), F32)
        for i in range(stack):
            sink = jnp.where(head_of_row == i, sink_ref[g * stack + i], sink)
        s_loc = jnp.where(valid, _dot_nt(qs, kwin[:, gcols]), NEG_BIG)
        s_cx = _dot_nt(qs, kc[:, gcols])
        m = jnp.maximum(jnp.maximum(jnp.max(s_loc, axis=1, keepdims=True),
                                    jnp.max(s_cx, axis=1, keepdims=True)), sink)
        p_loc = jnp.exp2(s_loc - m).astype(BF16)
        p_cx = jnp.exp2(s_cx - m).astype(BF16)
        v_loc = jnp.concatenate([vwin[:, gcols], jnp.ones((span, LANES), BF16)], axis=1)
        v_cx = jnp.concatenate([vc[:, gcols], jnp.ones((vc.shape[0], LANES), BF16)], axis=1)
        acc = _dot(p_loc, v_loc) + _dot(p_cx, v_cx)
        ratio = acc[:, :LANES] / (acc[:, LANES:] + jnp.exp2(sink - m))
        for i in range(pairs_per_group):
            p = g * pairs_per_group + i
            lo = ratio[2 * i * Q_BLOCK:(2 * i + 1) * Q_BLOCK]
            hi = ratio[(2 * i + 1) * Q_BLOCK:(2 * i + 2) * Q_BLOCK]
            o_ref[0, :, p * LANES:(p + 1) * LANES] = jnp.where(first, lo, hi).astype(o_ref.dtype)


def _window_attention(q, k2, v2, kc2, vc2, sinks):
    b, n, width = q.shape
    m = kc2.shape[1]
    kvw = k2.shape[2]
    n_blocks = n // Q_BLOCK
    kern = functools.partial(_swa_kernel, n_blocks=n_blocks)
    return pl.pallas_call(
        kern, out_shape=jax.ShapeDtypeStruct((b, n, width), BF16),
        grid=(b, n_blocks),
        in_specs=[pl.BlockSpec(memory_space=pltpu.SMEM),
                  pl.BlockSpec((1, Q_BLOCK, width), lambda i, j: (i, j, 0)),
                  pl.BlockSpec((1, n, kvw), lambda i, j: (i, 0, 0)),
                  pl.BlockSpec((1, n, kvw), lambda i, j: (i, 0, 0)),
                  pl.BlockSpec((1, m, kvw), lambda i, j: (i, 0, 0)),
                  pl.BlockSpec((1, m, kvw), lambda i, j: (i, 0, 0))],
        out_specs=pl.BlockSpec((1, Q_BLOCK, width), lambda i, j: (i, j, 0)),
        compiler_params=_params(("parallel", "parallel")), name="window_attention",
    )(sinks.astype(F32), q, k2, v2, kc2, vc2)


def _conv_kernel(x_ref, prev_ref, next_ref, w_ref, b_ref, o_ref):
    i = pl.program_id(1)
    tm = x_ref.shape[1]
    keep_prev = jnp.where(i > 0, 1.0, 0.0)
    keep_next = jnp.where(i < pl.num_programs(1) - 1, 1.0, 0.0)
    xe = jnp.concatenate([prev_ref[0] * keep_prev, x_ref[0], next_ref[0] * keep_next], axis=0)
    acc = jnp.zeros((tm, x_ref.shape[2]), F32) + b_ref[...]
    for k in range(D_CONV):
        off = 8 - D_CONV // 2 + k
        acc = acc + w_ref[k:k + 1, :] * xe[off:off + tm, :]
    o_ref[0] = _silu(acc)


def _conv_silu(x, conv_w, conv_b):
    b, length, ch = x.shape
    tm = min(512, length)
    per = tm // 8
    last = length // 8 - 1
    return pl.pallas_call(
        _conv_kernel, out_shape=jax.ShapeDtypeStruct((b, length, ch), F32),
        grid=(b, length // tm),
        in_specs=[pl.BlockSpec((1, tm, ch), lambda i, j: (i, j, 0)),
                  pl.BlockSpec((1, 8, ch), lambda i, j: (i, jnp.maximum(j * per - 1, 0), 0)),
                  pl.BlockSpec((1, 8, ch), lambda i, j: (i, jnp.minimum((j + 1) * per, last), 0)),
                  pl.BlockSpec((D_CONV, ch), lambda i, j: (0, 0)),
                  pl.BlockSpec((1, ch), lambda i, j: (0, 0))],
        out_specs=pl.BlockSpec((1, tm, ch), lambda i, j: (i, j, 0)),
        compiler_params=_params(("parallel", "parallel")), name="conv_silu",
    )(x, x, x, conv_w.reshape(D_CONV, ch).astype(F32), conv_b.reshape(1, ch).astype(F32))


def _ssd_kernel(*refs, reverse, finalize, d_inner, d_state, n_groups):
    u_ref, dt_ref, dtt_ref, a_ref, at_ref, bias_ref, biast_ref, init_ref = refs[:8]
    pos = 8
    if finalize:
        z_ref, yb_ref, skip_ref, gn_ref = refs[pos:pos + 4]
        pos += 4
    y_ref, fin_ref, st_ref = refs[pos:pos + 3]
    c = pl.program_id(1)

    @pl.when(c == 0)
    def _():
        st_ref[...] = init_ref[0]

    q_len = SSD_CHUNK
    u = u_ref[0]
    xs = u[:, :d_inner]
    dt = _softplus(dt_ref[0] + bias_ref[...])
    dtt = _softplus(dtt_ref[0] + biast_ref[...])
    da = dt * a_ref[...]
    dat = dtt * at_ref[...]
    row = lax.broadcasted_iota(I32, (q_len, q_len), 0)
    col = lax.broadcasted_iota(I32, (q_len, q_len), 1)
    lower = row >= col
    upper = row <= col
    causal = upper if reverse else lower
    tri_col = causal.astype(F32)
    tri_row = (lower if reverse else upper).astype(F32)
    q_col = jnp.dot(tri_col, da, precision=HIGHEST, preferred_element_type=F32)
    q_row = jnp.dot(dat, tri_row, precision=HIGHEST, preferred_element_type=F32)
    total = jnp.sum(da, axis=0, keepdims=True)
    first = _half_masks((q_len, LANES))
    heads_per_group = (d_inner // HEAD_DIM) // n_groups
    pairs_per_group = heads_per_group // 2
    y_pairs = []
    for g in range(n_groups):
        b_g = u[:, d_inner + g * d_state:d_inner + (g + 1) * d_state]
        c_g = u[:, d_inner + (n_groups + g) * d_state:d_inner + (n_groups + g + 1) * d_state]
        c_bf = c_g.astype(BF16)
        cb = _dot_nt(c_bf, b_g.astype(BF16))
        b_t = b_g.T.astype(BF16)
        for pp in range(pairs_per_group):
            p = g * pairs_per_group + pp
            h0, h1 = 2 * p, 2 * p + 1
            x_pair = xs[:, p * LANES:(p + 1) * LANES]
            dt_pair = jnp.where(first, dt[:, h0:h0 + 1], dt[:, h1:h1 + 1])
            qc_pair = jnp.where(first, q_col[:, h0:h0 + 1], q_col[:, h1:h1 + 1])
            tot_pair = jnp.where(first[:1], total[:, h0:h0 + 1], total[:, h1:h1 + 1])
            xdt = x_pair * dt_pair
            xdt_bf = xdt.astype(BF16)
            diag = []
            for h in (h0, h1):
                diff = q_col[:, h:h + 1] - q_row[h:h + 1, :]
                lmat = jnp.exp(jnp.where(causal, diff, NEG_BIG))
                diag.append(_dot((cb * lmat).astype(BF16), xdt_bf))
            y_diag = jnp.where(first, diag[0], diag[1])
            state = st_ref[p]
            y_off = _dot(c_bf, state.astype(BF16)) * jnp.exp(qc_pair)
            y_pairs.append(y_diag + y_off)
            carry_in = (xdt * jnp.exp(tot_pair - qc_pair)).astype(BF16)
            st_ref[p] = jnp.exp(tot_pair) * state + _dot(b_t, carry_in)
    y = jnp.concatenate(y_pairs, axis=1)
    if finalize:
        y = y + yb_ref[0] + skip_ref[...] * xs
        y = y * _silu(z_ref[0])
        gw = d_inner // n_groups
        y = jnp.concatenate(
            [y[:, g * gw:(g + 1) * gw] * lax.rsqrt(
                jnp.mean(y[:, g * gw:(g + 1) * gw] ** 2, axis=1, keepdims=True) + NORM_EPS)
             for g in range(n_groups)], axis=1)
        y = y * gn_ref[...]
    y_ref[0] = y.astype(y_ref.dtype)

    @pl.when(c == pl.num_programs(1) - 1)
    def _():
        fin_ref[0] = st_ref[...]


def _ssd_scan(u, dt, dtt, a, dt_bias, init, *, reverse, d_inner, d_state, n_groups, final=None):
    b, length, ch = u.shape
    heads = dt.shape[2]
    nc = length // SSD_CHUNK
    pairs = heads // 2
    cidx = (lambda j: nc - 1 - j) if reverse else (lambda j: j)
    small = lambda shape: pl.BlockSpec(shape, lambda i, j: (0,) * len(shape))
    in_specs = [pl.BlockSpec((1, SSD_CHUNK, ch), lambda i, j: (i, cidx(j), 0)),
                pl.BlockSpec((1, SSD_CHUNK, heads), lambda i, j: (i, cidx(j), 0)),
                pl.BlockSpec((1, heads, SSD_CHUNK), lambda i, j: (i, 0, cidx(j))),
                small((1, heads)), small((heads, 1)), small((1, heads)), small((heads, 1)),
                pl.BlockSpec((1, pairs, d_state, LANES), lambda i, j: (i, 0, 0, 0))]
    args = [u, dt, dtt, a.reshape(1, heads), a.reshape(heads, 1),
            dt_bias.reshape(1, heads).astype(F32), dt_bias.reshape(heads, 1).astype(F32), init]
    out_dtype = F32
    if final is not None:
        z, yb, d_skip, gnorm_g = final
        in_specs += [pl.BlockSpec((1, SSD_CHUNK, d_inner), lambda i, j: (i, cidx(j), 0)),
                     pl.BlockSpec((1, SSD_CHUNK, d_inner), lambda i, j: (i, cidx(j), 0)),
                     small((1, d_inner)), small((1, d_inner))]
        args += [z, yb, jnp.repeat(d_skip.astype(F32), HEAD_DIM).reshape(1, d_inner),
                 gnorm_g.reshape(1, d_inner).astype(F32)]
        out_dtype = BF16
    kern = functools.partial(_ssd_kernel, reverse=reverse, finalize=final is not None,
                             d_inner=d_inner, d_state=d_state, n_groups=n_groups)
    return pl.pallas_call(
        kern,
        out_shape=[jax.ShapeDtypeStruct((b, length, d_inner), out_dtype),
                   jax.ShapeDtypeStruct((b, pairs, d_state, LANES), F32)],
        grid=(b, nc), in_specs=in_specs,
        out_specs=[pl.BlockSpec((1, SSD_CHUNK, d_inner), lambda i, j: (i, cidx(j), 0)),
                   pl.BlockSpec((1, pairs, d_state, LANES), lambda i, j: (i, 0, 0, 0))],
        scratch_shapes=[pltpu.VMEM((pairs, d_state, LANES), F32)],
        compiler_params=_params(("parallel", "arbitrary")), name="ssd_scan",
    )(*args)


def _outproj_kernel(oa_ref, ob_ref, w_ref, x_ref, gpost_ref, gate_ref, gpre_ref, sc_ref, sh_ref, wrh_ref, wrl_ref,
                    xo_ref, h_ref, aff_ref, *, row_chunk):
    half = oa_ref.shape[2]
    for r0 in range(0, x_ref.shape[1], row_chunk):
        rows = slice(r0, r0 + row_chunk)
        mix = _dot(oa_ref[0, rows, :], w_ref[:half, :]) + _dot(ob_ref[0, rows, :], w_ref[half:, :])
        x_new = x_ref[0, rows, :] + gate_ref[0] * _rms(mix, gpost_ref[...])
        xo_ref[0, rows, :] = x_new
        h = _rms(x_new, gpre_ref[...]) * sc_ref[0] + sh_ref[0]
        h_ref[0, rows, :] = h
        h_hi = h.astype(BF16)
        h_lo = (h - h_hi.astype(F32)).astype(BF16)
        logits = _dot_nt(wrh_ref[...], h_hi) + _dot_nt(wrh_ref[...], h_lo) + _dot_nt(wrl_ref[...], h_hi)
        e = jnp.exp(logits - jnp.max(logits, axis=0, keepdims=True))
        aff_ref[0, :, rows] = e / jnp.sum(e, axis=0, keepdims=True)


def _out_project(oa, ob, w_out, x, g_post, gate, g_pre, scale1p, shift, w_router):
    b, n, d = x.shape
    half = oa.shape[2]
    n_exp = w_router.shape[1]
    tm = min(512, n)
    bm = gate.shape[0]
    mod_map = (lambda i, j: (i, 0, 0)) if bm > 1 else (lambda i, j: (0, 0, 0))
    row = pl.BlockSpec((1, d), lambda i, j: (0, 0))
    mod = pl.BlockSpec((1, 1, d), mod_map)
    tile = pl.BlockSpec((1, tm, d), lambda i, j: (i, j, 0))
    act = pl.BlockSpec((1, tm, half), lambda i, j: (i, j, 0))
    wr_spec = pl.BlockSpec((n_exp, d), lambda i, j: (0, 0))
    wr = w_router.T.astype(F32)
    wr_hi = wr.astype(BF16)
    wr_lo = (wr - wr_hi.astype(F32)).astype(BF16)
    return pl.pallas_call(
        functools.partial(_outproj_kernel, row_chunk=min(256, tm)),
        out_shape=[jax.ShapeDtypeStruct((b, n, d), F32), jax.ShapeDtypeStruct((b, n, d), F32),
                   jax.ShapeDtypeStruct((b, n_exp, n), F32)],
        grid=(b, n // tm),
        in_specs=[act, act, pl.BlockSpec((2 * half, d), lambda i, j: (0, 0)), tile, row, mod, row, mod, mod,
                  wr_spec, wr_spec],
        out_specs=[tile, tile, pl.BlockSpec((1, n_exp, tm), lambda i, j: (i, 0, j))],
        compiler_params=_params(("parallel", "parallel")), name="out_proj_router",
    )(oa, ob, w_out, x, g_post.reshape(1, d), gate, g_pre.reshape(1, d), scale1p, shift, wr_hi, wr_lo)


def _topk_kernel(aff_ref, idx_ref, gate_ref, work_ref, key_ref, *, cap, slot_block):
    n_exp, n = aff_ref.shape[1], aff_ref.shape[2]
    n_blk = n // LANES
    aff = aff_ref[0]

    def search(i, thr):
        cand = thr | jnp.left_shift(jnp.int32(1), 30 - i)
        cnt = jnp.sum(jnp.where(aff_ref[0] >= pltpu.bitcast(cand, F32), 1.0, 0.0), axis=1, keepdims=True)
        return jnp.where(cnt >= cap, cand, thr)

    thr = lax.fori_loop(0, 31, search, jnp.zeros((n_exp, 1), I32))
    above = aff >= pltpu.bitcast(thr + 1, F32)
    tied = (aff >= pltpu.bitcast(thr, F32)) & ~above
    need = cap - jnp.sum(jnp.where(above, 1.0, 0.0), axis=1, keepdims=True)

    r = lax.broadcasted_iota(I32, (LANES, LANES), 0)
    c = lax.broadcasted_iota(I32, (LANES, LANES), 1)
    incl = (r <= c).astype(BF16)

    def exclusive_prefix(mask_f32):
        offset = jnp.zeros((n_exp, 1), F32)
        for j in range(n_blk):
            blk = mask_f32[:, j * LANES:(j + 1) * LANES]
            run = _dot(blk.astype(BF16), incl)
            work_ref[:, j * LANES:(j + 1) * LANES] = run - blk + offset
            offset = offset + jnp.sum(blk, axis=1, keepdims=True)

    exclusive_prefix(jnp.where(tied, 1.0, 0.0))
    sel = above | (tied & (work_ref[...] < need))
    sel_f = jnp.where(sel, 1.0, 0.0)
    gate_ref[0] = jnp.where(sel, aff, 0.0)
    exclusive_prefix(sel_f)
    key = jnp.where(sel, work_ref[...], -1.0)
    for j in range(n_blk):
        key_ref[j] = key[:, j * LANES:(j + 1) * LANES]

    lane = lax.broadcasted_iota(I32, (slot_block, LANES), 1).astype(F32)
    slot = lax.broadcasted_iota(I32, (slot_block, LANES), 0).astype(F32)

    for e in range(n_exp):
        for sb in range(cap // slot_block):
            want = slot + float(sb * slot_block)

            def per_block(j, acc, e=e, want=want):
                key_row = key_ref[j, e:e + 1, :]
                tok = lane + lax.convert_element_type(j * LANES, F32)
                return acc + jnp.where(key_row == want, tok, 0.0)

            acc = lax.fori_loop(0, n_blk, per_block, jnp.zeros((slot_block, LANES), F32))
            idx_ref[0, e, sb * slot_block:(sb + 1) * slot_block, :] = (
                jnp.sum(acc, axis=1, keepdims=True).astype(I32))


def _route(aff_t):
    b, n_exp, n = aff_t.shape
    cap = CAPACITY_FACTOR * n // n_exp
    slot_block = min(LANES, cap)
    kern = functools.partial(_topk_kernel, cap=cap, slot_block=slot_block)
    idx, gate = pl.pallas_call(
        kern,
        out_shape=[jax.ShapeDtypeStruct((b, n_exp, cap, 1), I32), jax.ShapeDtypeStruct((b, n_exp, n), F32)],
        grid=(b,),
        in_specs=[pl.BlockSpec((1, n_exp, n), lambda i: (i, 0, 0))],
        out_specs=[pl.BlockSpec((1, n_exp, cap, 1), lambda i: (i, 0, 0, 0)),
                   pl.BlockSpec((1, n_exp, n), lambda i: (i, 0, 0))],
        scratch_shapes=[pltpu.VMEM((n_exp, n), F32), pltpu.VMEM((n // LANES, n_exp, LANES), F32)],
        compiler_params=_params(("parallel",)), name="expert_choice_route",
    )(aff_t)
    return idx.reshape(b * n_exp, 1, cap), gate.reshape(b * n_exp, 1, n)


def _moe_kernel(idx_ref, nxt_ref, h_ref, wg_ref, wu_ref, wd_ref, y_ref, xg_ref, *, cap, prefetch):
    sub = lax.broadcasted_iota(I32, (SUBLANES, h_ref.shape[2]), 0)

    def gather_group(src_ref, dst, base):
        tile = jnp.zeros(sub.shape, F32)
        for u in range(SUBLANES):
            t = src_ref[0, 0, base + u]
            r = t & (SUBLANES - 1)
            rows = h_ref[0, pl.ds(pl.multiple_of(t - r, SUBLANES), SUBLANES), :]
            rolled = pltpu.roll(rows, (u - r) & (SUBLANES - 1), axis=0)
            tile = jnp.where(sub == u, rolled, tile)
        dst[pl.ds(base, SUBLANES), :] = tile

    def gather_loop(dst):
        def body(i8, carry):
            gather_group(idx_ref, dst, pl.multiple_of(i8 * SUBLANES, SUBLANES))
            return carry
        lax.fori_loop(0, cap // SUBLANES, body, 0)

    if prefetch:
        e = pl.program_id(1)
        slot = e & 1

        @pl.when(e == 0)
        def _():
            gather_loop(xg_ref.at[0])
    else:
        slot = 0
        gather_loop(xg_ref.at[0])
    xg = xg_ref[slot].astype(BF16)
    gate = _dot(xg, wg_ref[0])
    up = _dot(xg, wu_ref[0])
    hid = (_silu(gate) * up).astype(BF16)
    y_ref[0, 0] = _dot(hid, wd_ref[0])
    if prefetch:
        for i8 in range(cap // SUBLANES):
            gather_group(nxt_ref, xg_ref.at[1 - slot], i8 * SUBLANES)


def _combine_kernel(idx_ref, gate_ref, y_ref, acc_ref, *, cap):
    @pl.when(pl.program_id(1) == 0)
    def _():
        acc_ref[...] = jnp.zeros_like(acc_ref)

    sub = lax.broadcasted_iota(I32, (SUBLANES, acc_ref.shape[2]), 0)

    def scatter(i8, carry):
        base = pl.multiple_of(i8 * SUBLANES, SUBLANES)
        y_tile = y_ref[0, 0, pl.ds(base, SUBLANES), :]
        for u0 in range(0, SUBLANES, 4):
            pending = []
            for u in range(u0, u0 + 4):
                t = idx_ref[0, 0, base + u]
                g = gate_ref[0, 0, t]
                r = t & (SUBLANES - 1)
                dst = acc_ref.at[0, pl.ds(pl.multiple_of(t - r, SUBLANES), SUBLANES), :]
                rolled = pltpu.roll(y_tile, (r - u) & (SUBLANES - 1), axis=0)
                pending.append((dst, dst[...] + g * rolled, sub == r))
            for dst, new, mask in pending:
                pltpu.store(dst, new, mask=mask)
        return carry

    lax.fori_loop(0, cap // SUBLANES, scatter, 0)


def _expert_ffn(h, idx, gate, wg, wu, wd):
    b, n, d = h.shape
    n_exp, _, ff = wg.shape
    cap = idx.shape[2]
    smem = lambda width: pl.BlockSpec((1, 1, width), lambda i, e: (i * n_exp + e, 0, 0), memory_space=pltpu.SMEM)
    sample_major = n * d * h.dtype.itemsize >= 3 * d * ff * wg.dtype.itemsize
    if sample_major:
        grid = (b, n_exp)
        be = lambda i, e: (i, e)
        nxt = pl.BlockSpec((1, 1, cap), lambda i, e: (i * n_exp + jnp.minimum(e + 1, n_exp - 1), 0, 0),
                           memory_space=pltpu.SMEM)
        h_spec = pl.BlockSpec((1, n, d), lambda i, e: (i, 0, 0), pipeline_mode=pl.Buffered(1))
    else:
        grid = (n_exp, b)
        be = lambda e, i: (i, e)
        nxt = pl.BlockSpec((1, 1, cap), lambda e, i: (i * n_exp + e, 0, 0), memory_space=pltpu.SMEM)
        h_spec = pl.BlockSpec((1, n, d), lambda e, i: (i, 0, 0))
    y = pl.pallas_call(
        functools.partial(_moe_kernel, cap=cap, prefetch=sample_major),
        out_shape=jax.ShapeDtypeStruct((b, n_exp, cap, d), F32),
        grid=grid,
        in_specs=[pl.BlockSpec((1, 1, cap), lambda *g: (be(*g)[0] * n_exp + be(*g)[1], 0, 0),
                               memory_space=pltpu.SMEM),
                  nxt, h_spec,
                  pl.BlockSpec((1, d, ff), lambda *g: (be(*g)[1], 0, 0)),
                  pl.BlockSpec((1, d, ff), lambda *g: (be(*g)[1], 0, 0)),
                  pl.BlockSpec((1, ff, d), lambda *g: (be(*g)[1], 0, 0))],
        out_specs=pl.BlockSpec((1, 1, cap, d), lambda *g: (*be(*g), 0, 0)),
        scratch_shapes=[pltpu.VMEM((2, cap, d), F32)],
        compiler_params=_params(("parallel", "arbitrary")), name="expert_ffn",
    )(idx, idx, h, wg, wu, wd)
    return pl.pallas_call(
        functools.partial(_combine_kernel, cap=cap),
        out_shape=jax.ShapeDtypeStruct((b, n, d), F32),
        grid=(b, n_exp),
        in_specs=[smem(cap), smem(n), pl.BlockSpec((1, 1, cap, d), lambda i, e: (i, e, 0, 0))],
        out_specs=pl.BlockSpec((1, n, d), lambda i, e: (i, 0, 0)),
        compiler_params=_params(("parallel", "arbitrary")), name="expert_combine",
    )(idx, gate, y)


def _residual_kernel(x_ref, f_ref, g_ref, gate_ref, o_ref):
    o_ref[0] = x_ref[0] + gate_ref[0] * _rms(f_ref[0], g_ref[...])


def _gated_residual(x, f, g, gate):
    b, n, d = x.shape
    tm = min(512, n)
    bm = gate.shape[0]
    mod_map = (lambda i, j: (i, 0, 0)) if bm > 1 else (lambda i, j: (0, 0, 0))
    tile = pl.BlockSpec((1, tm, d), lambda i, j: (i, j, 0))
    return pl.pallas_call(
        _residual_kernel, out_shape=jax.ShapeDtypeStruct((b, n, d), F32),
        grid=(b, n // tm),
        in_specs=[tile, tile, pl.BlockSpec((1, d), lambda i, j: (0, 0)), pl.BlockSpec((1, 1, d), mod_map)],
        out_specs=tile,
        compiler_params=_params(("parallel", "parallel")), name="gated_residual",
    )(x, f, g.reshape(1, d), gate)


def _rope_tables(n):
    t = jnp.arange(n, dtype=I32)
    row = (t // GRID_W).astype(F32)
    col = (t % GRID_W).astype(F32)
    n_freq = HEAD_DIM // 4
    inv_freq = ROPE_THETA ** (-jnp.arange(n_freq, dtype=F32) / n_freq)
    ang = jnp.concatenate([row[:, None] * inv_freq, col[:, None] * inv_freq], axis=-1)
    cos, sin = jnp.cos(ang), jnp.sin(ang)
    cos_t = jnp.tile(jnp.concatenate([cos, cos], axis=-1), (1, LANES // HEAD_DIM))
    sin_t = jnp.tile(jnp.concatenate([-sin, sin], axis=-1), (1, LANES // HEAD_DIM))
    return cos_t, sin_t


def _deinterleave_heads(w):
    d, width = w.shape
    w = w.reshape(d, width // HEAD_DIM, HEAD_DIM // 2, 2)
    return jnp.concatenate([w[..., 0], w[..., 1]], axis=-1).reshape(d, width)


def _dup_heads(w):
    d, width = w.shape
    w = w.reshape(d, width // HEAD_DIM, 1, HEAD_DIM)
    return jnp.concatenate([w, w], axis=2).reshape(d, 2 * width)


def _mod_split(mod_rows):
    sh1, sc1, gt1, sh2, sc2, gt2 = jnp.split(mod_rows[:, None, :], 6, axis=-1)
    return sh1, 1.0 + sc1, gt1, sh2, 1.0 + sc2, gt2


def _moe_block(x_mid, h2, aff_t, g_post_ffn, gt2, wg, wu, wd):
    idx, gate = _route(aff_t)
    ffn = _expert_ffn(h2, idx, gate, wg, wu, wd)
    return _gated_residual(x_mid, ffn, g_post_ffn, gt2)


def kernel(x, c, ctx, c_ctx, l0_w_mod, l0_b_mod, l0_g_pre_mix, l0_g_post_mix, l0_g_pre_ffn, l0_g_post_ffn, l0_w_in, l0_w_out, l0_lam_q1, l0_lam_k1, l0_lam_q2, l0_lam_k2, l0_subln_g, l0_rpb, l0_w_router, l0_w_gate, l0_w_up, l0_w_down, l1_w_mod, l1_b_mod, l1_g_pre_mix, l1_g_post_mix, l1_g_pre_ffn, l1_g_post_ffn, l1_w_in, l1_w_out, l1_sinks, l1_conv_w, l1_conv_b, l1_a_log_f, l1_a_log_b, l1_dt_bias_f, l1_dt_bias_b, l1_d_skip, l1_gnorm_g, l1_w_router, l1_w_gate, l1_w_up, l1_w_down):
    b, n, d = x.shape
    m = ctx.shape[1]
    half = d // 2
    scale = HEAD_DIM ** -0.5 * math.log2(math.e)
    cos_t, sin_t = _rope_tables(n)
    cond = jnp.zeros((16, d), F32).at[:b].set(c).at[b].set(c_ctx)

    mod = _modulation(cond, l0_w_mod, l0_b_mod)
    sh1, sc1, gt1, sh2, sc2, gt2 = _mod_split(mod[:b])
    csh1, csc1, cgt1, csh2, csc2, cgt2 = _mod_split(mod[b:b + 1])

    a_qk = half
    wq, wk, rest = l0_w_in[:, :a_qk], l0_w_in[:, a_qk:2 * a_qk], l0_w_in[:, 2 * a_qk:]
    qb_cols = slice(half, 2 * half)
    rest = rest.at[:, qb_cols].multiply(scale)
    w_in0 = jnp.concatenate([_deinterleave_heads(wq) * scale, _deinterleave_heads(wk), rest], axis=1).astype(BF16)
    segs0 = [(half, BF16)] * 6
    qa, ka, va, qb, kb, vb = _project(x, l0_g_pre_mix, sc1, sh1, w_in0, segs0, rope=(2 * a_qk, cos_t, sin_t))
    qa_c, ka_c, va_c, qb_c, kb_c, vb_c = _project(ctx, l0_g_pre_mix, csc1, csh1, w_in0, segs0)

    lam_init = 0.8 - 0.6 * math.exp(-0.3 * 0)
    lam = (jnp.exp(jnp.sum(l0_lam_q1.astype(F32) * l0_lam_k1.astype(F32)))
           - jnp.exp(jnp.sum(l0_lam_q2.astype(F32) * l0_lam_k2.astype(F32))) + lam_init)
    o_a = _diff_attention(qa, [(ka, va), (ka_c, va_c)], lam, l0_subln_g, 1.0 - lam_init)
    o_b = _neighbourhood_attention(qb, kb, vb, kb_c, vb_c, l0_rpb)
    oc_a = _diff_attention(qa_c, [(ka_c, va_c)], lam, l0_subln_g, 1.0 - lam_init)
    oc_b = _context_attention(qb_c, kb_c, vb_c)

    w_out0 = l0_w_out.astype(BF16)
    wg0, wu0, wd0 = l0_w_gate.astype(BF16), l0_w_up.astype(BF16), l0_w_down.astype(BF16)
    x_mid, h2, aff_t = _out_project(o_a, o_b, w_out0, x, l0_g_post_mix, gt1, l0_g_pre_ffn, sc2, sh2, l0_w_router)
    x1 = _moe_block(x_mid, h2, aff_t, l0_g_post_ffn, gt2, wg0, wu0, wd0)
    c_mid, hc2, aff_c = _out_project(oc_a, oc_b, w_out0, ctx, l0_g_post_mix, cgt1, l0_g_pre_ffn, csc2, csh2,
                                     l0_w_router)
    ctx1 = _moe_block(c_mid, hc2, aff_c, l0_g_post_ffn, cgt2, wg0, wu0, wd0)

    mod = _modulation(cond, l1_w_mod, l1_b_mod)
    sh1, sc1, gt1, sh2, sc2, gt2 = _mod_split(mod[:b])
    csh1, csc1, _, _, _, _ = _mod_split(mod[b:b + 1])

    c_q = half
    d_inner = half
    d_heads = l1_a_log_f.shape[0]
    d_xbc = l1_conv_w.shape[2]
    c_kv = (l1_w_in.shape[1] - c_q - d_xbc - d_inner - 2 * d_heads) // 2
    n_groups = 2
    d_state = (d_xbc - d_inner) // (2 * n_groups)
    wq = l1_w_in[:, :c_q]
    wk = l1_w_in[:, c_q:c_q + c_kv]
    wv = l1_w_in[:, c_q + c_kv:c_q + 2 * c_kv]
    o_x = c_q + 2 * c_kv
    w_x = l1_w_in[:, o_x:o_x + d_xbc + d_inner]
    w_dt = l1_w_in[:, o_x + d_xbc + d_inner:]
    w_kv = jnp.concatenate([_dup_heads(_deinterleave_heads(wk)), _dup_heads(wv), w_x, w_dt], axis=1)
    w_lat = jnp.concatenate([_deinterleave_heads(wq) * scale, w_kv], axis=1).astype(BF16)
    w_ctx = w_kv.astype(BF16)
    w_dt_t = w_dt.T.astype(BF16)
    segs_lat = [(c_q, BF16), (2 * c_kv, BF16), (2 * c_kv, BF16), (d_xbc, F32), (d_inner, F32), (2 * d_heads, F32)]
    q, k2, v2, xbc, z, dt, dtt = _project(x1, l1_g_pre_mix, sc1, sh1, w_lat, segs_lat,
                                          rope=(c_q + 2 * c_kv, cos_t, sin_t), wt=w_dt_t)
    k2_c, v2_c, xbc_c, _, dt_c, dtt_c = _project(ctx1, l1_g_pre_mix, csc1, csh1, w_ctx, segs_lat[1:], wt=w_dt_t)

    o_c = _window_attention(q, k2, v2, k2_c, v2_c, l1_sinks.astype(F32) * math.log2(math.e))

    a_f = -jnp.exp(l1_a_log_f.astype(F32))
    a_b = -jnp.exp(l1_a_log_b.astype(F32))
    u = _conv_silu(xbc, l1_conv_w, l1_conv_b)
    u_c = _conv_silu(xbc_c, l1_conv_w, l1_conv_b)
    ssd = functools.partial(_ssd_scan, d_inner=d_inner, d_state=d_state, n_groups=n_groups)
    zero_state = jnp.zeros((b, d_heads // 2, d_state, LANES), F32)
    _, sc_f = ssd(u_c, dt_c[..., :d_heads], dtt_c[:, :d_heads], a_f, l1_dt_bias_f, zero_state, reverse=False)
    _, sc_b = ssd(u_c, dt_c[..., d_heads:], dtt_c[:, d_heads:], a_b, l1_dt_bias_b, zero_state, reverse=True)
    y_b, _ = ssd(u, dt[..., d_heads:], dtt[:, d_heads:], a_b, l1_dt_bias_b, sc_b, reverse=True)
    o_d, _ = ssd(u, dt[..., :d_heads], dtt[:, :d_heads], a_f, l1_dt_bias_f, sc_f, reverse=False,
                 final=(z, y_b, l1_d_skip, l1_gnorm_g))

    x_mid, h2, aff_t = _out_project(o_c, o_d, l1_w_out.astype(BF16), x1, l1_g_post_mix, gt1, l1_g_pre_ffn, sc2, sh2,
                                    l1_w_router)
    return _moe_block(x_mid, h2, aff_t, l1_g_post_ffn, gt2, l1_w_gate.astype(BF16), l1_w_up.astype(BF16),
                      l1_w_down.astype(BF16))
```

```python
import functools
import math

import jax
import jax.numpy as jnp
from jax import lax
from jax.experimental import pallas as pl
from jax.experimental.pallas import tpu as pltpu

F32 = jnp.float32
BF16 = jnp.bfloat16
I32 = jnp.int32

HEAD_DIM = 64
LANES = 128
SUBLANES = 8
GRID_W = 64
ROPE_THETA = 10000.0
NORM_EPS = 1e-6
NA_ROWS = 8
NA_COLS = 16
WINDOW = 128
Q_BLOCK = 128
SSD_CHUNK = 128
D_CONV = 5
N_EXPERTS = 16
CAPACITY_FACTOR = 2
NEG_BIG = -1e30
VMEM_LIMIT = 56 * 1024 * 1024
HIGHEST = lax.Precision.HIGHEST
NT_DIMS = (((1,), (1,)), ((), ()))


def _params(sem, vmem=VMEM_LIMIT):
    return pltpu.CompilerParams(dimension_semantics=sem, vmem_limit_bytes=vmem)


def _rms(u, g):
    return u * lax.rsqrt(jnp.mean(u * u, axis=-1, keepdims=True) + NORM_EPS) * g


def _silu(u):
    return u * (1.0 / (1.0 + jnp.exp(-u)))


def _softplus(u):
    return jnp.maximum(u, 0.0) + jnp.log(1.0 + jnp.exp(-jnp.abs(u)))


def _dot(a, b):
    return jnp.dot(a, b, preferred_element_type=F32)


def _dot_nt(a, b):
    return lax.dot_general(a, b, NT_DIMS, preferred_element_type=F32)


def _half_masks(shape):
    lane = lax.broadcasted_iota(I32, shape, 1)
    return (lane % LANES) < HEAD_DIM


def _mod_kernel(c_ref, w_ref, b_ref, o_ref):
    s = _silu(c_ref[...]).astype(BF16)
    o_ref[...] = _dot(s, w_ref[...].astype(BF16)) + b_ref[...]


def _modulation(cond, w_mod, b_mod):
    rows, d = cond.shape
    n_out = w_mod.shape[1]
    tn = 1024
    return pl.pallas_call(
        _mod_kernel,
        out_shape=jax.ShapeDtypeStruct((rows, n_out), F32),
        grid=(n_out // tn,),
        in_specs=[pl.BlockSpec((rows, d), lambda j: (0, 0)),
                  pl.BlockSpec((d, tn), lambda j: (0, j)),
                  pl.BlockSpec((1, tn), lambda j: (0, j))],
        out_specs=pl.BlockSpec((rows, tn), lambda j: (0, j)),
        compiler_params=_params(("parallel",)),
        name="modulation",
    )(cond, w_mod, b_mod.reshape(1, n_out))


def _rope_block(u, cos, sin_signed, first_half):
    width = u.shape[1]
    fwd = pltpu.roll(u, HEAD_DIM // 2, axis=1)
    bwd = pltpu.roll(u, width - HEAD_DIM // 2, axis=1)
    return u * cos + jnp.where(first_half, bwd, fwd) * sin_signed


def _proj_kernel(*refs, seg_widths, rope_cols, has_t, col_chunk):
    x_ref, g_ref, sc_ref, sh_ref, w_ref = refs[:5]
    pos = 5
    if rope_cols:
        cos_ref, sin_ref = refs[pos:pos + 2]
        pos += 2
    if has_t:
        wt_ref = refs[pos]
        pos += 1
    out_refs = refs[pos:]
    h = (_rms(x_ref[0], g_ref[...]) * sc_ref[0] + sh_ref[0]).astype(BF16)
    if rope_cols:
        cos = cos_ref[...]
        sin_signed = sin_ref[...]
        first_half = (lax.broadcasted_iota(I32, cos.shape, 1) % HEAD_DIM) < HEAD_DIM // 2
    col = 0
    for o_ref, width in zip(out_refs, seg_widths):
        for c0 in range(0, width, col_chunk):
            cw = min(col_chunk, width - c0)
            acc = _dot(h, w_ref[:, col + c0:col + c0 + cw])
            if col + c0 < rope_cols:
                acc = jnp.concatenate(
                    [_rope_block(acc[:, k:k + LANES], cos, sin_signed, first_half)
                     for k in range(0, cw, LANES)], axis=1)
            o_ref[0, :, c0:c0 + cw] = acc.astype(o_ref.dtype)
        col += width
    if has_t:
        out_refs[-1][0] = _dot_nt(wt_ref[...], h)


def _project(x, g, scale1p, shift, w, segs, rope=None, wt=None):
    b, n, d = x.shape
    tm = min(512, n)
    bm = scale1p.shape[0]
    mod_map = (lambda i, j: (i, 0, 0)) if bm > 1 else (lambda i, j: (0, 0, 0))
    in_specs = [pl.BlockSpec((1, tm, d), lambda i, j: (i, j, 0)),
                pl.BlockSpec((1, d), lambda i, j: (0, 0)),
                pl.BlockSpec((1, 1, d), mod_map),
                pl.BlockSpec((1, 1, d), mod_map),
                pl.BlockSpec(w.shape, lambda i, j: (0, 0))]
    args = [x, g.reshape(1, d), scale1p, shift, w]
    rope_cols = 0
    if rope is not None:
        rope_cols, cos, sin_signed = rope
        in_specs += [pl.BlockSpec((tm, LANES), lambda i, j: (j, 0))] * 2
        args += [cos, sin_signed]
    if wt is not None:
        in_specs.append(pl.BlockSpec(wt.shape, lambda i, j: (0, 0)))
        args.append(wt)
    out_shape = [jax.ShapeDtypeStruct((b, n, wd), dt) for wd, dt in segs]
    out_specs = [pl.BlockSpec((1, tm, wd), lambda i, j: (i, j, 0)) for wd, _ in segs]
    if wt is not None:
        out_shape.append(jax.ShapeDtypeStruct((b, wt.shape[0], n), F32))
        out_specs.append(pl.BlockSpec((1, wt.shape[0], tm), lambda i, j: (i, 0, j)))
    kern = functools.partial(_proj_kernel, seg_widths=tuple(wd for wd, _ in segs),
                             rope_cols=rope_cols, has_t=wt is not None, col_chunk=512)
    return pl.pallas_call(
        kern, out_shape=out_shape, grid=(b, n // tm), in_specs=in_specs, out_specs=out_specs,
        compiler_params=_params(("parallel", "parallel")), name="in_proj",
    )(*args)


def _diff_kernel(*refs, seg_lens, tk, out_scale):
    lam_ref, q_ref, g_ref = refs[:3]
    n_seg = len(seg_lens)
    kv_refs = refs[3:3 + 2 * n_seg]
    o_ref, m_ref, acc_ref, s0_ref, s1_ref = refs[3 + 2 * n_seg:]
    qf = q_ref[0].astype(F32)
    first = _half_masks(qf.shape)
    qs = (jnp.where(first, qf, 0.0).astype(BF16), jnp.where(first, 0.0, qf).astype(BF16))
    m_ref[...] = jnp.full(m_ref.shape, NEG_BIG, F32)
    acc_ref[...] = jnp.zeros(acc_ref.shape, F32)

    def scores(kc):
        return [_dot_nt(qs[t], kc) for t in range(2)]

    def absorb(sc, vc):
        v_ones = jnp.concatenate([vc, jnp.ones(vc.shape, BF16)], axis=1)
        for t in range(2):
            m_old = m_ref[t]
            m_new = jnp.maximum(m_old, jnp.max(sc[t], axis=1, keepdims=True))
            p = jnp.exp2(sc[t] - m_new).astype(BF16)
            acc_ref[t] = jnp.exp2(m_old - m_new) * acc_ref[t] + _dot(p, v_ones)
            m_ref[t] = m_new

    def put(buf, sc):
        buf[0] = sc[0]
        buf[1] = sc[1]

    main = max(range(n_seg), key=lambda i: seg_lens[i])
    n_chunks = seg_lens[main] // tk if seg_lens[main] >= 2 * tk else 1
    if n_chunks > 1:
        assert n_chunks % 2 == 0 and seg_lens[main] == n_chunks * tk
        k_ref, v_ref = kv_refs[2 * main], kv_refs[2 * main + 1]
        rows = lambda ref, c: ref[0, pl.ds(pl.multiple_of(c * tk, tk), tk), :]
    singles = [s_i for s_i in range(n_seg) if s_i != main or n_chunks == 1]
    single_scores = [scores(kv_refs[2 * s_i][0]) for s_i in singles]
    if n_chunks > 1:
        put(s0_ref, scores(rows(k_ref, 0)))
    for s_i, sc in zip(singles, single_scores):
        absorb(sc, kv_refs[2 * s_i + 1][0])
    if n_chunks > 1:
        def pair(c, last):
            put(s1_ref, scores(rows(k_ref, c + 1)))
            absorb([s0_ref[0], s0_ref[1]], rows(v_ref, c))
            if not last:
                put(s0_ref, scores(rows(k_ref, c + 2)))
            absorb([s1_ref[0], s1_ref[1]], rows(v_ref, c + 1))

        def body(j, carry):
            pair(2 * j, False)
            return carry

        lax.fori_loop(0, n_chunks // 2 - 1, body, 0)
        pair(n_chunks - 2, True)
    ratio0 = acc_ref[0, :, :LANES] / acc_ref[0, :, LANES:]
    ratio1 = acc_ref[1, :, :LANES] / acc_ref[1, :, LANES:]
    o = ratio0 - lam_ref[0] * ratio1
    o_ref[0] = (_rms(o, g_ref[...]) * out_scale).astype(o_ref.dtype)


def _diff_attention(q, kv_segs, lam, subln_g, out_scale):
    b, nq, width = q.shape
    heads = width // LANES
    tq = min(512, nq)
    in_specs = [pl.BlockSpec(memory_space=pltpu.SMEM),
                pl.BlockSpec((1, tq, LANES), lambda i, h, j: (i, j, h)),
                pl.BlockSpec((1, LANES), lambda i, h, j: (0, 0))]
    args = [lam.reshape(1).astype(F32), q, subln_g.reshape(1, LANES).astype(F32)]
    seg_lens = []
    for k, v in kv_segs:
        length = k.shape[1]
        seg_lens.append(length)
        in_specs += [pl.BlockSpec((1, length, LANES), lambda i, h, j: (i, 0, h))] * 2
        args += [k, v]
    tk = 512
    kern = functools.partial(_diff_kernel, seg_lens=tuple(seg_lens), tk=tk, out_scale=out_scale)
    return pl.pallas_call(
        kern, out_shape=jax.ShapeDtypeStruct((b, nq, width), BF16),
        grid=(b, heads, nq // tq), in_specs=in_specs,
        out_specs=pl.BlockSpec((1, tq, LANES), lambda i, h, j: (i, j, h)),
        scratch_shapes=[pltpu.VMEM((2, tq, 1), F32), pltpu.VMEM((2, tq, 2 * LANES), F32),
                        pltpu.VMEM((2, tq, tk), F32), pltpu.VMEM((2, tq, tk), F32)],
        compiler_params=_params(("parallel", "parallel", "parallel")), name="diff_attention",
    )(*args)


NA_BLOCK_ROWS = 4
NA_WIN_ROWS = NA_BLOCK_ROWS + NA_ROWS


def _na_window_start(blk, rows):
    return jnp.clip(blk * NA_BLOCK_ROWS - NA_ROWS // 2, 0, rows - NA_WIN_ROWS)


def _natten_kernel(q_ref, k_ref, v_ref, kc_ref, vc_ref, bias_ref, o_ref, *, rows):
    start = pl.multiple_of(_na_window_start(pl.program_id(1), rows) * GRID_W, GRID_W)
    n_keys = NA_WIN_ROWS * GRID_W
    nq = NA_BLOCK_ROWS * GRID_W
    q = q_ref[0].astype(F32)
    kwin = k_ref[0, pl.ds(start, n_keys), :]
    vwin = v_ref[0, pl.ds(start, n_keys), :]
    kc = kc_ref[0]
    vc = vc_ref[0]
    first = _half_masks((nq, LANES))
    for p in range(q.shape[1] // LANES):
        cols = slice(p * LANES, (p + 1) * LANES)
        qs = jnp.concatenate([jnp.where(first, q[:, cols], 0.0), jnp.where(first, 0.0, q[:, cols])],
                             axis=0).astype(BF16)
        s_nb = _dot_nt(qs, kwin[:, cols]) + jnp.concatenate([bias_ref[0, 2 * p], bias_ref[0, 2 * p + 1]], axis=0)
        s_cx = _dot_nt(qs, kc[:, cols])
        m = jnp.maximum(jnp.max(s_nb, axis=1, keepdims=True), jnp.max(s_cx, axis=1, keepdims=True))
        p_nb = jnp.exp2(s_nb - m).astype(BF16)
        p_cx = jnp.exp2(s_cx - m).astype(BF16)
        v_nb = jnp.concatenate([vwin[:, cols], jnp.ones((n_keys, LANES), BF16)], axis=1)
        v_cx = jnp.concatenate([vc[:, cols], jnp.ones((vc.shape[0], LANES), BF16)], axis=1)
        acc = _dot(p_nb, v_nb) + _dot(p_cx, v_cx)
        ratio = acc[:, :LANES] / acc[:, LANES:]
        o_ref[0, :, cols] = jnp.where(first, ratio[:nq], ratio[nq:]).astype(o_ref.dtype)


def _na_bias_table(rpb, rows):
    heads = rpb.shape[0]
    w = jnp.arange(GRID_W)
    kc = jnp.arange(GRID_W)
    col_start = jnp.clip(w - NA_COLS // 2, 0, GRID_W - NA_COLS)
    valid = (kc[None, :] >= col_start[:, None]) & (kc[None, :] < col_start[:, None] + NA_COLS)
    pad = GRID_W - NA_COLS
    padded = jnp.pad(rpb.astype(F32) * math.log2(math.e), ((0, 0), (0, 0), (pad, pad)))
    cols = jnp.stack([padded[:, :, GRID_W - 1 - wi:2 * GRID_W - 1 - wi] for wi in range(GRID_W)], axis=2)
    cols = jnp.where(valid[None, None], cols, NEG_BIG)
    masked = jnp.full((heads, GRID_W, GRID_W), NEG_BIG, F32)
    cases = []
    for first_row in (0, NA_BLOCK_ROWS, rows - NA_BLOCK_ROWS):
        win0 = min(max(first_row - NA_ROWS // 2, 0), rows - NA_WIN_ROWS)
        per_q = []
        for qr in range(NA_BLOCK_ROWS):
            r = first_row + qr
            rs = min(max(r - NA_ROWS // 2, 0), rows - NA_ROWS)
            per_k = [cols[:, win0 + kr - r + NA_ROWS - 1] if rs <= win0 + kr < rs + NA_ROWS else masked
                     for kr in range(NA_WIN_ROWS)]
            per_q.append(jnp.stack(per_k, axis=2))
        cases.append(jnp.stack(per_q, axis=1))
    return jnp.stack(cases).reshape(3, heads, NA_BLOCK_ROWS * GRID_W, NA_WIN_ROWS * GRID_W)


def _neighbourhood_attention(q, k, v, kc, vc, rpb):
    b, n, width = q.shape
    m = kc.shape[1]
    rows = n // GRID_W
    n_blk = rows // NA_BLOCK_ROWS
    assert rows % NA_BLOCK_ROWS == 0 and rows >= NA_WIN_ROWS + NA_BLOCK_ROWS
    bias = _na_bias_table(rpb, rows)
    heads = rpb.shape[0]
    nq = NA_BLOCK_ROWS * GRID_W
    kern = functools.partial(_natten_kernel, rows=rows)
    return pl.pallas_call(
        kern, out_shape=jax.ShapeDtypeStruct((b, n, width), BF16),
        grid=(b, n_blk),
        in_specs=[pl.BlockSpec((1, nq, width), lambda i, r: (i, r, 0)),
                  pl.BlockSpec((1, n, width), lambda i, r: (i, 0, 0)),
                  pl.BlockSpec((1, n, width), lambda i, r: (i, 0, 0)),
                  pl.BlockSpec((1, m, width), lambda i, r: (i, 0, 0)),
                  pl.BlockSpec((1, m, width), lambda i, r: (i, 0, 0)),
                  pl.BlockSpec((1, heads, nq, NA_WIN_ROWS * GRID_W),
                               lambda i, r: (jnp.where(r == 0, 0, jnp.where(r == n_blk - 1, 2, 1)), 0, 0, 0))],
        out_specs=pl.BlockSpec((1, nq, width), lambda i, r: (i, r, 0)),
        compiler_params=_params(("parallel", "parallel")), name="neighbourhood_attention",
    )(q, k, v, kc, vc, bias)


def _ctx_attn_kernel(q_ref, k_ref, v_ref, o_ref):
    q = q_ref[0].astype(F32)
    k = k_ref[0]
    v = v_ref[0]
    first = _half_masks((q.shape[0], LANES))
    for p in range(q.shape[1] // LANES):
        cols = slice(p * LANES, (p + 1) * LANES)
        outs = []
        for sub in range(2):
            qm = jnp.where(first if sub == 0 else ~first, q[:, cols], 0.0).astype(BF16)
            s = _dot_nt(qm, k[:, cols])
            e = jnp.exp2(s - jnp.max(s, axis=1, keepdims=True))
            outs.append(_dot(e.astype(BF16), v[:, cols]) / jnp.sum(e, axis=1, keepdims=True))
        o_ref[0, :, cols] = jnp.where(first, outs[0], outs[1]).astype(o_ref.dtype)


def _context_attention(q, k, v):
    b, m, width = q.shape
    spec = pl.BlockSpec((1, m, width), lambda i: (i, 0, 0))
    return pl.pallas_call(
        _ctx_attn_kernel, out_shape=jax.ShapeDtypeStruct((b, m, width), BF16),
        grid=(b,), in_specs=[spec, spec, spec], out_specs=spec,
        compiler_params=_params(("parallel",)), name="context_attention",
    )(q, k, v)


def _swa_kernel(sink_ref, q_ref, k_ref, v_ref, kc_ref, vc_ref, o_ref, *, n_blocks):
    blk = pl.program_id(1)
    span = 3 * Q_BLOCK
    start = pl.multiple_of(jnp.clip(blk - 1, 0, n_blocks - 3) * Q_BLOCK, Q_BLOCK)
    q = q_ref[0].astype(F32)
    kwin = k_ref[0, pl.ds(start, span), :]
    vwin = v_ref[0, pl.ds(start, span), :]
    kc = kc_ref[0]
    vc = vc_ref[0]
    first = _half_masks((Q_BLOCK, LANES))
    pairs = q.shape[1] // LANES
    n_groups = k_ref.shape[2] // LANES
    pairs_per_group = pairs // n_groups
    stack = 2 * pairs_per_group
    row = lax.broadcasted_iota(I32, (stack * Q_BLOCK, span), 0)
    qpos = blk * Q_BLOCK + (row & (Q_BLOCK - 1))
    kpos = start + lax.broadcasted_iota(I32, (stack * Q_BLOCK, span), 1)
    valid = jnp.abs(qpos - kpos) <= WINDOW
    head_of_row = lax.broadcasted_iota(I32, (stack * Q_BLOCK, 1), 0) >> (Q_BLOCK.bit_length() - 1)
    for g in range(n_groups):
        gcols = slice(g * LANES, (g + 1) * LANES)
        parts = []
        for p in range(g * pairs_per_group, (g + 1) * pairs_per_group):
            q_pair = q[:, p * LANES:(p + 1) * LANES]
            parts += [jnp.where(first, q_pair, 0.0), jnp.where(first, 0.0, q_pair)]
        qs = jnp.concatenate(parts, axis=0).astype(BF16)
        sink = jnp.zeros((stack * Q_BLOCK, 1), F32)
        for i in range(stack):
            sink = jnp.where(head_of_row == i, sink_ref[g * stack + i], sink)
        s_loc = jnp.where(valid, _dot_nt(qs, kwin[:, gcols]), NEG_BIG)
        s_cx = _dot_nt(qs, kc[:, gcols])
        m = jnp.maximum(jnp.maximum(jnp.max(s_loc, axis=1, keepdims=True),
                                    jnp.max(s_cx, axis=1, keepdims=True)), sink)
        p_loc = jnp.exp2(s_loc - m).astype(BF16)
        p_cx = jnp.exp2(s_cx - m).astype(BF16)
        v_loc = jnp.concatenate([vwin[:, gcols], jnp.ones((span, LANES), BF16)], axis=1)
        v_cx = jnp.concatenate([vc[:, gcols], jnp.ones((vc.shape[0], LANES), BF16)], axis=1)
        acc = _dot(p_loc, v_loc) + _dot(p_cx, v_cx)
        ratio = acc[:, :LANES] / (acc[:, LANES:] + jnp.exp2(sink - m))
        for i in range(pairs_per_group):
            p = g * pairs_per_group + i
            lo = ratio[2 * i * Q_BLOCK:(2 * i + 1) * Q_BLOCK]
            hi = ratio[(2 * i + 1) * Q_BLOCK:(2 * i + 2) * Q_BLOCK]
            o_ref[0, :, p * LANES:(p + 1) * LANES] = jnp.where(first, lo, hi).astype(o_ref.dtype)


def _window_attention(q, k2, v2, kc2, vc2, sinks):
    b, n, width = q.shape
    m = kc2.shape[1]
    kvw = k2.shape[2]
    n_blocks = n // Q_BLOCK
    kern = functools.partial(_swa_kernel, n_blocks=n_blocks)
    return pl.pallas_call(
        kern, out_shape=jax.ShapeDtypeStruct((b, n, width), BF16),
        grid=(b, n_blocks),
        in_specs=[pl.BlockSpec(memory_space=pltpu.SMEM),
                  pl.BlockSpec((1, Q_BLOCK, width), lambda i, j: (i, j, 0)),
                  pl.BlockSpec((1, n, kvw), lambda i, j: (i, 0, 0)),
                  pl.BlockSpec((1, n, kvw), lambda i, j: (i, 0, 0)),
                  pl.BlockSpec((1, m, kvw), lambda i, j: (i, 0, 0)),
                  pl.BlockSpec((1, m, kvw), lambda i, j: (i, 0, 0))],
        out_specs=pl.BlockSpec((1, Q_BLOCK, width), lambda i, j: (i, j, 0)),
        compiler_params=_params(("parallel", "parallel")), name="window_attention",
    )(sinks.astype(F32), q, k2, v2, kc2, vc2)


def _conv_kernel(x_ref, prev_ref, next_ref, w_ref, b_ref, o_ref):
    i = pl.program_id(1)
    tm = x_ref.shape[1]
    keep_prev = jnp.where(i > 0, 1.0, 0.0)
    keep_next = jnp.where(i < pl.num_programs(1) - 1, 1.0, 0.0)
    xe = jnp.concatenate([prev_ref[0] * keep_prev, x_ref[0], next_ref[0] * keep_next], axis=0)
    acc = jnp.zeros((tm, x_ref.shape[2]), F32) + b_ref[...]
    for k in range(D_CONV):
        off = 8 - D_CONV // 2 + k
        acc = acc + w_ref[k:k + 1, :] * xe[off:off + tm, :]
    o_ref[0] = _silu(acc)


def _conv_silu(x, conv_w, conv_b):
    b, length, ch = x.shape
    tm = min(512, length)
    per = tm // 8
    last = length // 8 - 1
    return pl.pallas_call(
        _conv_kernel, out_shape=jax.ShapeDtypeStruct((b, length, ch), F32),
        grid=(b, length // tm),
        in_specs=[pl.BlockSpec((1, tm, ch), lambda i, j: (i, j, 0)),
                  pl.BlockSpec((1, 8, ch), lambda i, j: (i, jnp.maximum(j * per - 1, 0), 0)),
                  pl.BlockSpec((1, 8, ch), lambda i, j: (i, jnp.minimum((j + 1) * per, last), 0)),
                  pl.BlockSpec((D_CONV, ch), lambda i, j: (0, 0)),
                  pl.BlockSpec((1, ch), lambda i, j: (0, 0))],
        out_specs=pl.BlockSpec((1, tm, ch), lambda i, j: (i, j, 0)),
        compiler_params=_params(("parallel", "parallel")), name="conv_silu",
    )(x, x, x, conv_w.reshape(D_CONV, ch).astype(F32), conv_b.reshape(1, ch).astype(F32))


def _ssd_kernel(*refs, reverse, finalize, d_inner, d_state, n_groups):
    u_ref, dt_ref, dtt_ref, a_ref, at_ref, bias_ref, biast_ref, init_ref = refs[:8]
    pos = 8
    if finalize:
        z_ref, yb_ref, skip_ref, gn_ref = refs[pos:pos + 4]
        pos += 4
    y_ref, fin_ref, st_ref = refs[pos:pos + 3]
    c = pl.program_id(1)

    @pl.when(c == 0)
    def _():
        st_ref[...] = init_ref[0]

    q_len = SSD_CHUNK
    u = u_ref[0]
    xs = u[:, :d_inner]
    dt = _softplus(dt_ref[0] + bias_ref[...])
    dtt = _softplus(dtt_ref[0] + biast_ref[...])
    da = dt * a_ref[...]
    dat = dtt * at_ref[...]
    row = lax.broadcasted_iota(I32, (q_len, q_len), 0)
    col = lax.broadcasted_iota(I32, (q_len, q_len), 1)
    lower = row >= col
    upper = row <= col
    causal = upper if reverse else lower
    tri_col = causal.astype(F32)
    tri_row = (lower if reverse else upper).astype(F32)
    q_col = jnp.dot(tri_col, da, precision=HIGHEST, preferred_element_type=F32)
    q_row = jnp.dot(dat, tri_row, precision=HIGHEST, preferred_element_type=F32)
    total = jnp.sum(da, axis=0, keepdims=True)
    first = _half_masks((q_len, LANES))
    heads_per_group = (d_inner // HEAD_DIM) // n_groups
    pairs_per_group = heads_per_group // 2
    y_pairs = []
    for g in range(n_groups):
        b_g = u[:, d_inner + g * d_state:d_inner + (g + 1) * d_state]
        c_g = u[:, d_inner + (n_groups + g) * d_state:d_inner + (n_groups + g + 1) * d_state]
        c_bf = c_g.astype(BF16)
        cb = _dot_nt(c_bf, b_g.astype(BF16))
        b_t = b_g.T.astype(BF16)
        for pp in range(pairs_per_group):
            p = g * pairs_per_group + pp
            h0, h1 = 2 * p, 2 * p + 1
            x_pair = xs[:, p * LANES:(p + 1) * LANES]
            dt_pair = jnp.where(first, dt[:, h0:h0 + 1], dt[:, h1:h1 + 1])
            qc_pair = jnp.where(first, q_col[:, h0:h0 + 1], q_col[:, h1:h1 + 1])
            tot_pair = jnp.where(first[:1], total[:, h0:h0 + 1], total[:, h1:h1 + 1])
            xdt = x_pair * dt_pair
            xdt_bf = xdt.astype(BF16)
            diag = []
            for h in (h0, h1):
                diff = q_col[:, h:h + 1] - q_row[h:h + 1, :]
                lmat = jnp.exp(jnp.where(causal, diff, NEG_BIG))
                diag.append(_dot((cb * lmat).astype(BF16), xdt_bf))
            y_diag = jnp.where(first, diag[0], diag[1])
            state = st_ref[p]
            y_off = _dot(c_bf, state.astype(BF16)) * jnp.exp(qc_pair)
            y_pairs.append(y_diag + y_off)
            carry_in = (xdt * jnp.exp(tot_pair - qc_pair)).astype(BF16)
            st_ref[p] = jnp.exp(tot_pair) * state + _dot(b_t, carry_in)
    y = jnp.concatenate(y_pairs, axis=1)
    if finalize:
        y = y + yb_ref[0] + skip_ref[...] * xs
        y = y * _silu(z_ref[0])
        gw = d_inner // n_groups
        y = jnp.concatenate(
            [y[:, g * gw:(g + 1) * gw] * lax.rsqrt(
                jnp.mean(y[:, g * gw:(g + 1) * gw] ** 2, axis=1, keepdims=True) + NORM_EPS)
             for g in range(n_groups)], axis=1)
        y = y * gn_ref[...]
    y_ref[0] = y.astype(y_ref.dtype)

    @pl.when(c == pl.num_programs(1) - 1)
    def _():
        fin_ref[0] = st_ref[...]


def _ssd_scan(u, dt, dtt, a, dt_bias, init, *, reverse, d_inner, d_state, n_groups, final=None):
    b, length, ch = u.shape
    heads = dt.shape[2]
    nc = length // SSD_CHUNK
    pairs = heads // 2
    cidx = (lambda j: nc - 1 - j) if reverse else (lambda j: j)
    small = lambda shape: pl.BlockSpec(shape, lambda i, j: (0,) * len(shape))
    in_specs = [pl.BlockSpec((1, SSD_CHUNK, ch), lambda i, j: (i, cidx(j), 0)),
                pl.BlockSpec((1, SSD_CHUNK, heads), lambda i, j: (i, cidx(j), 0)),
                pl.BlockSpec((1, heads, SSD_CHUNK), lambda i, j: (i, 0, cidx(j))),
                small((1, heads)), small((heads, 1)), small((1, heads)), small((heads, 1)),
                pl.BlockSpec((1, pairs, d_state, LANES), lambda i, j: (i, 0, 0, 0))]
    args = [u, dt, dtt, a.reshape(1, heads), a.reshape(heads, 1),
            dt_bias.reshape(1, heads).astype(F32), dt_bias.reshape(heads, 1).astype(F32), init]
    out_dtype = F32
    if final is not None:
        z, yb, d_skip, gnorm_g = final
        in_specs += [pl.BlockSpec((1, SSD_CHUNK, d_inner), lambda i, j: (i, cidx(j), 0)),
                     pl.BlockSpec((1, SSD_CHUNK, d_inner), lambda i, j: (i, cidx(j), 0)),
                     small((1, d_inner)), small((1, d_inner))]
        args += [z, yb, jnp.repeat(d_skip.astype(F32), HEAD_DIM).reshape(1, d_inner),
                 gnorm_g.reshape(1, d_inner).astype(F32)]
        out_dtype = BF16
    kern = functools.partial(_ssd_kernel, reverse=reverse, finalize=final is not None,
                             d_inner=d_inner, d_state=d_state, n_groups=n_groups)
    return pl.pallas_call(
        kern,
        out_shape=[jax.ShapeDtypeStruct((b, length, d_inner), out_dtype),
                   jax.ShapeDtypeStruct((b, pairs, d_state, LANES), F32)],
        grid=(b, nc), in_specs=in_specs,
        out_specs=[pl.BlockSpec((1, SSD_CHUNK, d_inner), lambda i, j: (i, cidx(j), 0)),
                   pl.BlockSpec((1, pairs, d_state, LANES), lambda i, j: (i, 0, 0, 0))],
        scratch_shapes=[pltpu.VMEM((pairs, d_state, LANES), F32)],
        compiler_params=_params(("parallel", "arbitrary")), name="ssd_scan",
    )(*args)


def _outproj_kernel(oa_ref, ob_ref, w_ref, x_ref, gpost_ref, gate_ref, gpre_ref, sc_ref, sh_ref, wrh_ref, wrl_ref,
                    xo_ref, h_ref, aff_ref, *, row_chunk):
    half = oa_ref.shape[2]
    for r0 in range(0, x_ref.shape[1], row_chunk):
        rows = slice(r0, r0 + row_chunk)
        mix = _dot(oa_ref[0, rows, :], w_ref[:half, :]) + _dot(ob_ref[0, rows, :], w_ref[half:, :])
        x_new = x_ref[0, rows, :] + gate_ref[0] * _rms(mix, gpost_ref[...])
        xo_ref[0, rows, :] = x_new
        h = _rms(x_new, gpre_ref[...]) * sc_ref[0] + sh_ref[0]
        h_ref[0, rows, :] = h
        h_hi = h.astype(BF16)
        h_lo = (h - h_hi.astype(F32)).astype(BF16)
        logits = _dot_nt(wrh_ref[...], h_hi) + _dot_nt(wrh_ref[...], h_lo) + _dot_nt(wrl_ref[...], h_hi)
        e = jnp.exp(logits - jnp.max(logits, axis=0, keepdims=True))
        aff_ref[0, :, rows] = e / jnp.sum(e, axis=0, keepdims=True)


def _out_project(oa, ob, w_out, x, g_post, gate, g_pre, scale1p, shift, w_router):
    b, n, d = x.shape
    half = oa.shape[2]
    n_exp = w_router.shape[1]
    tm = min(512, n)
    bm = gate.shape[0]
    mod_map = (lambda i, j: (i, 0, 0)) if bm > 1 else (lambda i, j: (0, 0, 0))
    row = pl.BlockSpec((1, d), lambda i, j: (0, 0))
    mod = pl.BlockSpec((1, 1, d), mod_map)
    tile = pl.BlockSpec((1, tm, d), lambda i, j: (i, j, 0))
    act = pl.BlockSpec((1, tm, half), lambda i, j: (i, j, 0))
    wr_spec = pl.BlockSpec((n_exp, d), lambda i, j: (0, 0))
    wr = w_router.T.astype(F32)
    wr_hi = wr.astype(BF16)
    wr_lo = (wr - wr_hi.astype(F32)).astype(BF16)
    return pl.pallas_call(
        functools.partial(_outproj_kernel, row_chunk=min(256, tm)),
        out_shape=[jax.ShapeDtypeStruct((b, n, d), F32), jax.ShapeDtypeStruct((b, n, d), F32),
                   jax.ShapeDtypeStruct((b, n_exp, n), F32)],
        grid=(b, n // tm),
        in_specs=[act, act, pl.BlockSpec((2 * half, d), lambda i, j: (0, 0)), tile, row, mod, row, mod, mod,
                  wr_spec, wr_spec],
        out_specs=[tile, tile, pl.BlockSpec((1, n_exp, tm), lambda i, j: (i, 0, j))],
        compiler_params=_params(("parallel", "parallel")), name="out_proj_router",
    )(oa, ob, w_out, x, g_post.reshape(1, d), gate, g_pre.reshape(1, d), scale1p, shift, wr_hi, wr_lo)


def _topk_kernel(aff_ref, idx_ref, gate_ref, work_ref, key_ref, *, cap, slot_block):
    n_exp, n = aff_ref.shape[1], aff_ref.shape[2]
    n_blk = n // LANES
    aff = aff_ref[0]

    def search(i, thr):
        cand = thr | jnp.left_shift(jnp.int32(1), 30 - i)
        cnt = jnp.sum(jnp.where(aff_ref[0] >= pltpu.bitcast(cand, F32), 1.0, 0.0), axis=1, keepdims=True)
        return jnp.where(cnt >= cap, cand, thr)

    thr = lax.fori_loop(0, 31, search, jnp.zeros((n_exp, 1), I32))
    above = aff >= pltpu.bitcast(thr + 1, F32)
    tied = (aff >= pltpu.bitcast(thr, F32)) & ~above
    need = cap - jnp.sum(jnp.where(above, 1.0, 0.0), axis=1, keepdims=True)

    r = lax.broadcasted_iota(I32, (LANES, LANES), 0)
    c = lax.broadcasted_iota(I32, (LANES, LANES), 1)
    incl = (r <= c).astype(BF16)

    def exclusive_prefix(mask_f32):
        starts = [jnp.zeros((n_exp, 1), F32)]
        for j in range(n_blk):
            blk = mask_f32[:, j * LANES:(j + 1) * LANES]
            run = _dot(blk.astype(BF16), incl)
            work_ref[:, j * LANES:(j + 1) * LANES] = run - blk + starts[-1]
            starts.append(starts[-1] + jnp.sum(blk, axis=1, keepdims=True))
        return starts

    exclusive_prefix(jnp.where(tied, 1.0, 0.0))
    sel = above | (tied & (work_ref[...] < need))
    sel_f = jnp.where(sel, 1.0, 0.0)
    gate_ref[0] = jnp.where(sel, aff, 0.0)
    starts = exclusive_prefix(sel_f)
    key = jnp.where(sel, work_ref[...], -1.0)
    for j in range(n_blk):
        key_ref[j] = key[:, j * LANES:(j + 1) * LANES]

    lane = lax.broadcasted_iota(I32, (slot_block, LANES), 1).astype(F32)
    slot = lax.broadcasted_iota(I32, (slot_block, LANES), 0).astype(F32)

    for sb in range(cap // slot_block):
        first = sum(jnp.where(starts[j + 1] <= sb * slot_block, 1.0, 0.0) for j in range(n_blk))
        last = sum(jnp.where(starts[j] < (sb + 1) * slot_block, 1.0, 0.0) for j in range(n_blk))
        want = slot + float(sb * slot_block)
        for e in range(n_exp):
            def per_block(j, acc, e=e, want=want):
                key_row = key_ref[j, e:e + 1, :]
                tok = lane + lax.convert_element_type(j * LANES, F32)
                return acc + jnp.where(key_row == want, tok, 0.0)

            acc = lax.fori_loop(first[e, 0].astype(I32), last[e, 0].astype(I32), per_block,
                                jnp.zeros((slot_block, LANES), F32))
            idx_ref[0, e, sb * slot_block:(sb + 1) * slot_block, :] = (
                jnp.sum(acc, axis=1, keepdims=True).astype(I32))


def _route(aff_t):
    b, n_exp, n = aff_t.shape
    cap = CAPACITY_FACTOR * n // n_exp
    slot_block = min(LANES, cap)
    kern = functools.partial(_topk_kernel, cap=cap, slot_block=slot_block)
    idx, gate = pl.pallas_call(
        kern,
        out_shape=[jax.ShapeDtypeStruct((b, n_exp, cap, 1), I32), jax.ShapeDtypeStruct((b, n_exp, n), F32)],
        grid=(b,),
        in_specs=[pl.BlockSpec((1, n_exp, n), lambda i: (i, 0, 0))],
        out_specs=[pl.BlockSpec((1, n_exp, cap, 1), lambda i: (i, 0, 0, 0)),
                   pl.BlockSpec((1, n_exp, n), lambda i: (i, 0, 0))],
        scratch_shapes=[pltpu.VMEM((n_exp, n), F32), pltpu.VMEM((n // LANES, n_exp, LANES), F32)],
        compiler_params=_params(("parallel",)), name="expert_choice_route",
    )(aff_t)
    return idx.reshape(b * n_exp, 1, cap), gate.reshape(b * n_exp, 1, n)


def _moe_kernel(idx_ref, nxt_ref, h_ref, wg_ref, wu_ref, wd_ref, y_ref, xg_ref, *, cap, prefetch):
    sub = lax.broadcasted_iota(I32, (SUBLANES, h_ref.shape[2]), 0)

    def gather_group(src_ref, dst, base):
        tile = jnp.zeros(sub.shape, F32)
        for u in range(SUBLANES):
            t = src_ref[0, 0, base + u]
            r = t & (SUBLANES - 1)
            rows = h_ref[0, pl.ds(pl.multiple_of(t - r, SUBLANES), SUBLANES), :]
            rolled = pltpu.roll(rows, (u - r) & (SUBLANES - 1), axis=0)
            tile = jnp.where(sub == u, rolled, tile)
        dst[pl.ds(base, SUBLANES), :] = tile

    def gather_loop(dst):
        def body(i8, carry):
            gather_group(idx_ref, dst, pl.multiple_of(i8 * SUBLANES, SUBLANES))
            return carry
        lax.fori_loop(0, cap // SUBLANES, body, 0)

    if prefetch:
        e = pl.program_id(1)
        slot = e & 1

        @pl.when(e == 0)
        def _():
            gather_loop(xg_ref.at[0])
    else:
        slot = 0
        gather_loop(xg_ref.at[0])
    xg = xg_ref[slot].astype(BF16)
    gate = _dot(xg, wg_ref[0])
    up = _dot(xg, wu_ref[0])
    hid = (_silu(gate) * up).astype(BF16)
    y_ref[0, 0] = _dot(hid, wd_ref[0])
    if prefetch:
        for i8 in range(cap // SUBLANES):
            gather_group(nxt_ref, xg_ref.at[1 - slot], i8 * SUBLANES)


def _combine_kernel(idx_ref, gate_ref, y_ref, acc_ref, *, cap):
    @pl.when(pl.program_id(1) == 0)
    def _():
        acc_ref[...] = jnp.zeros_like(acc_ref)

    sub = lax.broadcasted_iota(I32, (SUBLANES, acc_ref.shape[2]), 0)

    def scatter(i8, carry):
        base = pl.multiple_of(i8 * SUBLANES, SUBLANES)
        y_tile = y_ref[0, 0, pl.ds(base, SUBLANES), :]
        for u0 in range(0, SUBLANES, 4):
            pending = []
            for u in range(u0, u0 + 4):
                t = idx_ref[0, 0, base + u]
                g = gate_ref[0, 0, t]
                r = t & (SUBLANES - 1)
                dst = acc_ref.at[0, pl.ds(pl.multiple_of(t - r, SUBLANES), SUBLANES), :]
                rolled = pltpu.roll(y_tile, (r - u) & (SUBLANES - 1), axis=0)
                pending.append((dst, dst[...] + g * rolled, sub == r))
            for dst, new, mask in pending:
                pltpu.store(dst, new, mask=mask)
        return carry

    lax.fori_loop(0, cap // SUBLANES, scatter, 0)


def _expert_ffn(h, idx, gate, wg, wu, wd):
    b, n, d = h.shape
    n_exp, _, ff = wg.shape
    cap = idx.shape[2]
    smem = lambda width: pl.BlockSpec((1, 1, width), lambda i, e: (i * n_exp + e, 0, 0), memory_space=pltpu.SMEM)
    sample_major = n * d * h.dtype.itemsize >= 3 * d * ff * wg.dtype.itemsize
    if sample_major:
        grid = (b, n_exp)
        be = lambda i, e: (i, e)
        nxt = pl.BlockSpec((1, 1, cap), lambda i, e: (i * n_exp + jnp.minimum(e + 1, n_exp - 1), 0, 0),
                           memory_space=pltpu.SMEM)
        h_spec = pl.BlockSpec((1, n, d), lambda i, e: (i, 0, 0), pipeline_mode=pl.Buffered(1))
    else:
        grid = (n_exp, b)
        be = lambda e, i: (i, e)
        nxt = pl.BlockSpec((1, 1, cap), lambda e, i: (i * n_exp + e, 0, 0), memory_space=pltpu.SMEM)
        h_spec = pl.BlockSpec((1, n, d), lambda e, i: (i, 0, 0))
    y = pl.pallas_call(
        functools.partial(_moe_kernel, cap=cap, prefetch=sample_major),
        out_shape=jax.ShapeDtypeStruct((b, n_exp, cap, d), F32),
        grid=grid,
        in_specs=[pl.BlockSpec((1, 1, cap), lambda *g: (be(*g)[0] * n_exp + be(*g)[1], 0, 0),
                               memory_space=pltpu.SMEM),
                  nxt, h_spec,
                  pl.BlockSpec((1, d, ff), lambda *g: (be(*g)[1], 0, 0)),
                  pl.BlockSpec((1, d, ff), lambda *g: (be(*g)[1], 0, 0)),
                  pl.BlockSpec((1, ff, d), lambda *g: (be(*g)[1], 0, 0))],
        out_specs=pl.BlockSpec((1, 1, cap, d), lambda *g: (*be(*g), 0, 0)),
        scratch_shapes=[pltpu.VMEM((2, cap, d), F32)],
        compiler_params=_params(("parallel", "arbitrary")), name="expert_ffn",
    )(idx, idx, h, wg, wu, wd)
    return pl.pallas_call(
        functools.partial(_combine_kernel, cap=cap),
        out_shape=jax.ShapeDtypeStruct((b, n, d), F32),
        grid=(b, n_exp),
        in_specs=[smem(cap), smem(n), pl.BlockSpec((1, 1, cap, d), lambda i, e: (i, e, 0, 0))],
        out_specs=pl.BlockSpec((1, n, d), lambda i, e: (i, 0, 0), pipeline_mode=pl.Buffered(1)),
        compiler_params=_params(("parallel", "arbitrary")), name="expert_combine",
    )(idx, gate, y)


def _residual_kernel(x_ref, f_ref, g_ref, gate_ref, o_ref):
    o_ref[0] = x_ref[0] + gate_ref[0] * _rms(f_ref[0], g_ref[...])


def _gated_residual(x, f, g, gate):
    b, n, d = x.shape
    tm = min(512, n)
    bm = gate.shape[0]
    mod_map = (lambda i, j: (i, 0, 0)) if bm > 1 else (lambda i, j: (0, 0, 0))
    tile = pl.BlockSpec((1, tm, d), lambda i, j: (i, j, 0))
    return pl.pallas_call(
        _residual_kernel, out_shape=jax.ShapeDtypeStruct((b, n, d), F32),
        grid=(b, n // tm),
        in_specs=[tile, tile, pl.BlockSpec((1, d), lambda i, j: (0, 0)), pl.BlockSpec((1, 1, d), mod_map)],
        out_specs=tile,
        compiler_params=_params(("parallel", "parallel")), name="gated_residual",
    )(x, f, g.reshape(1, d), gate)


def _rope_tables(n):
    t = jnp.arange(n, dtype=I32)
    row = (t // GRID_W).astype(F32)
    col = (t % GRID_W).astype(F32)
    n_freq = HEAD_DIM // 4
    inv_freq = ROPE_THETA ** (-jnp.arange(n_freq, dtype=F32) / n_freq)
    ang = jnp.concatenate([row[:, None] * inv_freq, col[:, None] * inv_freq], axis=-1)
    cos, sin = jnp.cos(ang), jnp.sin(ang)
    cos_t = jnp.tile(jnp.concatenate([cos, cos], axis=-1), (1, LANES // HEAD_DIM))
    sin_t = jnp.tile(jnp.concatenate([-sin, sin], axis=-1), (1, LANES // HEAD_DIM))
    return cos_t, sin_t


def _deinterleave_heads(w):
    d, width = w.shape
    w = w.reshape(d, width // HEAD_DIM, HEAD_DIM // 2, 2)
    return jnp.concatenate([w[..., 0], w[..., 1]], axis=-1).reshape(d, width)


def _dup_heads(w):
    d, width = w.shape
    w = w.reshape(d, width // HEAD_DIM, 1, HEAD_DIM)
    return jnp.concatenate([w, w], axis=2).reshape(d, 2 * width)


def _mod_split(mod_rows):
    sh1, sc1, gt1, sh2, sc2, gt2 = jnp.split(mod_rows[:, None, :], 6, axis=-1)
    return sh1, 1.0 + sc1, gt1, sh2, 1.0 + sc2, gt2


def _moe_block(x_mid, h2, aff_t, g_post_ffn, gt2, wg, wu, wd):
    idx, gate = _route(aff_t)
    ffn = _expert_ffn(h2, idx, gate, wg, wu, wd)
    return _gated_residual(x_mid, ffn, g_post_ffn, gt2)


def kernel(x, c, ctx, c_ctx, l0_w_mod, l0_b_mod, l0_g_pre_mix, l0_g_post_mix, l0_g_pre_ffn, l0_g_post_ffn, l0_w_in, l0_w_out, l0_lam_q1, l0_lam_k1, l0_lam_q2, l0_lam_k2, l0_subln_g, l0_rpb, l0_w_router, l0_w_gate, l0_w_up, l0_w_down, l1_w_mod, l1_b_mod, l1_g_pre_mix, l1_g_post_mix, l1_g_pre_ffn, l1_g_post_ffn, l1_w_in, l1_w_out, l1_sinks, l1_conv_w, l1_conv_b, l1_a_log_f, l1_a_log_b, l1_dt_bias_f, l1_dt_bias_b, l1_d_skip, l1_gnorm_g, l1_w_router, l1_w_gate, l1_w_up, l1_w_down):
    b, n, d = x.shape
    m = ctx.shape[1]
    half = d // 2
    scale = HEAD_DIM ** -0.5 * math.log2(math.e)
    cos_t, sin_t = _rope_tables(n)
    cond = jnp.zeros((16, d), F32).at[:b].set(c).at[b].set(c_ctx)

    mod = _modulation(cond, l0_w_mod, l0_b_mod)
    sh1, sc1, gt1, sh2, sc2, gt2 = _mod_split(mod[:b])
    csh1, csc1, cgt1, csh2, csc2, cgt2 = _mod_split(mod[b:b + 1])

    a_qk = half
    wq, wk, rest = l0_w_in[:, :a_qk], l0_w_in[:, a_qk:2 * a_qk], l0_w_in[:, 2 * a_qk:]
    qb_cols = slice(half, 2 * half)
    rest = rest.at[:, qb_cols].multiply(scale)
    w_in0 = jnp.concatenate([_deinterleave_heads(wq) * scale, _deinterleave_heads(wk), rest], axis=1).astype(BF16)
    segs0 = [(half, BF16)] * 6
    qa, ka, va, qb, kb, vb = _project(x, l0_g_pre_mix, sc1, sh1, w_in0, segs0, rope=(2 * a_qk, cos_t, sin_t))
    qa_c, ka_c, va_c, qb_c, kb_c, vb_c = _project(ctx, l0_g_pre_mix, csc1, csh1, w_in0, segs0)

    lam_init = 0.8 - 0.6 * math.exp(-0.3 * 0)
    lam = (jnp.exp(jnp.sum(l0_lam_q1.astype(F32) * l0_lam_k1.astype(F32)))
           - jnp.exp(jnp.sum(l0_lam_q2.astype(F32) * l0_lam_k2.astype(F32))) + lam_init)
    o_a = _diff_attention(qa, [(ka, va), (ka_c, va_c)], lam, l0_subln_g, 1.0 - lam_init)
    o_b = _neighbourhood_attention(qb, kb, vb, kb_c, vb_c, l0_rpb)
    oc_a = _diff_attention(qa_c, [(ka_c, va_c)], lam, l0_subln_g, 1.0 - lam_init)
    oc_b = _context_attention(qb_c, kb_c, vb_c)

    w_out0 = l0_w_out.astype(BF16)
    wg0, wu0, wd0 = l0_w_gate.astype(BF16), l0_w_up.astype(BF16), l0_w_down.astype(BF16)
    x_mid, h2, aff_t = _out_project(o_a, o_b, w_out0, x, l0_g_post_mix, gt1, l0_g_pre_ffn, sc2, sh2, l0_w_router)
    x1 = _moe_block(x_mid, h2, aff_t, l0_g_post_ffn, gt2, wg0, wu0, wd0)
    c_mid, hc2, aff_c = _out_project(oc_a, oc_b, w_out0, ctx, l0_g_post_mix, cgt1, l0_g_pre_ffn, csc2, csh2,
                                     l0_w_router)
    ctx1 = _moe_block(c_mid, hc2, aff_c, l0_g_post_ffn, cgt2, wg0, wu0, wd0)

    mod = _modulation(cond, l1_w_mod, l1_b_mod)
    sh1, sc1, gt1, sh2, sc2, gt2 = _mod_split(mod[:b])
    csh1, csc1, _, _, _, _ = _mod_split(mod[b:b + 1])

    c_q = half
    d_inner = half
    d_heads = l1_a_log_f.shape[0]
    d_xbc = l1_conv_w.shape[2]
    c_kv = (l1_w_in.shape[1] - c_q - d_xbc - d_inner - 2 * d_heads) // 2
    n_groups = 2
    d_state = (d_xbc - d_inner) // (2 * n_groups)
    wq = l1_w_in[:, :c_q]
    wk = l1_w_in[:, c_q:c_q + c_kv]
    wv = l1_w_in[:, c_q + c_kv:c_q + 2 * c_kv]
    o_x = c_q + 2 * c_kv
    w_x = l1_w_in[:, o_x:o_x + d_xbc + d_inner]
    w_dt = l1_w_in[:, o_x + d_xbc + d_inner:]
    w_kv = jnp.concatenate([_dup_heads(_deinterleave_heads(wk)), _dup_heads(wv), w_x, w_dt], axis=1)
    w_lat = jnp.concatenate([_deinterleave_heads(wq) * scale, w_kv], axis=1).astype(BF16)
    w_ctx = w_kv.astype(BF16)
    w_dt_t = w_dt.T.astype(BF16)
    segs_lat = [(c_q, BF16), (2 * c_kv, BF16), (2 * c_kv, BF16), (d_xbc, F32), (d_inner, F32), (2 * d_heads, F32)]
    q, k2, v2, xbc, z, dt, dtt = _project(x1, l1_g_pre_mix, sc1, sh1, w_lat, segs_lat,
                                          rope=(c_q + 2 * c_kv, cos_t, sin_t), wt=w_dt_t)
    k2_c, v2_c, xbc_c, _, dt_c, dtt_c = _project(ctx1, l1_g_pre_mix, csc1, csh1, w_ctx, segs_lat[1:], wt=w_dt_t)

    o_c = _window_attention(q, k2, v2, k2_c, v2_c, l1_sinks.astype(F32) * math.log2(math.e))

    a_f = -jnp.exp(l1_a_log_f.astype(F32))
    a_b = -jnp.exp(l1_a_log_b.astype(F32))
    u = _conv_silu(xbc, l1_conv_w, l1_conv_b)
    u_c = _conv_silu(xbc_c, l1_conv_w, l1_conv_b)
    ssd = functools.partial(_ssd_scan, d_inner=d_inner, d_state=d_state, n_groups=n_groups)
    zero_state = jnp.zeros((b, d_heads // 2, d_state, LANES), F32)
    _, sc_f = ssd(u_c, dt_c[..., :d_heads], dtt_c[:, :d_heads], a_f, l1_dt_bias_f, zero_state, reverse=False)
    _, sc_b = ssd(u_c, dt_c[..., d_heads:], dtt_c[:, d_heads:], a_b, l1_dt_bias_b, zero_state, reverse=True)
    y_b, _ = ssd(u, dt[..., d_heads:], dtt[:, d_heads:], a_b, l1_dt_bias_b, sc_b, reverse=True)
    o_d, _ = ssd(u, dt[..., :d_heads], dtt[:, :d_heads], a_f, l1_dt_bias_f, sc_f, reverse=False,
                 final=(z, y_b, l1_d_skip, l1_gnorm_g))

    x_mid, h2, aff_t = _out_project(o_c, o_d, l1_w_out.astype(BF16), x1, l1_g_post_mix, gt1, l1_g_pre_ffn, sc2, sh2,
                                    l1_w_router)
    return _moe_block(x_mid, h2, aff_t, l1_g_post_ffn, gt2, l1_w_gate.astype(BF16), l1_w_up.astype(BF16),
                      l1_w_down.astype(BF16))
```

```python
import functools
import math

import jax
import jax.numpy as jnp
from jax import lax
from jax.experimental import pallas as pl
from jax.experimental.pallas import tpu as pltpu

F32 = jnp.float32
BF16 = jnp.bfloat16
I32 = jnp.int32

HEAD_DIM = 64
LANES = 128
SUBLANES = 8
GRID_W = 64
ROPE_THETA = 10000.0
NORM_EPS = 1e-6
NA_ROWS = 8
NA_COLS = 16
WINDOW = 128
Q_BLOCK = 128
SSD_CHUNK = 128
D_CONV = 5
N_EXPERTS = 16
CAPACITY_FACTOR = 2
NEG_BIG = -1e30
VMEM_LIMIT = 56 * 1024 * 1024
HIGHEST = lax.Precision.HIGHEST
NT_DIMS = (((1,), (1,)), ((), ()))


def _params(sem, vmem=VMEM_LIMIT):
    return pltpu.CompilerParams(dimension_semantics=sem, vmem_limit_bytes=vmem)


def _rms(u, g):
    return u * lax.rsqrt(jnp.mean(u * u, axis=-1, keepdims=True) + NORM_EPS) * g


def _silu(u):
    return u * (1.0 / (1.0 + jnp.exp(-u)))


def _softplus(u):
    return jnp.maximum(u, 0.0) + jnp.log(1.0 + jnp.exp(-jnp.abs(u)))


def _dot(a, b):
    return jnp.dot(a, b, preferred_element_type=F32)


def _dot_nt(a, b):
    return lax.dot_general(a, b, NT_DIMS, preferred_element_type=F32)


def _half_masks(shape):
    lane = lax.broadcasted_iota(I32, shape, 1)
    return (lane % LANES) < HEAD_DIM


def _mod_kernel(c_ref, w_ref, b_ref, o_ref):
    s = _silu(c_ref[...]).astype(BF16)
    o_ref[...] = _dot(s, w_ref[...].astype(BF16)) + b_ref[...]


def _modulation(cond, w_mod, b_mod):
    rows, d = cond.shape
    n_out = w_mod.shape[1]
    tn = 1024
    return pl.pallas_call(
        _mod_kernel,
        out_shape=jax.ShapeDtypeStruct((rows, n_out), F32),
        grid=(n_out // tn,),
        in_specs=[pl.BlockSpec((rows, d), lambda j: (0, 0)),
                  pl.BlockSpec((d, tn), lambda j: (0, j)),
                  pl.BlockSpec((1, tn), lambda j: (0, j))],
        out_specs=pl.BlockSpec((rows, tn), lambda j: (0, j)),
        compiler_params=_params(("parallel",)),
        name="modulation",
    )(cond, w_mod, b_mod.reshape(1, n_out))


def _rope_block(u, cos, sin_signed, first_half):
    width = u.shape[1]
    fwd = pltpu.roll(u, HEAD_DIM // 2, axis=1)
    bwd = pltpu.roll(u, width - HEAD_DIM // 2, axis=1)
    return u * cos + jnp.where(first_half, bwd, fwd) * sin_signed


def _proj_kernel(*refs, seg_widths, rope_cols, has_t, col_chunk):
    x_ref, g_ref, sc_ref, sh_ref, w_ref = refs[:5]
    pos = 5
    if rope_cols:
        cos_ref, sin_ref = refs[pos:pos + 2]
        pos += 2
    if has_t:
        wt_ref = refs[pos]
        pos += 1
    out_refs = refs[pos:]
    h = (_rms(x_ref[0], g_ref[...]) * sc_ref[0] + sh_ref[0]).astype(BF16)
    if rope_cols:
        cos = cos_ref[...]
        sin_signed = sin_ref[...]
        first_half = (lax.broadcasted_iota(I32, cos.shape, 1) % HEAD_DIM) < HEAD_DIM // 2
    col = 0
    for o_ref, width in zip(out_refs, seg_widths):
        for c0 in range(0, width, col_chunk):
            cw = min(col_chunk, width - c0)
            acc = _dot(h, w_ref[:, col + c0:col + c0 + cw])
            if col + c0 < rope_cols:
                acc = jnp.concatenate(
                    [_rope_block(acc[:, k:k + LANES], cos, sin_signed, first_half)
                     for k in range(0, cw, LANES)], axis=1)
            o_ref[0, :, c0:c0 + cw] = acc.astype(o_ref.dtype)
        col += width
    if has_t:
        out_refs[-1][0] = _dot_nt(wt_ref[...], h)


def _project(x, g, scale1p, shift, w, segs, rope=None, wt=None):
    b, n, d = x.shape
    tm = min(512, n)
    bm = scale1p.shape[0]
    mod_map = (lambda i, j: (i, 0, 0)) if bm > 1 else (lambda i, j: (0, 0, 0))
    in_specs = [pl.BlockSpec((1, tm, d), lambda i, j: (i, j, 0)),
                pl.BlockSpec((1, d), lambda i, j: (0, 0)),
                pl.BlockSpec((1, 1, d), mod_map),
                pl.BlockSpec((1, 1, d), mod_map),
                pl.BlockSpec(w.shape, lambda i, j: (0, 0))]
    args = [x, g.reshape(1, d), scale1p, shift, w]
    rope_cols = 0
    if rope is not None:
        rope_cols, cos, sin_signed = rope
        in_specs += [pl.BlockSpec((tm, LANES), lambda i, j: (j, 0))] * 2
        args += [cos, sin_signed]
    if wt is not None:
        in_specs.append(pl.BlockSpec(wt.shape, lambda i, j: (0, 0)))
        args.append(wt)
    out_shape = [jax.ShapeDtypeStruct((b, n, wd), dt) for wd, dt in segs]
    out_specs = [pl.BlockSpec((1, tm, wd), lambda i, j: (i, j, 0)) for wd, _ in segs]
    if wt is not None:
        out_shape.append(jax.ShapeDtypeStruct((b, wt.shape[0], n), F32))
        out_specs.append(pl.BlockSpec((1, wt.shape[0], tm), lambda i, j: (i, 0, j)))
    kern = functools.partial(_proj_kernel, seg_widths=tuple(wd for wd, _ in segs),
                             rope_cols=rope_cols, has_t=wt is not None, col_chunk=512)
    return pl.pallas_call(
        kern, out_shape=out_shape, grid=(b, n // tm), in_specs=in_specs, out_specs=out_specs,
        compiler_params=_params(("parallel", "parallel")), name="in_proj",
    )(*args)


def _diff_kernel(*refs, seg_lens, tk, out_scale, heads):
    n_seg = len(seg_lens)
    m_all, acc_all, s_all = refs[4 + 2 * n_seg:]
    for h in range(heads):
        _diff_head(refs[:4 + 2 * n_seg], m_all.at[h], acc_all.at[h], s_all.at[2 * (h % 2)],
                   s_all.at[2 * (h % 2) + 1], slice(h * LANES, (h + 1) * LANES), seg_lens, tk, out_scale)


def _diff_head(io_refs, m_ref, acc_ref, s0_ref, s1_ref, cols, seg_lens, tk, out_scale):
    lam_ref, q_ref, g_ref = io_refs[:3]
    n_seg = len(seg_lens)
    kv_refs = [r.at[:, :, cols] for r in io_refs[3:3 + 2 * n_seg]]
    o_ref = io_refs[3 + 2 * n_seg]
    qf = q_ref[0, :, cols].astype(F32)
    first = _half_masks(qf.shape)
    qs = (jnp.where(first, qf, 0.0).astype(BF16), jnp.where(first, 0.0, qf).astype(BF16))
    m_ref[...] = jnp.full(m_ref.shape, NEG_BIG, F32)
    acc_ref[...] = jnp.zeros(acc_ref.shape, F32)

    def scores(kc):
        return [_dot_nt(qs[t], kc) for t in range(2)]

    def absorb(sc, vc):
        v_ones = jnp.concatenate([vc, jnp.ones(vc.shape, BF16)], axis=1)
        for t in range(2):
            m_old = m_ref[t]
            m_new = jnp.maximum(m_old, jnp.max(sc[t], axis=1, keepdims=True))
            p = jnp.exp2(sc[t] - m_new).astype(BF16)
            acc_ref[t] = jnp.exp2(m_old - m_new) * acc_ref[t] + _dot(p, v_ones)
            m_ref[t] = m_new

    def put(buf, sc):
        buf[0] = sc[0]
        buf[1] = sc[1]

    main = max(range(n_seg), key=lambda i: seg_lens[i])
    n_chunks = seg_lens[main] // tk if seg_lens[main] >= 2 * tk else 1
    if n_chunks > 1:
        assert n_chunks % 2 == 0 and seg_lens[main] == n_chunks * tk
        k_ref, v_ref = kv_refs[2 * main], kv_refs[2 * main + 1]
        rows = lambda ref, c: ref[0, pl.ds(pl.multiple_of(c * tk, tk), tk), :]
    singles = [s_i for s_i in range(n_seg) if s_i != main or n_chunks == 1]
    single_scores = [scores(kv_refs[2 * s_i][0]) for s_i in singles]
    if n_chunks > 1:
        put(s0_ref, scores(rows(k_ref, 0)))
    for s_i, sc in zip(singles, single_scores):
        absorb(sc, kv_refs[2 * s_i + 1][0])
    if n_chunks > 1:
        def pair(c, last):
            put(s1_ref, scores(rows(k_ref, c + 1)))
            absorb([s0_ref[0], s0_ref[1]], rows(v_ref, c))
            if not last:
                put(s0_ref, scores(rows(k_ref, c + 2)))
            absorb([s1_ref[0], s1_ref[1]], rows(v_ref, c + 1))

        def body(j, carry):
            pair(2 * j, False)
            return carry

        lax.fori_loop(0, n_chunks // 2 - 1, body, 0)
        pair(n_chunks - 2, True)
    ratio0 = acc_ref[0, :, :LANES] / acc_ref[0, :, LANES:]
    ratio1 = acc_ref[1, :, :LANES] / acc_ref[1, :, LANES:]
    o = ratio0 - lam_ref[0] * ratio1
    o_ref[0, :, cols] = (_rms(o, g_ref[...]) * out_scale).astype(o_ref.dtype)


def _diff_attention(q, kv_segs, lam, subln_g, out_scale):
    b, nq, width = q.shape
    heads = width // LANES
    tq = min(512, nq)
    in_specs = [pl.BlockSpec(memory_space=pltpu.SMEM),
                pl.BlockSpec((1, tq, width), lambda i, j: (i, j, 0)),
                pl.BlockSpec((1, LANES), lambda i, j: (0, 0))]
    args = [lam.reshape(1).astype(F32), q, subln_g.reshape(1, LANES).astype(F32)]
    seg_lens = []
    for k, v in kv_segs:
        length = k.shape[1]
        seg_lens.append(length)
        in_specs += [pl.BlockSpec((1, length, width), lambda i, j: (i, 0, 0))] * 2
        args += [k, v]
    tk = 512
    kern = functools.partial(_diff_kernel, seg_lens=tuple(seg_lens), tk=tk, out_scale=out_scale, heads=heads)
    return pl.pallas_call(
        kern, out_shape=jax.ShapeDtypeStruct((b, nq, width), BF16),
        grid=(b, nq // tq), in_specs=in_specs,
        out_specs=pl.BlockSpec((1, tq, width), lambda i, j: (i, j, 0)),
        scratch_shapes=[pltpu.VMEM((heads, 2, tq, 1), F32), pltpu.VMEM((heads, 2, tq, 2 * LANES), F32),
                        pltpu.VMEM((4, 2, tq, tk), F32)],
        compiler_params=_params(("parallel", "parallel")), name="diff_attention",
    )(*args)


NA_BLOCK_ROWS = 4
NA_WIN_ROWS = NA_BLOCK_ROWS + NA_ROWS


def _na_window_start(blk, rows):
    return jnp.clip(blk * NA_BLOCK_ROWS - NA_ROWS // 2, 0, rows - NA_WIN_ROWS)


def _natten_kernel(q_ref, k_ref, v_ref, kc_ref, vc_ref, bias_ref, o_ref, *, rows):
    start = pl.multiple_of(_na_window_start(pl.program_id(1), rows) * GRID_W, GRID_W)
    n_keys = NA_WIN_ROWS * GRID_W
    nq = NA_BLOCK_ROWS * GRID_W
    q = q_ref[0].astype(F32)
    kwin = k_ref[0, pl.ds(start, n_keys), :]
    vwin = v_ref[0, pl.ds(start, n_keys), :]
    kc = kc_ref[0]
    vc = vc_ref[0]
    first = _half_masks((nq, LANES))
    for p in range(q.shape[1] // LANES):
        cols = slice(p * LANES, (p + 1) * LANES)
        qs = jnp.concatenate([jnp.where(first, q[:, cols], 0.0), jnp.where(first, 0.0, q[:, cols])],
                             axis=0).astype(BF16)
        s_nb = _dot_nt(qs, kwin[:, cols]) + jnp.concatenate([bias_ref[0, 2 * p], bias_ref[0, 2 * p + 1]], axis=0)
        s_cx = _dot_nt(qs, kc[:, cols])
        m = jnp.maximum(jnp.max(s_nb, axis=1, keepdims=True), jnp.max(s_cx, axis=1, keepdims=True))
        p_nb = jnp.exp2(s_nb - m).astype(BF16)
        p_cx = jnp.exp2(s_cx - m).astype(BF16)
        v_nb = jnp.concatenate([vwin[:, cols], jnp.ones((n_keys, LANES), BF16)], axis=1)
        v_cx = jnp.concatenate([vc[:, cols], jnp.ones((vc.shape[0], LANES), BF16)], axis=1)
        acc = _dot(p_nb, v_nb) + _dot(p_cx, v_cx)
        ratio = acc[:, :LANES] / acc[:, LANES:]
        o_ref[0, :, cols] = jnp.where(first, ratio[:nq], ratio[nq:]).astype(o_ref.dtype)


def _na_bias_table(rpb, rows):
    heads = rpb.shape[0]
    w = jnp.arange(GRID_W)
    kc = jnp.arange(GRID_W)
    col_start = jnp.clip(w - NA_COLS // 2, 0, GRID_W - NA_COLS)
    valid = (kc[None, :] >= col_start[:, None]) & (kc[None, :] < col_start[:, None] + NA_COLS)
    pad = GRID_W - NA_COLS
    padded = jnp.pad(rpb.astype(F32) * math.log2(math.e), ((0, 0), (0, 0), (pad, pad)))
    cols = jnp.stack([padded[:, :, GRID_W - 1 - wi:2 * GRID_W - 1 - wi] for wi in range(GRID_W)], axis=2)
    cols = jnp.where(valid[None, None], cols, NEG_BIG)
    masked = jnp.full((heads, GRID_W, GRID_W), NEG_BIG, F32)
    cases = []
    for first_row in (0, NA_BLOCK_ROWS, rows - NA_BLOCK_ROWS):
        win0 = min(max(first_row - NA_ROWS // 2, 0), rows - NA_WIN_ROWS)
        per_q = []
        for qr in range(NA_BLOCK_ROWS):
            r = first_row + qr
            rs = min(max(r - NA_ROWS // 2, 0), rows - NA_ROWS)
            per_k = [cols[:, win0 + kr - r + NA_ROWS - 1] if rs <= win0 + kr < rs + NA_ROWS else masked
                     for kr in range(NA_WIN_ROWS)]
            per_q.append(jnp.stack(per_k, axis=2))
        cases.append(jnp.stack(per_q, axis=1))
    return jnp.stack(cases).reshape(3, heads, NA_BLOCK_ROWS * GRID_W, NA_WIN_ROWS * GRID_W)


def _neighbourhood_attention(q, k, v, kc, vc, rpb):
    b, n, width = q.shape
    m = kc.shape[1]
    rows = n // GRID_W
    n_blk = rows // NA_BLOCK_ROWS
    assert rows % NA_BLOCK_ROWS == 0 and rows >= NA_WIN_ROWS + NA_BLOCK_ROWS
    bias = _na_bias_table(rpb, rows)
    heads = rpb.shape[0]
    nq = NA_BLOCK_ROWS * GRID_W
    kern = functools.partial(_natten_kernel, rows=rows)
    return pl.pallas_call(
        kern, out_shape=jax.ShapeDtypeStruct((b, n, width), BF16),
        grid=(b, n_blk),
        in_specs=[pl.BlockSpec((1, nq, width), lambda i, r: (i, r, 0)),
                  pl.BlockSpec((1, n, width), lambda i, r: (i, 0, 0)),
                  pl.BlockSpec((1, n, width), lambda i, r: (i, 0, 0)),
                  pl.BlockSpec((1, m, width), lambda i, r: (i, 0, 0)),
                  pl.BlockSpec((1, m, width), lambda i, r: (i, 0, 0)),
                  pl.BlockSpec((1, heads, nq, NA_WIN_ROWS * GRID_W),
                               lambda i, r: (jnp.where(r == 0, 0, jnp.where(r == n_blk - 1, 2, 1)), 0, 0, 0))],
        out_specs=pl.BlockSpec((1, nq, width), lambda i, r: (i, r, 0)),
        compiler_params=_params(("parallel", "parallel")), name="neighbourhood_attention",
    )(q, k, v, kc, vc, bias)


def _ctx_attn_kernel(q_ref, k_ref, v_ref, o_ref):
    q = q_ref[0].astype(F32)
    k = k_ref[0]
    v = v_ref[0]
    first = _half_masks((q.shape[0], LANES))
    for p in range(q.shape[1] // LANES):
        cols = slice(p * LANES, (p + 1) * LANES)
        outs = []
        for sub in range(2):
            qm = jnp.where(first if sub == 0 else ~first, q[:, cols], 0.0).astype(BF16)
            s = _dot_nt(qm, k[:, cols])
            e = jnp.exp2(s - jnp.max(s, axis=1, keepdims=True))
            outs.append(_dot(e.astype(BF16), v[:, cols]) / jnp.sum(e, axis=1, keepdims=True))
        o_ref[0, :, cols] = jnp.where(first, outs[0], outs[1]).astype(o_ref.dtype)


def _context_attention(q, k, v):
    b, m, width = q.shape
    spec = pl.BlockSpec((1, m, width), lambda i: (i, 0, 0))
    return pl.pallas_call(
        _ctx_attn_kernel, out_shape=jax.ShapeDtypeStruct((b, m, width), BF16),
        grid=(b,), in_specs=[spec, spec, spec], out_specs=spec,
        compiler_params=_params(("parallel",)), name="context_attention",
    )(q, k, v)


def _swa_kernel(sink_ref, q_ref, k_ref, v_ref, kc_ref, vc_ref, o_ref, *, n_blocks):
    blk = pl.program_id(1)
    span = 3 * Q_BLOCK
    start = pl.multiple_of(jnp.clip(blk - 1, 0, n_blocks - 3) * Q_BLOCK, Q_BLOCK)
    q = q_ref[0].astype(F32)
    kwin = k_ref[0, pl.ds(start, span), :]
    vwin = v_ref[0, pl.ds(start, span), :]
    kc = kc_ref[0]
    vc = vc_ref[0]
    first = _half_masks((Q_BLOCK, LANES))
    pairs = q.shape[1] // LANES
    n_groups = k_ref.shape[2] // LANES
    pairs_per_group = pairs // n_groups
    stack = 2 * pairs_per_group
    row = lax.broadcasted_iota(I32, (stack * Q_BLOCK, span), 0)
    qpos = blk * Q_BLOCK + (row & (Q_BLOCK - 1))
    kpos = start + lax.broadcasted_iota(I32, (stack * Q_BLOCK, span), 1)
    valid = jnp.abs(qpos - kpos) <= WINDOW
    head_of_row = lax.broadcasted_iota(I32, (stack * Q_BLOCK, 1), 0) >> (Q_BLOCK.bit_length() - 1)
    for g in range(n_groups):
        gcols = slice(g * LANES, (g + 1) * LANES)
        parts = []
        for p in range(g * pairs_per_group, (g + 1) * pairs_per_group):
            q_pair = q[:, p * LANES:(p + 1) * LANES]
            parts += [jnp.where(first, q_pair, 0.0), jnp.where(first, 0.0, q_pair)]
        qs = jnp.concatenate(parts, axis=0).astype(BF16)
        sink = jnp.zeros((stack * Q_BLOCK, 1), F32)
        for i in range(stack):
            sink = jnp.where(head_of_row == i, sink_ref[g * stack + i], sink)
        s_loc = jnp.where(valid, _dot_nt(qs, kwin[:, gcols]), NEG_BIG)
        s_cx = _dot_nt(qs, kc[:, gcols])
        m = jnp.maximum(jnp.maximum(jnp.max(s_loc, axis=1, keepdims=True),
                                    jnp.max(s_cx, axis=1, keepdims=True)), sink)
        p_loc = jnp.exp2(s_loc - m).astype(BF16)
        p_cx = jnp.exp2(s_cx - m).astype(BF16)
        v_loc = jnp.concatenate([vwin[:, gcols], jnp.ones((span, LANES), BF16)], axis=1)
        v_cx = jnp.concatenate([vc[:, gcols], jnp.ones((vc.shape[0], LANES), BF16)], axis=1)
        acc = _dot(p_loc, v_loc) + _dot(p_cx, v_cx)
        ratio = acc[:, :LANES] / (acc[:, LANES:] + jnp.exp2(sink - m))
        for i in range(pairs_per_group):
            p = g * pairs_per_group + i
            lo = ratio[2 * i * Q_BLOCK:(2 * i + 1) * Q_BLOCK]
            hi = ratio[(2 * i + 1) * Q_BLOCK:(2 * i + 2) * Q_BLOCK]
            o_ref[0, :, p * LANES:(p + 1) * LANES] = jnp.where(first, lo, hi).astype(o_ref.dtype)


def _window_attention(q, k2, v2, kc2, vc2, sinks):
    b, n, width = q.shape
    m = kc2.shape[1]
    kvw = k2.shape[2]
    n_blocks = n // Q_BLOCK
    kern = functools.partial(_swa_kernel, n_blocks=n_blocks)
    return pl.pallas_call(
        kern, out_shape=jax.ShapeDtypeStruct((b, n, width), BF16),
        grid=(b, n_blocks),
        in_specs=[pl.BlockSpec(memory_space=pltpu.SMEM),
                  pl.BlockSpec((1, Q_BLOCK, width), lambda i, j: (i, j, 0)),
                  pl.BlockSpec((1, n, kvw), lambda i, j: (i, 0, 0)),
                  pl.BlockSpec((1, n, kvw), lambda i, j: (i, 0, 0)),
                  pl.BlockSpec((1, m, kvw), lambda i, j: (i, 0, 0)),
                  pl.BlockSpec((1, m, kvw), lambda i, j: (i, 0, 0))],
        out_specs=pl.BlockSpec((1, Q_BLOCK, width), lambda i, j: (i, j, 0)),
        compiler_params=_params(("parallel", "parallel")), name="window_attention",
    )(sinks.astype(F32), q, k2, v2, kc2, vc2)


def _conv_kernel(x_ref, prev_ref, next_ref, w_ref, b_ref, o_ref):
    i = pl.program_id(1)
    tm = x_ref.shape[1]
    keep_prev = jnp.where(i > 0, 1.0, 0.0)
    keep_next = jnp.where(i < pl.num_programs(1) - 1, 1.0, 0.0)
    xe = jnp.concatenate([prev_ref[0] * keep_prev, x_ref[0], next_ref[0] * keep_next], axis=0)
    acc = jnp.zeros((tm, x_ref.shape[2]), F32) + b_ref[...]
    for k in range(D_CONV):
        off = 8 - D_CONV // 2 + k
        acc = acc + w_ref[k:k + 1, :] * xe[off:off + tm, :]
    o_ref[0] = _silu(acc)


def _conv_silu(x, conv_w, conv_b):
    b, length, ch = x.shape
    tm = min(512, length)
    per = tm // 8
    last = length // 8 - 1
    return pl.pallas_call(
        _conv_kernel, out_shape=jax.ShapeDtypeStruct((b, length, ch), F32),
        grid=(b, length // tm),
        in_specs=[pl.BlockSpec((1, tm, ch), lambda i, j: (i, j, 0)),
                  pl.BlockSpec((1, 8, ch), lambda i, j: (i, jnp.maximum(j * per - 1, 0), 0)),
                  pl.BlockSpec((1, 8, ch), lambda i, j: (i, jnp.minimum((j + 1) * per, last), 0)),
                  pl.BlockSpec((D_CONV, ch), lambda i, j: (0, 0)),
                  pl.BlockSpec((1, ch), lambda i, j: (0, 0))],
        out_specs=pl.BlockSpec((1, tm, ch), lambda i, j: (i, j, 0)),
        compiler_params=_params(("parallel", "parallel")), name="conv_silu",
    )(x, x, x, conv_w.reshape(D_CONV, ch).astype(F32), conv_b.reshape(1, ch).astype(F32))


def _ssd_kernel(*refs, reverse, finalize, d_inner, d_state, n_groups):
    u_ref, dt_ref, dtt_ref, a_ref, at_ref, bias_ref, biast_ref, init_ref = refs[:8]
    pos = 8
    if finalize:
        z_ref, yb_ref, skip_ref, gn_ref = refs[pos:pos + 4]
        pos += 4
    y_ref, fin_ref, st_ref = refs[pos:pos + 3]
    c = pl.program_id(1)

    @pl.when(c == 0)
    def _():
        st_ref[...] = init_ref[0]

    q_len = SSD_CHUNK
    row = lax.broadcasted_iota(I32, (q_len, q_len), 0)
    col = lax.broadcasted_iota(I32, (q_len, q_len), 1)
    lower = row >= col
    upper = row <= col
    causal = upper if reverse else lower
    tri_col = causal.astype(F32)
    tri_row = (lower if reverse else upper).astype(F32)
    first = _half_masks((q_len, LANES))
    heads_per_group = (d_inner // HEAD_DIM) // n_groups
    pairs_per_group = heads_per_group // 2
    n_sub = u_ref.shape[1] // q_len
    for sc in (reversed(range(n_sub)) if reverse else range(n_sub)):
        _ssd_chunk(slice(sc * q_len, (sc + 1) * q_len), refs, causal, tri_col, tri_row, first, pairs_per_group,
                   finalize, d_inner, d_state, n_groups)

    @pl.when(c == pl.num_programs(1) - 1)
    def _():
        fin_ref[0] = st_ref[...]


def _ssd_chunk(rows, refs, causal, tri_col, tri_row, first, pairs_per_group, finalize, d_inner, d_state, n_groups):
    u_ref, dt_ref, dtt_ref, a_ref, at_ref, bias_ref, biast_ref, _ = refs[:8]
    pos = 8
    if finalize:
        z_ref, yb_ref, skip_ref, gn_ref = refs[pos:pos + 4]
        pos += 4
    y_ref, _, st_ref = refs[pos:pos + 3]
    u = u_ref[0, rows, :]
    xs = u[:, :d_inner]
    dt = _softplus(dt_ref[0, rows, :] + bias_ref[...])
    dtt = _softplus(dtt_ref[0, :, rows] + biast_ref[...])
    da = dt * a_ref[...]
    dat = dtt * at_ref[...]
    q_col = jnp.dot(tri_col, da, precision=HIGHEST, preferred_element_type=F32)
    q_row = jnp.dot(dat, tri_row, precision=HIGHEST, preferred_element_type=F32)
    total = jnp.sum(da, axis=0, keepdims=True)
    y_pairs = []
    for g in range(n_groups):
        b_g = u[:, d_inner + g * d_state:d_inner + (g + 1) * d_state]
        c_g = u[:, d_inner + (n_groups + g) * d_state:d_inner + (n_groups + g + 1) * d_state]
        c_bf = c_g.astype(BF16)
        cb = _dot_nt(c_bf, b_g.astype(BF16))
        b_t = b_g.T.astype(BF16)
        for pp in range(pairs_per_group):
            p = g * pairs_per_group + pp
            h0, h1 = 2 * p, 2 * p + 1
            x_pair = xs[:, p * LANES:(p + 1) * LANES]
            dt_pair = jnp.where(first, dt[:, h0:h0 + 1], dt[:, h1:h1 + 1])
            qc_pair = jnp.where(first, q_col[:, h0:h0 + 1], q_col[:, h1:h1 + 1])
            tot_pair = jnp.where(first[:1], total[:, h0:h0 + 1], total[:, h1:h1 + 1])
            xdt = x_pair * dt_pair
            xdt_bf = xdt.astype(BF16)
            diag = []
            for h in (h0, h1):
                diff = q_col[:, h:h + 1] - q_row[h:h + 1, :]
                lmat = jnp.exp(jnp.where(causal, diff, NEG_BIG))
                diag.append(_dot((cb * lmat).astype(BF16), xdt_bf))
            y_diag = jnp.where(first, diag[0], diag[1])
            state = st_ref[p]
            y_off = _dot(c_bf, state.astype(BF16)) * jnp.exp(qc_pair)
            y_pairs.append(y_diag + y_off)
            carry_in = (xdt * jnp.exp(tot_pair - qc_pair)).astype(BF16)
            st_ref[p] = jnp.exp(tot_pair) * state + _dot(b_t, carry_in)
    y = jnp.concatenate(y_pairs, axis=1)
    if finalize:
        y = y + yb_ref[0, rows, :] + skip_ref[...] * xs
        y = y * _silu(z_ref[0, rows, :])
        gw = d_inner // n_groups
        y = jnp.concatenate(
            [y[:, g * gw:(g + 1) * gw] * lax.rsqrt(
                jnp.mean(y[:, g * gw:(g + 1) * gw] ** 2, axis=1, keepdims=True) + NORM_EPS)
             for g in range(n_groups)], axis=1)
        y = y * gn_ref[...]
    y_ref[0, rows, :] = y.astype(y_ref.dtype)


def _ssd_scan(u, dt, dtt, a, dt_bias, init, *, reverse, d_inner, d_state, n_groups, final=None):
    b, length, ch = u.shape
    heads = dt.shape[2]
    step_rows = SSD_CHUNK * math.gcd(length // SSD_CHUNK, 4)
    nc = length // step_rows
    pairs = heads // 2
    cidx = (lambda j: nc - 1 - j) if reverse else (lambda j: j)
    small = lambda shape: pl.BlockSpec(shape, lambda i, j: (0,) * len(shape))
    in_specs = [pl.BlockSpec((1, step_rows, ch), lambda i, j: (i, cidx(j), 0)),
                pl.BlockSpec((1, step_rows, heads), lambda i, j: (i, cidx(j), 0)),
                pl.BlockSpec((1, heads, step_rows), lambda i, j: (i, 0, cidx(j))),
                small((1, heads)), small((heads, 1)), small((1, heads)), small((heads, 1)),
                pl.BlockSpec((1, pairs, d_state, LANES), lambda i, j: (i, 0, 0, 0))]
    args = [u, dt, dtt, a.reshape(1, heads), a.reshape(heads, 1),
            dt_bias.reshape(1, heads).astype(F32), dt_bias.reshape(heads, 1).astype(F32), init]
    out_dtype = F32
    if final is not None:
        z, yb, d_skip, gnorm_g = final
        in_specs += [pl.BlockSpec((1, step_rows, d_inner), lambda i, j: (i, cidx(j), 0)),
                     pl.BlockSpec((1, step_rows, d_inner), lambda i, j: (i, cidx(j), 0)),
                     small((1, d_inner)), small((1, d_inner))]
        args += [z, yb, jnp.repeat(d_skip.astype(F32), HEAD_DIM).reshape(1, d_inner),
                 gnorm_g.reshape(1, d_inner).astype(F32)]
        out_dtype = BF16
    kern = functools.partial(_ssd_kernel, reverse=reverse, finalize=final is not None,
                             d_inner=d_inner, d_state=d_state, n_groups=n_groups)
    return pl.pallas_call(
        kern,
        out_shape=[jax.ShapeDtypeStruct((b, length, d_inner), out_dtype),
                   jax.ShapeDtypeStruct((b, pairs, d_state, LANES), F32)],
        grid=(b, nc), in_specs=in_specs,
        out_specs=[pl.BlockSpec((1, step_rows, d_inner), lambda i, j: (i, cidx(j), 0)),
                   pl.BlockSpec((1, pairs, d_state, LANES), lambda i, j: (i, 0, 0, 0))],
        scratch_shapes=[pltpu.VMEM((pairs, d_state, LANES), F32)],
        compiler_params=_params(("parallel", "arbitrary")), name="ssd_scan",
    )(*args)


def _outproj_kernel(oa_ref, ob_ref, w_ref, x_ref, gpost_ref, gate_ref, gpre_ref, sc_ref, sh_ref, wrh_ref, wrl_ref,
                    xo_ref, h_ref, aff_ref, *, row_chunk):
    half = oa_ref.shape[2]
    for r0 in range(0, x_ref.shape[1], row_chunk):
        rows = slice(r0, r0 + row_chunk)
        mix = _dot(oa_ref[0, rows, :], w_ref[:half, :]) + _dot(ob_ref[0, rows, :], w_ref[half:, :])
        x_new = x_ref[0, rows, :] + gate_ref[0] * _rms(mix, gpost_ref[...])
        xo_ref[0, rows, :] = x_new
        h = _rms(x_new, gpre_ref[...]) * sc_ref[0] + sh_ref[0]
        h_ref[0, rows, :] = h
        h_hi = h.astype(BF16)
        h_lo = (h - h_hi.astype(F32)).astype(BF16)
        logits = _dot_nt(wrh_ref[...], h_hi) + _dot_nt(wrh_ref[...], h_lo) + _dot_nt(wrl_ref[...], h_hi)
        e = jnp.exp(logits - jnp.max(logits, axis=0, keepdims=True))
        aff_ref[0, :, rows] = e / jnp.sum(e, axis=0, keepdims=True)


def _out_project(oa, ob, w_out, x, g_post, gate, g_pre, scale1p, shift, w_router):
    b, n, d = x.shape
    half = oa.shape[2]
    n_exp = w_router.shape[1]
    tm = min(512, n)
    bm = gate.shape[0]
    mod_map = (lambda i, j: (i, 0, 0)) if bm > 1 else (lambda i, j: (0, 0, 0))
    row = pl.BlockSpec((1, d), lambda i, j: (0, 0))
    mod = pl.BlockSpec((1, 1, d), mod_map)
    tile = pl.BlockSpec((1, tm, d), lambda i, j: (i, j, 0))
    act = pl.BlockSpec((1, tm, half), lambda i, j: (i, j, 0))
    wr_spec = pl.BlockSpec((n_exp, d), lambda i, j: (0, 0))
    wr = w_router.T.astype(F32)
    wr_hi = wr.astype(BF16)
    wr_lo = (wr - wr_hi.astype(F32)).astype(BF16)
    return pl.pallas_call(
        functools.partial(_outproj_kernel, row_chunk=min(256, tm)),
        out_shape=[jax.ShapeDtypeStruct((b, n, d), F32), jax.ShapeDtypeStruct((b, n, d), F32),
                   jax.ShapeDtypeStruct((b, n_exp, n), F32)],
        grid=(b, n // tm),
        in_specs=[act, act, pl.BlockSpec((2 * half, d), lambda i, j: (0, 0)), tile, row, mod, row, mod, mod,
                  wr_spec, wr_spec],
        out_specs=[tile, tile, pl.BlockSpec((1, n_exp, tm), lambda i, j: (i, 0, j))],
        compiler_params=_params(("parallel", "parallel")), name="out_proj_router",
    )(oa, ob, w_out, x, g_post.reshape(1, d), gate, g_pre.reshape(1, d), scale1p, shift, wr_hi, wr_lo)


def _topk_kernel(aff_ref, idx_ref, gate_ref, work_ref, key_ref, *, cap, slot_block):
    n_exp, n = aff_ref.shape[1], aff_ref.shape[2]
    n_blk = n // LANES
    aff = aff_ref[0]

    def search(i, thr):
        cand = thr | jnp.left_shift(jnp.int32(1), 30 - i)
        cnt = jnp.sum(jnp.where(aff_ref[0] >= pltpu.bitcast(cand, F32), 1.0, 0.0), axis=1, keepdims=True)
        return jnp.where(cnt >= cap, cand, thr)

    thr = lax.fori_loop(0, 31, search, jnp.zeros((n_exp, 1), I32))
    above = aff >= pltpu.bitcast(thr + 1, F32)
    tied = (aff >= pltpu.bitcast(thr, F32)) & ~above
    need = cap - jnp.sum(jnp.where(above, 1.0, 0.0), axis=1, keepdims=True)

    r = lax.broadcasted_iota(I32, (LANES, LANES), 0)
    c = lax.broadcasted_iota(I32, (LANES, LANES), 1)
    incl = (r <= c).astype(BF16)

    def exclusive_prefix(mask_f32):
        starts = [jnp.zeros((n_exp, 1), F32)]
        for j in range(n_blk):
            blk = mask_f32[:, j * LANES:(j + 1) * LANES]
            run = _dot(blk.astype(BF16), incl)
            work_ref[:, j * LANES:(j + 1) * LANES] = run - blk + starts[-1]
            starts.append(starts[-1] + jnp.sum(blk, axis=1, keepdims=True))
        return starts

    exclusive_prefix(jnp.where(tied, 1.0, 0.0))
    sel = above | (tied & (work_ref[...] < need))
    sel_f = jnp.where(sel, 1.0, 0.0)
    gate_ref[0] = jnp.where(sel, aff, 0.0)
    starts = exclusive_prefix(sel_f)
    key = jnp.where(sel, work_ref[...], -1.0)
    for j in range(n_blk):
        key_ref[j] = key[:, j * LANES:(j + 1) * LANES]

    lane = lax.broadcasted_iota(I32, (slot_block, LANES), 1).astype(F32)
    slot = lax.broadcasted_iota(I32, (slot_block, LANES), 0).astype(F32)

    for sb in range(cap // slot_block):
        first = sum(jnp.where(starts[j + 1] <= sb * slot_block, 1.0, 0.0) for j in range(n_blk))
        last = sum(jnp.where(starts[j] < (sb + 1) * slot_block, 1.0, 0.0) for j in range(n_blk))
        want = slot + float(sb * slot_block)
        for e in range(n_exp):
            def per_block(j, acc, e=e, want=want):
                key_row = key_ref[j, e:e + 1, :]
                tok = lane + lax.convert_element_type(j * LANES, F32)
                return acc + jnp.where(key_row == want, tok, 0.0)

            acc = lax.fori_loop(first[e, 0].astype(I32), last[e, 0].astype(I32), per_block,
                                jnp.zeros((slot_block, LANES), F32))
            idx_ref[0, e, sb * slot_block:(sb + 1) * slot_block, :] = (
                jnp.sum(acc, axis=1, keepdims=True).astype(I32))


def _route(aff_t):
    b, n_exp, n = aff_t.shape
    cap = CAPACITY_FACTOR * n // n_exp
    slot_block = min(LANES, cap)
    kern = functools.partial(_topk_kernel, cap=cap, slot_block=slot_block)
    idx, gate = pl.pallas_call(
        kern,
        out_shape=[jax.ShapeDtypeStruct((b, n_exp, cap, 1), I32), jax.ShapeDtypeStruct((b, n_exp, n), F32)],
        grid=(b,),
        in_specs=[pl.BlockSpec((1, n_exp, n), lambda i: (i, 0, 0))],
        out_specs=[pl.BlockSpec((1, n_exp, cap, 1), lambda i: (i, 0, 0, 0)),
                   pl.BlockSpec((1, n_exp, n), lambda i: (i, 0, 0))],
        scratch_shapes=[pltpu.VMEM((n_exp, n), F32), pltpu.VMEM((n // LANES, n_exp, LANES), F32)],
        compiler_params=_params(("parallel",)), name="expert_choice_route",
    )(aff_t)
    return idx.reshape(b * n_exp, 1, cap), gate.reshape(b * n_exp, 1, n)


def _moe_kernel(idx_ref, nxt_ref, h_ref, wg_ref, wu_ref, wd_ref, y_ref, xg_ref, *, cap, prefetch):
    sub = lax.broadcasted_iota(I32, (SUBLANES, h_ref.shape[2]), 0)

    def gather_group(src_ref, dst, base):
        tile = jnp.zeros(sub.shape, F32)
        for u in range(SUBLANES):
            t = src_ref[0, 0, base + u]
            r = t & (SUBLANES - 1)
            rows = h_ref[0, pl.ds(pl.multiple_of(t - r, SUBLANES), SUBLANES), :]
            rolled = pltpu.roll(rows, (u - r) & (SUBLANES - 1), axis=0)
            tile = jnp.where(sub == u, rolled, tile)
        dst[pl.ds(base, SUBLANES), :] = tile

    def gather_loop(dst):
        def body(i8, carry):
            gather_group(idx_ref, dst, pl.multiple_of(i8 * SUBLANES, SUBLANES))
            return carry
        lax.fori_loop(0, cap // SUBLANES, body, 0)

    if prefetch:
        e = pl.program_id(1)
        slot = e & 1

        @pl.when(e == 0)
        def _():
            gather_loop(xg_ref.at[0])
    else:
        slot = 0
        gather_loop(xg_ref.at[0])
    xg = xg_ref[slot].astype(BF16)
    gate = _dot(xg, wg_ref[0])
    up = _dot(xg, wu_ref[0])
    hid = (_silu(gate) * up).astype(BF16)
    y_ref[0, 0] = _dot(hid, wd_ref[0])
    if prefetch:
        for i8 in range(cap // SUBLANES):
            gather_group(nxt_ref, xg_ref.at[1 - slot], i8 * SUBLANES)


def _combine_kernel(idx_ref, gate_ref, y_ref, acc_ref, *, cap):
    @pl.when(pl.program_id(1) == 0)
    def _():
        acc_ref[...] = jnp.zeros_like(acc_ref)

    sub = lax.broadcasted_iota(I32, (SUBLANES, acc_ref.shape[2]), 0)

    def scatter(i8, carry):
        base = pl.multiple_of(i8 * SUBLANES, SUBLANES)
        y_tile = y_ref[0, 0, pl.ds(base, SUBLANES), :]
        for u0 in range(0, SUBLANES, 4):
            pending = []
            for u in range(u0, u0 + 4):
                t = idx_ref[0, 0, base + u]
                g = gate_ref[0, 0, t]
                r = t & (SUBLANES - 1)
                dst = acc_ref.at[0, pl.ds(pl.multiple_of(t - r, SUBLANES), SUBLANES), :]
                rolled = pltpu.roll(y_tile, (r - u) & (SUBLANES - 1), axis=0)
                pending.append((dst, dst[...] + g * rolled, sub == r))
            for dst, new, mask in pending:
                pltpu.store(dst, new, mask=mask)
        return carry

    lax.fori_loop(0, cap // SUBLANES, scatter, 0)


def _expert_ffn(h, idx, gate, wg, wu, wd):
    b, n, d = h.shape
    n_exp, _, ff = wg.shape
    cap = idx.shape[2]
    smem = lambda width: pl.BlockSpec((1, 1, width), lambda i, e: (i * n_exp + e, 0, 0), memory_space=pltpu.SMEM)
    sample_major = n * d * h.dtype.itemsize >= 3 * d * ff * wg.dtype.itemsize
    if sample_major:
        grid = (b, n_exp)
        be = lambda i, e: (i, e)
        nxt = pl.BlockSpec((1, 1, cap), lambda i, e: (i * n_exp + jnp.minimum(e + 1, n_exp - 1), 0, 0),
                           memory_space=pltpu.SMEM)
        h_spec = pl.BlockSpec((1, n, d), lambda i, e: (i, 0, 0), pipeline_mode=pl.Buffered(1))
    else:
        grid = (n_exp, b)
        be = lambda e, i: (i, e)
        nxt = pl.BlockSpec((1, 1, cap), lambda e, i: (i * n_exp + e, 0, 0), memory_space=pltpu.SMEM)
        h_spec = pl.BlockSpec((1, n, d), lambda e, i: (i, 0, 0))
    y = pl.pallas_call(
        functools.partial(_moe_kernel, cap=cap, prefetch=sample_major),
        out_shape=jax.ShapeDtypeStruct((b, n_exp, cap, d), F32),
        grid=grid,
        in_specs=[pl.BlockSpec((1, 1, cap), lambda *g: (be(*g)[0] * n_exp + be(*g)[1], 0, 0),
                               memory_space=pltpu.SMEM),
                  nxt, h_spec,
                  pl.BlockSpec((1, d, ff), lambda *g: (be(*g)[1], 0, 0)),
                  pl.BlockSpec((1, d, ff), lambda *g: (be(*g)[1], 0, 0)),
                  pl.BlockSpec((1, ff, d), lambda *g: (be(*g)[1], 0, 0))],
        out_specs=pl.BlockSpec((1, 1, cap, d), lambda *g: (*be(*g), 0, 0)),
        scratch_shapes=[pltpu.VMEM((2, cap, d), F32)],
        compiler_params=_params(("parallel", "arbitrary")), name="expert_ffn",
    )(idx, idx, h, wg, wu, wd)
    return pl.pallas_call(
        functools.partial(_combine_kernel, cap=cap),
        out_shape=jax.ShapeDtypeStruct((b, n, d), F32),
        grid=(b, n_exp),
        in_specs=[smem(cap), smem(n), pl.BlockSpec((1, 1, cap, d), lambda i, e: (i, e, 0, 0))],
        out_specs=pl.BlockSpec((1, n, d), lambda i, e: (i, 0, 0)),
        compiler_params=_params(("parallel", "arbitrary")), name="expert_combine",
    )(idx, gate, y)


def _residual_kernel(x_ref, f_ref, g_ref, gate_ref, o_ref):
    o_ref[0] = x_ref[0] + gate_ref[0] * _rms(f_ref[0], g_ref[...])


def _gated_residual(x, f, g, gate):
    b, n, d = x.shape
    tm = min(512, n)
    bm = gate.shape[0]
    mod_map = (lambda i, j: (i, 0, 0)) if bm > 1 else (lambda i, j: (0, 0, 0))
    tile = pl.BlockSpec((1, tm, d), lambda i, j: (i, j, 0))
    return pl.pallas_call(
        _residual_kernel, out_shape=jax.ShapeDtypeStruct((b, n, d), F32),
        grid=(b, n // tm),
        in_specs=[tile, tile, pl.BlockSpec((1, d), lambda i, j: (0, 0)), pl.BlockSpec((1, 1, d), mod_map)],
        out_specs=tile,
        compiler_params=_params(("parallel", "parallel")), name="gated_residual",
    )(x, f, g.reshape(1, d), gate)


def _rope_tables(n):
    t = jnp.arange(n, dtype=I32)
    row = (t // GRID_W).astype(F32)
    col = (t % GRID_W).astype(F32)
    n_freq = HEAD_DIM // 4
    inv_freq = ROPE_THETA ** (-jnp.arange(n_freq, dtype=F32) / n_freq)
    ang = jnp.concatenate([row[:, None] * inv_freq, col[:, None] * inv_freq], axis=-1)
    cos, sin = jnp.cos(ang), jnp.sin(ang)
    cos_t = jnp.tile(jnp.concatenate([cos, cos], axis=-1), (1, LANES // HEAD_DIM))
    sin_t = jnp.tile(jnp.concatenate([-sin, sin], axis=-1), (1, LANES // HEAD_DIM))
    return cos_t, sin_t


def _deinterleave_heads(w):
    d, width = w.shape
    w = w.reshape(d, width // HEAD_DIM, HEAD_DIM // 2, 2)
    return jnp.concatenate([w[..., 0], w[..., 1]], axis=-1).reshape(d, width)


def _dup_heads(w):
    d, width = w.shape
    w = w.reshape(d, width // HEAD_DIM, 1, HEAD_DIM)
    return jnp.concatenate([w, w], axis=2).reshape(d, 2 * width)


def _mod_split(mod_rows):
    sh1, sc1, gt1, sh2, sc2, gt2 = jnp.split(mod_rows[:, None, :], 6, axis=-1)
    return sh1, 1.0 + sc1, gt1, sh2, 1.0 + sc2, gt2


def _moe_block(x_mid, h2, aff_t, g_post_ffn, gt2, wg, wu, wd):
    b, n, d = h2.shape
    n_exp, _, ff = wg.shape
    idx, gate = _route(aff_t)
    if b > 1 and n * d * h2.dtype.itemsize < 3 * d * ff * wg.dtype.itemsize:
        cap = idx.shape[2]
        idx = idx.reshape(b, n_exp, cap) + (jnp.arange(b, dtype=I32) * n)[:, None, None]
        idx = idx.transpose(1, 0, 2).reshape(n_exp, 1, b * cap)
        gate = gate.reshape(b, n_exp, n).transpose(1, 0, 2).reshape(n_exp, 1, b * n)
        ffn = _expert_ffn(h2.reshape(1, b * n, d), idx, gate, wg, wu, wd).reshape(b, n, d)
    else:
        ffn = _expert_ffn(h2, idx, gate, wg, wu, wd)
    return _gated_residual(x_mid, ffn, g_post_ffn, gt2)


def kernel(x, c, ctx, c_ctx, l0_w_mod, l0_b_mod, l0_g_pre_mix, l0_g_post_mix, l0_g_pre_ffn, l0_g_post_ffn, l0_w_in, l0_w_out, l0_lam_q1, l0_lam_k1, l0_lam_q2, l0_lam_k2, l0_subln_g, l0_rpb, l0_w_router, l0_w_gate, l0_w_up, l0_w_down, l1_w_mod, l1_b_mod, l1_g_pre_mix, l1_g_post_mix, l1_g_pre_ffn, l1_g_post_ffn, l1_w_in, l1_w_out, l1_sinks, l1_conv_w, l1_conv_b, l1_a_log_f, l1_a_log_b, l1_dt_bias_f, l1_dt_bias_b, l1_d_skip, l1_gnorm_g, l1_w_router, l1_w_gate, l1_w_up, l1_w_down):
    b, n, d = x.shape
    m = ctx.shape[1]
    half = d // 2
    scale = HEAD_DIM ** -0.5 * math.log2(math.e)
    cos_t, sin_t = _rope_tables(n)
    cond = jnp.zeros((16, d), F32).at[:b].set(c).at[b].set(c_ctx)

    mod = _modulation(cond, l0_w_mod, l0_b_mod)
    sh1, sc1, gt1, sh2, sc2, gt2 = _mod_split(mod[:b])
    csh1, csc1, cgt1, csh2, csc2, cgt2 = _mod_split(mod[b:b + 1])

    a_qk = half
    wq, wk, rest = l0_w_in[:, :a_qk], l0_w_in[:, a_qk:2 * a_qk], l0_w_in[:, 2 * a_qk:]
    qb_cols = slice(half, 2 * half)
    rest = rest.at[:, qb_cols].multiply(scale)
    w_in0 = jnp.concatenate([_deinterleave_heads(wq) * scale, _deinterleave_heads(wk), rest], axis=1).astype(BF16)
    segs0 = [(half, BF16)] * 6
    qa, ka, va, qb, kb, vb = _project(x, l0_g_pre_mix, sc1, sh1, w_in0, segs0, rope=(2 * a_qk, cos_t, sin_t))
    qa_c, ka_c, va_c, qb_c, kb_c, vb_c = _project(ctx, l0_g_pre_mix, csc1, csh1, w_in0, segs0)

    lam_init = 0.8 - 0.6 * math.exp(-0.3 * 0)
    lam = (jnp.exp(jnp.sum(l0_lam_q1.astype(F32) * l0_lam_k1.astype(F32)))
           - jnp.exp(jnp.sum(l0_lam_q2.astype(F32) * l0_lam_k2.astype(F32))) + lam_init)
    o_a = _diff_attention(qa, [(ka, va), (ka_c, va_c)], lam, l0_subln_g, 1.0 - lam_init)
    o_b = _neighbourhood_attention(qb, kb, vb, kb_c, vb_c, l0_rpb)
    oc_a = _diff_attention(qa_c, [(ka_c, va_c)], lam, l0_subln_g, 1.0 - lam_init)
    oc_b = _context_attention(qb_c, kb_c, vb_c)

    w_out0 = l0_w_out.astype(BF16)
    wg0, wu0, wd0 = l0_w_gate.astype(BF16), l0_w_up.astype(BF16), l0_w_down.astype(BF16)
    x_mid, h2, aff_t = _out_project(o_a, o_b, w_out0, x, l0_g_post_mix, gt1, l0_g_pre_ffn, sc2, sh2, l0_w_router)
    x1 = _moe_block(x_mid, h2, aff_t, l0_g_post_ffn, gt2, wg0, wu0, wd0)
    c_mid, hc2, aff_c = _out_project(oc_a, oc_b, w_out0, ctx, l0_g_post_mix, cgt1, l0_g_pre_ffn, csc2, csh2,
                                     l0_w_router)
    ctx1 = _moe_block(c_mid, hc2, aff_c, l0_g_post_ffn, cgt2, wg0, wu0, wd0)

    mod = _modulation(cond, l1_w_mod, l1_b_mod)
    sh1, sc1, gt1, sh2, sc2, gt2 = _mod_split(mod[:b])
    csh1, csc1, _, _, _, _ = _mod_split(mod[b:b + 1])

    c_q = half
    d_inner = half
    d_heads = l1_a_log_f.shape[0]
    d_xbc = l1_conv_w.shape[2]
    c_kv = (l1_w_in.shape[1] - c_q - d_xbc - d_inner - 2 * d_heads) // 2
    n_groups = 2
    d_state = (d_xbc - d_inner) // (2 * n_groups)
    wq = l1_w_in[:, :c_q]
    wk = l1_w_in[:, c_q:c_q + c_kv]
    wv = l1_w_in[:, c_q + c_kv:c_q + 2 * c_kv]
    o_x = c_q + 2 * c_kv
    w_x = l1_w_in[:, o_x:o_x + d_xbc + d_inner]
    w_dt = l1_w_in[:, o_x + d_xbc + d_inner:]
    w_kv = jnp.concatenate([_dup_heads(_deinterleave_heads(wk)), _dup_heads(wv), w_x, w_dt], axis=1)
    w_lat = jnp.concatenate([_deinterleave_heads(wq) * scale, w_kv], axis=1).astype(BF16)
    w_ctx = w_kv.astype(BF16)
    w_dt_t = w_dt.T.astype(BF16)
    segs_lat = [(c_q, BF16), (2 * c_kv, BF16), (2 * c_kv, BF16), (d_xbc, F32), (d_inner, F32), (2 * d_heads, F32)]
    q, k2, v2, xbc, z, dt, dtt = _project(x1, l1_g_pre_mix, sc1, sh1, w_lat, segs_lat,
                                          rope=(c_q + 2 * c_kv, cos_t, sin_t), wt=w_dt_t)
    k2_c, v2_c, xbc_c, _, dt_c, dtt_c = _project(ctx1, l1_g_pre_mix, csc1, csh1, w_ctx, segs_lat[1:], wt=w_dt_t)

    o_c = _window_attention(q, k2, v2, k2_c, v2_c, l1_sinks.astype(F32) * math.log2(math.e))

    a_f = -jnp.exp(l1_a_log_f.astype(F32))
    a_b = -jnp.exp(l1_a_log_b.astype(F32))
    u = _conv_silu(xbc, l1_conv_w, l1_conv_b)
    u_c = _conv_silu(xbc_c, l1_conv_w, l1_conv_b)
    ssd = functools.partial(_ssd_scan, d_inner=d_inner, d_state=d_state, n_groups=n_groups)
    zero_state = jnp.zeros((b, d_heads // 2, d_state, LANES), F32)
    _, sc_f = ssd(u_c, dt_c[..., :d_heads], dtt_c[:, :d_heads], a_f, l1_dt_bias_f, zero_state, reverse=False)
    _, sc_b = ssd(u_c, dt_c[..., d_heads:], dtt_c[:, d_heads:], a_b, l1_dt_bias_b, zero_state, reverse=True)
    y_b, _ = ssd(u, dt[..., d_heads:], dtt[:, d_heads:], a_b, l1_dt_bias_b, sc_b, reverse=True)
    o_d, _ = ssd(u, dt[..., :d_heads], dtt[:, :d_heads], a_f, l1_dt_bias_f, sc_f, reverse=False,
                 final=(z, y_b, l1_d_skip, l1_gnorm_g))

    x_mid, h2, aff_t = _out_project(o_c, o_d, l1_w_out.astype(BF16), x1, l1_g_post_mix, gt1, l1_g_pre_ffn, sc2, sh2,
                                    l1_w_router)
    return _moe_block(x_mid, h2, aff_t, l1_g_post_ffn, gt2, l1_w_gate.astype(BF16), l1_w_up.astype(BF16),
                      l1_w_down.astype(BF16))
```

```python
import functools
import math

import jax
import jax.numpy as jnp
from jax import lax
from jax.experimental import pallas as pl
from jax.experimental.pallas import tpu as pltpu

F32 = jnp.float32
BF16 = jnp.bfloat16
I32 = jnp.int32

HEAD_DIM = 64
LANES = 128
SUBLANES = 8
GRID_W = 64
ROPE_THETA = 10000.0
NORM_EPS = 1e-6
NA_ROWS = 8
NA_COLS = 16
WINDOW = 128
Q_BLOCK = 128
SSD_CHUNK = 128
D_CONV = 5
N_EXPERTS = 16
CAPACITY_FACTOR = 2
NEG_BIG = -1e30
VMEM_LIMIT = 56 * 1024 * 1024
HIGHEST = lax.Precision.HIGHEST
NT_DIMS = (((1,), (1,)), ((), ()))


def _params(sem, vmem=VMEM_LIMIT):
    return pltpu.CompilerParams(dimension_semantics=sem, vmem_limit_bytes=vmem)


def _rms(u, g):
    return u * lax.rsqrt(jnp.mean(u * u, axis=-1, keepdims=True) + NORM_EPS) * g


def _silu(u):
    return u * (1.0 / (1.0 + jnp.exp(-u)))


def _softplus(u):
    return jnp.maximum(u, 0.0) + jnp.log(1.0 + jnp.exp(-jnp.abs(u)))


def _dot(a, b):
    return jnp.dot(a, b, preferred_element_type=F32)


def _dot_nt(a, b):
    return lax.dot_general(a, b, NT_DIMS, preferred_element_type=F32)


def _half_masks(shape):
    lane = lax.broadcasted_iota(I32, shape, 1)
    return (lane % LANES) < HEAD_DIM


def _mod_kernel(c_ref, w_ref, b_ref, o_ref):
    s = _silu(c_ref[...]).astype(BF16)
    o_ref[...] = _dot(s, w_ref[...].astype(BF16)) + b_ref[...]


def _modulation(cond, w_mod, b_mod):
    rows, d = cond.shape
    n_out = w_mod.shape[1]
    tn = 1024
    return pl.pallas_call(
        _mod_kernel,
        out_shape=jax.ShapeDtypeStruct((rows, n_out), F32),
        grid=(n_out // tn,),
        in_specs=[pl.BlockSpec((rows, d), lambda j: (0, 0)),
                  pl.BlockSpec((d, tn), lambda j: (0, j)),
                  pl.BlockSpec((1, tn), lambda j: (0, j))],
        out_specs=pl.BlockSpec((rows, tn), lambda j: (0, j)),
        compiler_params=_params(("parallel",)),
        name="modulation",
    )(cond, w_mod, b_mod.reshape(1, n_out))


def _rope_block(u, cos, sin_signed, first_half):
    width = u.shape[1]
    fwd = pltpu.roll(u, HEAD_DIM // 2, axis=1)
    bwd = pltpu.roll(u, width - HEAD_DIM // 2, axis=1)
    return u * cos + jnp.where(first_half, bwd, fwd) * sin_signed


def _proj_kernel(*refs, seg_widths, rope_cols, has_t, col_chunk):
    x_ref, g_ref, sc_ref, sh_ref, w_ref = refs[:5]
    pos = 5
    if rope_cols:
        cos_ref, sin_ref = refs[pos:pos + 2]
        pos += 2
    if has_t:
        wt_ref = refs[pos]
        pos += 1
    out_refs = refs[pos:]
    h = (_rms(x_ref[0], g_ref[...]) * sc_ref[0] + sh_ref[0]).astype(BF16)
    if rope_cols:
        cos = cos_ref[...]
        sin_signed = sin_ref[...]
        first_half = (lax.broadcasted_iota(I32, cos.shape, 1) % HEAD_DIM) < HEAD_DIM // 2
    col = 0
    for o_ref, width in zip(out_refs, seg_widths):
        for c0 in range(0, width, col_chunk):
            cw = min(col_chunk, width - c0)
            acc = _dot(h, w_ref[:, col + c0:col + c0 + cw])
            if col + c0 < rope_cols:
                acc = jnp.concatenate(
                    [_rope_block(acc[:, k:k + LANES], cos, sin_signed, first_half)
                     for k in range(0, cw, LANES)], axis=1)
            o_ref[0, :, c0:c0 + cw] = acc.astype(o_ref.dtype)
        col += width
    if has_t:
        out_refs[-1][0] = _dot_nt(wt_ref[...], h)


def _project(x, g, scale1p, shift, w, segs, rope=None, wt=None):
    b, n, d = x.shape
    tm = min(512, n)
    bm = scale1p.shape[0]
    mod_map = (lambda i, j: (i, 0, 0)) if bm > 1 else (lambda i, j: (0, 0, 0))
    in_specs = [pl.BlockSpec((1, tm, d), lambda i, j: (i, j, 0)),
                pl.BlockSpec((1, d), lambda i, j: (0, 0)),
                pl.BlockSpec((1, 1, d), mod_map),
                pl.BlockSpec((1, 1, d), mod_map),
                pl.BlockSpec(w.shape, lambda i, j: (0, 0))]
    args = [x, g.reshape(1, d), scale1p, shift, w]
    rope_cols = 0
    if rope is not None:
        rope_cols, cos, sin_signed = rope
        in_specs += [pl.BlockSpec((tm, LANES), lambda i, j: (j, 0))] * 2
        args += [cos, sin_signed]
    if wt is not None:
        in_specs.append(pl.BlockSpec(wt.shape, lambda i, j: (0, 0)))
        args.append(wt)
    out_shape = [jax.ShapeDtypeStruct((b, n, wd), dt) for wd, dt in segs]
    out_specs = [pl.BlockSpec((1, tm, wd), lambda i, j: (i, j, 0)) for wd, _ in segs]
    if wt is not None:
        out_shape.append(jax.ShapeDtypeStruct((b, wt.shape[0], n), F32))
        out_specs.append(pl.BlockSpec((1, wt.shape[0], tm), lambda i, j: (i, 0, j)))
    kern = functools.partial(_proj_kernel, seg_widths=tuple(wd for wd, _ in segs),
                             rope_cols=rope_cols, has_t=wt is not None, col_chunk=512)
    return pl.pallas_call(
        kern, out_shape=out_shape, grid=(b, n // tm), in_specs=in_specs, out_specs=out_specs,
        compiler_params=_params(("parallel", "parallel")), name="in_proj",
    )(*args)


def _diff_kernel(*refs, seg_lens, tk, out_scale, heads):
    n_seg = len(seg_lens)
    m_all, acc_all, s_all = refs[4 + 2 * n_seg:]
    for h in range(heads):
        _diff_head(refs[:4 + 2 * n_seg], m_all.at[h], acc_all.at[h], s_all.at[2 * (h % 2)],
                   s_all.at[2 * (h % 2) + 1], slice(h * LANES, (h + 1) * LANES), seg_lens, tk, out_scale)


def _diff_head(io_refs, m_ref, acc_ref, s0_ref, s1_ref, cols, seg_lens, tk, out_scale):
    lam_ref, q_ref, g_ref = io_refs[:3]
    n_seg = len(seg_lens)
    kv_refs = [r.at[:, :, cols] for r in io_refs[3:3 + 2 * n_seg]]
    o_ref = io_refs[3 + 2 * n_seg]
    qf = q_ref[0, :, cols].astype(F32)
    first = _half_masks(qf.shape)
    qs = (jnp.where(first, qf, 0.0).astype(BF16), jnp.where(first, 0.0, qf).astype(BF16))
    m_ref[...] = jnp.full(m_ref.shape, NEG_BIG, F32)
    acc_ref[...] = jnp.zeros(acc_ref.shape, F32)

    def scores(kc):
        return [_dot_nt(qs[t], kc) for t in range(2)]

    def absorb(sc, vc):
        v_ones = jnp.concatenate([vc, jnp.ones(vc.shape, BF16)], axis=1)
        for t in range(2):
            m_old = m_ref[t]
            m_new = jnp.maximum(m_old, jnp.max(sc[t], axis=1, keepdims=True))
            p = jnp.exp2(sc[t] - m_new).astype(BF16)
            acc_ref[t] = jnp.exp2(m_old - m_new) * acc_ref[t] + _dot(p, v_ones)
            m_ref[t] = m_new

    def put(buf, sc):
        buf[0] = sc[0]
        buf[1] = sc[1]

    main = max(range(n_seg), key=lambda i: seg_lens[i])
    n_chunks = seg_lens[main] // tk if seg_lens[main] >= 2 * tk else 1
    if n_chunks > 1:
        assert n_chunks % 2 == 0 and seg_lens[main] == n_chunks * tk
        k_ref, v_ref = kv_refs[2 * main], kv_refs[2 * main + 1]
        rows = lambda ref, c: ref[0, pl.ds(pl.multiple_of(c * tk, tk), tk), :]
    singles = [s_i for s_i in range(n_seg) if s_i != main or n_chunks == 1]
    single_scores = [scores(kv_refs[2 * s_i][0]) for s_i in singles]
    if n_chunks > 1:
        put(s0_ref, scores(rows(k_ref, 0)))
    for s_i, sc in zip(singles, single_scores):
        absorb(sc, kv_refs[2 * s_i + 1][0])
    if n_chunks > 1:
        def pair(c, last):
            put(s1_ref, scores(rows(k_ref, c + 1)))
            absorb([s0_ref[0], s0_ref[1]], rows(v_ref, c))
            if not last:
                put(s0_ref, scores(rows(k_ref, c + 2)))
            absorb([s1_ref[0], s1_ref[1]], rows(v_ref, c + 1))

        def body(j, carry):
            pair(2 * j, False)
            return carry

        lax.fori_loop(0, n_chunks // 2 - 1, body, 0)
        pair(n_chunks - 2, True)
    ratio0 = acc_ref[0, :, :LANES] / acc_ref[0, :, LANES:]
    ratio1 = acc_ref[1, :, :LANES] / acc_ref[1, :, LANES:]
    o = ratio0 - lam_ref[0] * ratio1
    o_ref[0, :, cols] = (_rms(o, g_ref[...]) * out_scale).astype(o_ref.dtype)


def _diff_attention(q, kv_segs, lam, subln_g, out_scale):
    b, nq, width = q.shape
    heads = width // LANES
    tq = min(512, nq)
    in_specs = [pl.BlockSpec(memory_space=pltpu.SMEM),
                pl.BlockSpec((1, tq, width), lambda i, j: (i, j, 0)),
                pl.BlockSpec((1, LANES), lambda i, j: (0, 0))]
    args = [lam.reshape(1).astype(F32), q, subln_g.reshape(1, LANES).astype(F32)]
    seg_lens = []
    for k, v in kv_segs:
        length = k.shape[1]
        seg_lens.append(length)
        in_specs += [pl.BlockSpec((1, length, width), lambda i, j: (i, 0, 0))] * 2
        args += [k, v]
    tk = 512
    kern = functools.partial(_diff_kernel, seg_lens=tuple(seg_lens), tk=tk, out_scale=out_scale, heads=heads)
    return pl.pallas_call(
        kern, out_shape=jax.ShapeDtypeStruct((b, nq, width), BF16),
        grid=(b, nq // tq), in_specs=in_specs,
        out_specs=pl.BlockSpec((1, tq, width), lambda i, j: (i, j, 0)),
        scratch_shapes=[pltpu.VMEM((heads, 2, tq, 1), F32), pltpu.VMEM((heads, 2, tq, 2 * LANES), F32),
                        pltpu.VMEM((4, 2, tq, tk), F32)],
        compiler_params=_params(("parallel", "parallel")), name="diff_attention",
    )(*args)


NA_BLOCK_ROWS = 4
NA_WIN_ROWS = NA_BLOCK_ROWS + NA_ROWS


def _na_window_start(blk, rows):
    return jnp.clip(blk * NA_BLOCK_ROWS - NA_ROWS // 2, 0, rows - NA_WIN_ROWS)


def _natten_kernel(q_ref, k_ref, v_ref, kc_ref, vc_ref, bias_ref, o_ref, *, rows):
    nq = NA_BLOCK_ROWS * GRID_W
    per_step = q_ref.shape[1] // nq
    for sb in range(per_step):
        _natten_block(q_ref, k_ref, v_ref, kc_ref, vc_ref, bias_ref, o_ref, pl.program_id(1) * per_step + sb,
                      slice(sb * nq, (sb + 1) * nq), rows)


def _natten_block(q_ref, k_ref, v_ref, kc_ref, vc_ref, bias_ref, o_ref, blk, qrows, rows):
    start = pl.multiple_of(_na_window_start(blk, rows) * GRID_W, GRID_W)
    n_keys = NA_WIN_ROWS * GRID_W
    nq = NA_BLOCK_ROWS * GRID_W
    case = jnp.where(blk == 0, 0, jnp.where(blk == rows // NA_BLOCK_ROWS - 1, 2, 1))
    q = q_ref[0, qrows, :].astype(F32)
    kwin = k_ref[0, pl.ds(start, n_keys), :]
    vwin = v_ref[0, pl.ds(start, n_keys), :]
    kc = kc_ref[0]
    vc = vc_ref[0]
    first = _half_masks((nq, LANES))
    for p in range(q.shape[1] // LANES):
        cols = slice(p * LANES, (p + 1) * LANES)
        qs = jnp.concatenate([jnp.where(first, q[:, cols], 0.0), jnp.where(first, 0.0, q[:, cols])],
                             axis=0).astype(BF16)
        s_nb = _dot_nt(qs, kwin[:, cols]) + jnp.concatenate([bias_ref[case, 2 * p], bias_ref[case, 2 * p + 1]],
                                                            axis=0)
        s_cx = _dot_nt(qs, kc[:, cols])
        m = jnp.maximum(jnp.max(s_nb, axis=1, keepdims=True), jnp.max(s_cx, axis=1, keepdims=True))
        p_nb = jnp.exp2(s_nb - m).astype(BF16)
        p_cx = jnp.exp2(s_cx - m).astype(BF16)
        v_nb = jnp.concatenate([vwin[:, cols], jnp.ones((n_keys, LANES), BF16)], axis=1)
        v_cx = jnp.concatenate([vc[:, cols], jnp.ones((vc.shape[0], LANES), BF16)], axis=1)
        acc = _dot(p_nb, v_nb) + _dot(p_cx, v_cx)
        ratio = acc[:, :LANES] / acc[:, LANES:]
        o_ref[0, qrows, cols] = jnp.where(first, ratio[:nq], ratio[nq:]).astype(o_ref.dtype)


def _na_bias_table(rpb, rows):
    heads = rpb.shape[0]
    w = jnp.arange(GRID_W)
    kc = jnp.arange(GRID_W)
    col_start = jnp.clip(w - NA_COLS // 2, 0, GRID_W - NA_COLS)
    valid = (kc[None, :] >= col_start[:, None]) & (kc[None, :] < col_start[:, None] + NA_COLS)
    pad = GRID_W - NA_COLS
    padded = jnp.pad(rpb.astype(F32) * math.log2(math.e), ((0, 0), (0, 0), (pad, pad)))
    cols = jnp.stack([padded[:, :, GRID_W - 1 - wi:2 * GRID_W - 1 - wi] for wi in range(GRID_W)], axis=2)
    cols = jnp.where(valid[None, None], cols, NEG_BIG)
    masked = jnp.full((heads, GRID_W, GRID_W), NEG_BIG, F32)
    cases = []
    for first_row in (0, NA_BLOCK_ROWS, rows - NA_BLOCK_ROWS):
        win0 = min(max(first_row - NA_ROWS // 2, 0), rows - NA_WIN_ROWS)
        per_q = []
        for qr in range(NA_BLOCK_ROWS):
            r = first_row + qr
            rs = min(max(r - NA_ROWS // 2, 0), rows - NA_ROWS)
            per_k = [cols[:, win0 + kr - r + NA_ROWS - 1] if rs <= win0 + kr < rs + NA_ROWS else masked
                     for kr in range(NA_WIN_ROWS)]
            per_q.append(jnp.stack(per_k, axis=2))
        cases.append(jnp.stack(per_q, axis=1))
    return jnp.stack(cases).reshape(3, heads, NA_BLOCK_ROWS * GRID_W, NA_WIN_ROWS * GRID_W)


def _neighbourhood_attention(q, k, v, kc, vc, rpb):
    b, n, width = q.shape
    m = kc.shape[1]
    rows = n // GRID_W
    n_blk = rows // NA_BLOCK_ROWS
    assert rows % NA_BLOCK_ROWS == 0 and rows >= NA_WIN_ROWS + NA_BLOCK_ROWS
    bias = _na_bias_table(rpb, rows)
    heads = rpb.shape[0]
    step_q = NA_BLOCK_ROWS * GRID_W * math.gcd(n_blk, 2)
    kern = functools.partial(_natten_kernel, rows=rows)
    return pl.pallas_call(
        kern, out_shape=jax.ShapeDtypeStruct((b, n, width), BF16),
        grid=(b, n // step_q),
        in_specs=[pl.BlockSpec((1, step_q, width), lambda i, r: (i, r, 0)),
                  pl.BlockSpec((1, n, width), lambda i, r: (i, 0, 0)),
                  pl.BlockSpec((1, n, width), lambda i, r: (i, 0, 0)),
                  pl.BlockSpec((1, m, width), lambda i, r: (i, 0, 0)),
                  pl.BlockSpec((1, m, width), lambda i, r: (i, 0, 0)),
                  pl.BlockSpec(bias.shape, lambda i, r: (0, 0, 0, 0), pipeline_mode=pl.Buffered(1))],
        out_specs=pl.BlockSpec((1, step_q, width), lambda i, r: (i, r, 0)),
        compiler_params=_params(("parallel", "parallel")), name="neighbourhood_attention",
    )(q, k, v, kc, vc, bias)


def _ctx_attn_kernel(q_ref, k_ref, v_ref, o_ref):
    q = q_ref[0].astype(F32)
    k = k_ref[0]
    v = v_ref[0]
    first = _half_masks((q.shape[0], LANES))
    for p in range(q.shape[1] // LANES):
        cols = slice(p * LANES, (p + 1) * LANES)
        outs = []
        for sub in range(2):
            qm = jnp.where(first if sub == 0 else ~first, q[:, cols], 0.0).astype(BF16)
            s = _dot_nt(qm, k[:, cols])
            e = jnp.exp2(s - jnp.max(s, axis=1, keepdims=True))
            outs.append(_dot(e.astype(BF16), v[:, cols]) / jnp.sum(e, axis=1, keepdims=True))
        o_ref[0, :, cols] = jnp.where(first, outs[0], outs[1]).astype(o_ref.dtype)


def _context_attention(q, k, v):
    b, m, width = q.shape
    spec = pl.BlockSpec((1, m, width), lambda i: (i, 0, 0))
    return pl.pallas_call(
        _ctx_attn_kernel, out_shape=jax.ShapeDtypeStruct((b, m, width), BF16),
        grid=(b,), in_specs=[spec, spec, spec], out_specs=spec,
        compiler_params=_params(("parallel",)), name="context_attention",
    )(q, k, v)


def _swa_kernel(sink_ref, q_ref, k_ref, v_ref, kc_ref, vc_ref, o_ref, *, n_blocks):
    for sb in range(q_ref.shape[1] // Q_BLOCK):
        _swa_block(sink_ref, q_ref, k_ref, v_ref, kc_ref, vc_ref, o_ref, pl.program_id(1) * (q_ref.shape[1] // Q_BLOCK) + sb,
                   slice(sb * Q_BLOCK, (sb + 1) * Q_BLOCK), n_blocks)


def _swa_block(sink_ref, q_ref, k_ref, v_ref, kc_ref, vc_ref, o_ref, blk, qrows, n_blocks):
    span = 3 * Q_BLOCK
    start = pl.multiple_of(jnp.clip(blk - 1, 0, n_blocks - 3) * Q_BLOCK, Q_BLOCK)
    q = q_ref[0, qrows, :].astype(F32)
    kwin = k_ref[0, pl.ds(start, span), :]
    vwin = v_ref[0, pl.ds(start, span), :]
    kc = kc_ref[0]
    vc = vc_ref[0]
    first = _half_masks((Q_BLOCK, LANES))
    pairs = q.shape[1] // LANES
    n_groups = k_ref.shape[2] // LANES
    pairs_per_group = pairs // n_groups
    stack = 2 * pairs_per_group
    row = lax.broadcasted_iota(I32, (stack * Q_BLOCK, span), 0)
    qpos = blk * Q_BLOCK + (row & (Q_BLOCK - 1))
    kpos = start + lax.broadcasted_iota(I32, (stack * Q_BLOCK, span), 1)
    valid = jnp.abs(qpos - kpos) <= WINDOW
    head_of_row = lax.broadcasted_iota(I32, (stack * Q_BLOCK, 1), 0) >> (Q_BLOCK.bit_length() - 1)
    for g in range(n_groups):
        gcols = slice(g * LANES, (g + 1) * LANES)
        parts = []
        for p in range(g * pairs_per_group, (g + 1) * pairs_per_group):
            q_pair = q[:, p * LANES:(p + 1) * LANES]
            parts += [jnp.where(first, q_pair, 0.0), jnp.where(first, 0.0, q_pair)]
        qs = jnp.concatenate(parts, axis=0).astype(BF16)
        sink = jnp.zeros((stack * Q_BLOCK, 1), F32)
        for i in range(stack):
            sink = jnp.where(head_of_row == i, sink_ref[g * stack + i], sink)
        s_loc = jnp.where(valid, _dot_nt(qs, kwin[:, gcols]), NEG_BIG)
        s_cx = _dot_nt(qs, kc[:, gcols])
        m = jnp.maximum(jnp.maximum(jnp.max(s_loc, axis=1, keepdims=True),
                                    jnp.max(s_cx, axis=1, keepdims=True)), sink)
        p_loc = jnp.exp2(s_loc - m).astype(BF16)
        p_cx = jnp.exp2(s_cx - m).astype(BF16)
        v_loc = jnp.concatenate([vwin[:, gcols], jnp.ones((span, LANES), BF16)], axis=1)
        v_cx = jnp.concatenate([vc[:, gcols], jnp.ones((vc.shape[0], LANES), BF16)], axis=1)
        acc = _dot(p_loc, v_loc) + _dot(p_cx, v_cx)
        ratio = acc[:, :LANES] / (acc[:, LANES:] + jnp.exp2(sink - m))
        for i in range(pairs_per_group):
            p = g * pairs_per_group + i
            lo = ratio[2 * i * Q_BLOCK:(2 * i + 1) * Q_BLOCK]
            hi = ratio[(2 * i + 1) * Q_BLOCK:(2 * i + 2) * Q_BLOCK]
            o_ref[0, qrows, p * LANES:(p + 1) * LANES] = jnp.where(first, lo, hi).astype(o_ref.dtype)


def _window_attention(q, k2, v2, kc2, vc2, sinks):
    b, n, width = q.shape
    m = kc2.shape[1]
    kvw = k2.shape[2]
    n_blocks = n // Q_BLOCK
    step_rows = Q_BLOCK * math.gcd(n_blocks, 4)
    kern = functools.partial(_swa_kernel, n_blocks=n_blocks)
    return pl.pallas_call(
        kern, out_shape=jax.ShapeDtypeStruct((b, n, width), BF16),
        grid=(b, n // step_rows),
        in_specs=[pl.BlockSpec(memory_space=pltpu.SMEM),
                  pl.BlockSpec((1, step_rows, width), lambda i, j: (i, j, 0)),
                  pl.BlockSpec((1, n, kvw), lambda i, j: (i, 0, 0)),
                  pl.BlockSpec((1, n, kvw), lambda i, j: (i, 0, 0)),
                  pl.BlockSpec((1, m, kvw), lambda i, j: (i, 0, 0)),
                  pl.BlockSpec((1, m, kvw), lambda i, j: (i, 0, 0))],
        out_specs=pl.BlockSpec((1, step_rows, width), lambda i, j: (i, j, 0)),
        compiler_params=_params(("parallel", "parallel")), name="window_attention",
    )(sinks.astype(F32), q, k2, v2, kc2, vc2)


def _conv_kernel(x_ref, prev_ref, next_ref, w_ref, b_ref, o_ref):
    i = pl.program_id(1)
    tm = x_ref.shape[1]
    keep_prev = jnp.where(i > 0, 1.0, 0.0)
    keep_next = jnp.where(i < pl.num_programs(1) - 1, 1.0, 0.0)
    xe = jnp.concatenate([prev_ref[0] * keep_prev, x_ref[0], next_ref[0] * keep_next], axis=0)
    acc = jnp.zeros((tm, x_ref.shape[2]), F32) + b_ref[...]
    for k in range(D_CONV):
        off = 8 - D_CONV // 2 + k
        acc = acc + w_ref[k:k + 1, :] * xe[off:off + tm, :]
    o_ref[0] = _silu(acc)


def _conv_silu(x, conv_w, conv_b):
    b, length, ch = x.shape
    tm = min(512, length)
    per = tm // 8
    last = length // 8 - 1
    return pl.pallas_call(
        _conv_kernel, out_shape=jax.ShapeDtypeStruct((b, length, ch), F32),
        grid=(b, length // tm),
        in_specs=[pl.BlockSpec((1, tm, ch), lambda i, j: (i, j, 0)),
                  pl.BlockSpec((1, 8, ch), lambda i, j: (i, jnp.maximum(j * per - 1, 0), 0)),
                  pl.BlockSpec((1, 8, ch), lambda i, j: (i, jnp.minimum((j + 1) * per, last), 0)),
                  pl.BlockSpec((D_CONV, ch), lambda i, j: (0, 0)),
                  pl.BlockSpec((1, ch), lambda i, j: (0, 0))],
        out_specs=pl.BlockSpec((1, tm, ch), lambda i, j: (i, j, 0)),
        compiler_params=_params(("parallel", "parallel")), name="conv_silu",
    )(x, x, x, conv_w.reshape(D_CONV, ch).astype(F32), conv_b.reshape(1, ch).astype(F32))


def _ssd_kernel(*refs, reverse, finalize, d_inner, d_state, n_groups):
    u_ref, dt_ref, dtt_ref, a_ref, at_ref, bias_ref, biast_ref, init_ref = refs[:8]
    pos = 8
    if finalize:
        z_ref, yb_ref, skip_ref, gn_ref = refs[pos:pos + 4]
        pos += 4
    y_ref, fin_ref, st_ref = refs[pos:pos + 3]
    c = pl.program_id(1)

    @pl.when(c == 0)
    def _():
        st_ref[...] = init_ref[0]

    q_len = SSD_CHUNK
    row = lax.broadcasted_iota(I32, (q_len, q_len), 0)
    col = lax.broadcasted_iota(I32, (q_len, q_len), 1)
    lower = row >= col
    upper = row <= col
    causal = upper if reverse else lower
    tri_col = causal.astype(F32)
    tri_row = (lower if reverse else upper).astype(F32)
    first = _half_masks((q_len, LANES))
    heads_per_group = (d_inner // HEAD_DIM) // n_groups
    pairs_per_group = heads_per_group // 2
    n_sub = u_ref.shape[1] // q_len
    for sc in (reversed(range(n_sub)) if reverse else range(n_sub)):
        _ssd_chunk(slice(sc * q_len, (sc + 1) * q_len), refs, causal, tri_col, tri_row, first, pairs_per_group,
                   finalize, d_inner, d_state, n_groups)

    @pl.when(c == pl.num_programs(1) - 1)
    def _():
        fin_ref[0] = st_ref[...]


def _ssd_chunk(rows, refs, causal, tri_col, tri_row, first, pairs_per_group, finalize, d_inner, d_state, n_groups):
    u_ref, dt_ref, dtt_ref, a_ref, at_ref, bias_ref, biast_ref, _ = refs[:8]
    pos = 8
    if finalize:
        z_ref, yb_ref, skip_ref, gn_ref = refs[pos:pos + 4]
        pos += 4
    y_ref, _, st_ref = refs[pos:pos + 3]
    u = u_ref[0, rows, :]
    xs = u[:, :d_inner]
    dt = _softplus(dt_ref[0, rows, :] + bias_ref[...])
    dtt = _softplus(dtt_ref[0, :, rows] + biast_ref[...])
    da = dt * a_ref[...]
    dat = dtt * at_ref[...]
    q_col = jnp.dot(tri_col, da, precision=HIGHEST, preferred_element_type=F32)
    q_row = jnp.dot(dat, tri_row, precision=HIGHEST, preferred_element_type=F32)
    total = jnp.sum(da, axis=0, keepdims=True)
    y_pairs = []
    for g in range(n_groups):
        b_g = u[:, d_inner + g * d_state:d_inner + (g + 1) * d_state]
        c_g = u[:, d_inner + (n_groups + g) * d_state:d_inner + (n_groups + g + 1) * d_state]
        c_bf = c_g.astype(BF16)
        cb = _dot_nt(c_bf, b_g.astype(BF16))
        b_t = b_g.T.astype(BF16)
        for pp in range(pairs_per_group):
            p = g * pairs_per_group + pp
            h0, h1 = 2 * p, 2 * p + 1
            x_pair = xs[:, p * LANES:(p + 1) * LANES]
            dt_pair = jnp.where(first, dt[:, h0:h0 + 1], dt[:, h1:h1 + 1])
            qc_pair = jnp.where(first, q_col[:, h0:h0 + 1], q_col[:, h1:h1 + 1])
            tot_pair = jnp.where(first[:1], total[:, h0:h0 + 1], total[:, h1:h1 + 1])
            xdt = x_pair * dt_pair
            xdt_bf = xdt.astype(BF16)
            diag = []
            for h in (h0, h1):
                diff = q_col[:, h:h + 1] - q_row[h:h + 1, :]
                lmat = jnp.exp(jnp.where(causal, diff, NEG_BIG))
                diag.append(_dot((cb * lmat).astype(BF16), xdt_bf))
            y_diag = jnp.where(first, diag[0], diag[1])
            state = st_ref[p]
            y_off = _dot(c_bf, state.astype(BF16)) * jnp.exp(qc_pair)
            y_pairs.append(y_diag + y_off)
            carry_in = (xdt * jnp.exp(tot_pair - qc_pair)).astype(BF16)
            st_ref[p] = jnp.exp(tot_pair) * state + _dot(b_t, carry_in)
    y = jnp.concatenate(y_pairs, axis=1)
    if finalize:
        y = y + yb_ref[0, rows, :] + skip_ref[...] * xs
        y = y * _silu(z_ref[0, rows, :])
        gw = d_inner // n_groups
        y = jnp.concatenate(
            [y[:, g * gw:(g + 1) * gw] * lax.rsqrt(
                jnp.mean(y[:, g * gw:(g + 1) * gw] ** 2, axis=1, keepdims=True) + NORM_EPS)
             for g in range(n_groups)], axis=1)
        y = y * gn_ref[...]
    y_ref[0, rows, :] = y.astype(y_ref.dtype)


def _ssd_scan(u, dt, dtt, a, dt_bias, init, *, reverse, d_inner, d_state, n_groups, final=None):
    b, length, ch = u.shape
    heads = dt.shape[2]
    step_rows = SSD_CHUNK * math.gcd(length // SSD_CHUNK, 4)
    nc = length // step_rows
    pairs = heads // 2
    cidx = (lambda j: nc - 1 - j) if reverse else (lambda j: j)
    small = lambda shape: pl.BlockSpec(shape, lambda i, j: (0,) * len(shape))
    in_specs = [pl.BlockSpec((1, step_rows, ch), lambda i, j: (i, cidx(j), 0)),
                pl.BlockSpec((1, step_rows, heads), lambda i, j: (i, cidx(j), 0)),
                pl.BlockSpec((1, heads, step_rows), lambda i, j: (i, 0, cidx(j))),
                small((1, heads)), small((heads, 1)), small((1, heads)), small((heads, 1)),
                pl.BlockSpec((1, pairs, d_state, LANES), lambda i, j: (i, 0, 0, 0))]
    args = [u, dt, dtt, a.reshape(1, heads), a.reshape(heads, 1),
            dt_bias.reshape(1, heads).astype(F32), dt_bias.reshape(heads, 1).astype(F32), init]
    out_dtype = F32
    if final is not None:
        z, yb, d_skip, gnorm_g = final
        in_specs += [pl.BlockSpec((1, step_rows, d_inner), lambda i, j: (i, cidx(j), 0)),
                     pl.BlockSpec((1, step_rows, d_inner), lambda i, j: (i, cidx(j), 0)),
                     small((1, d_inner)), small((1, d_inner))]
        args += [z, yb, jnp.repeat(d_skip.astype(F32), HEAD_DIM).reshape(1, d_inner),
                 gnorm_g.reshape(1, d_inner).astype(F32)]
        out_dtype = BF16
    kern = functools.partial(_ssd_kernel, reverse=reverse, finalize=final is not None,
                             d_inner=d_inner, d_state=d_state, n_groups=n_groups)
    return pl.pallas_call(
        kern,
        out_shape=[jax.ShapeDtypeStruct((b, length, d_inner), out_dtype),
                   jax.ShapeDtypeStruct((b, pairs, d_state, LANES), F32)],
        grid=(b, nc), in_specs=in_specs,
        out_specs=[pl.BlockSpec((1, step_rows, d_inner), lambda i, j: (i, cidx(j), 0)),
                   pl.BlockSpec((1, pairs, d_state, LANES), lambda i, j: (i, 0, 0, 0))],
        scratch_shapes=[pltpu.VMEM((pairs, d_state, LANES), F32)],
        compiler_params=_params(("parallel", "arbitrary")), name="ssd_scan",
    )(*args)


def _outproj_kernel(oa_ref, ob_ref, w_ref, x_ref, gpost_ref, gate_ref, gpre_ref, sc_ref, sh_ref, wrh_ref, wrl_ref,
                    xo_ref, h_ref, aff_ref, *, row_chunk):
    half = oa_ref.shape[2]
    for r0 in range(0, x_ref.shape[1], row_chunk):
        rows = slice(r0, r0 + row_chunk)
        mix = _dot(oa_ref[0, rows, :], w_ref[:half, :]) + _dot(ob_ref[0, rows, :], w_ref[half:, :])
        x_new = x_ref[0, rows, :] + gate_ref[0] * _rms(mix, gpost_ref[...])
        xo_ref[0, rows, :] = x_new
        h = _rms(x_new, gpre_ref[...]) * sc_ref[0] + sh_ref[0]
        h_ref[0, rows, :] = h
        h_hi = h.astype(BF16)
        h_lo = (h - h_hi.astype(F32)).astype(BF16)
        logits = _dot_nt(wrh_ref[...], h_hi) + _dot_nt(wrh_ref[...], h_lo) + _dot_nt(wrl_ref[...], h_hi)
        e = jnp.exp(logits - jnp.max(logits, axis=0, keepdims=True))
        aff_ref[0, :, rows] = e / jnp.sum(e, axis=0, keepdims=True)


def _out_project(oa, ob, w_out, x, g_post, gate, g_pre, scale1p, shift, w_router):
    b, n, d = x.shape
    half = oa.shape[2]
    n_exp = w_router.shape[1]
    tm = min(512, n)
    bm = gate.shape[0]
    mod_map = (lambda i, j: (i, 0, 0)) if bm > 1 else (lambda i, j: (0, 0, 0))
    row = pl.BlockSpec((1, d), lambda i, j: (0, 0))
    mod = pl.BlockSpec((1, 1, d), mod_map)
    tile = pl.BlockSpec((1, tm, d), lambda i, j: (i, j, 0))
    act = pl.BlockSpec((1, tm, half), lambda i, j: (i, j, 0))
    wr_spec = pl.BlockSpec((n_exp, d), lambda i, j: (0, 0))
    wr = w_router.T.astype(F32)
    wr_hi = wr.astype(BF16)
    wr_lo = (wr - wr_hi.astype(F32)).astype(BF16)
    return pl.pallas_call(
        functools.partial(_outproj_kernel, row_chunk=min(256, tm)),
        out_shape=[jax.ShapeDtypeStruct((b, n, d), F32), jax.ShapeDtypeStruct((b, n, d), F32),
                   jax.ShapeDtypeStruct((b, n_exp, n), F32)],
        grid=(b, n // tm),
        in_specs=[act, act, pl.BlockSpec((2 * half, d), lambda i, j: (0, 0)), tile, row, mod, row, mod, mod,
                  wr_spec, wr_spec],
        out_specs=[tile, tile, pl.BlockSpec((1, n_exp, tm), lambda i, j: (i, 0, j))],
        compiler_params=_params(("parallel", "parallel")), name="out_proj_router",
    )(oa, ob, w_out, x, g_post.reshape(1, d), gate, g_pre.reshape(1, d), scale1p, shift, wr_hi, wr_lo)


def _topk_kernel(aff_ref, idx_ref, gate_ref, work_ref, key_ref, *, cap, slot_block):
    n_exp, n = aff_ref.shape[1], aff_ref.shape[2]
    n_blk = n // LANES
    aff = aff_ref[0]

    def search(i, thr):
        cand = thr | jnp.left_shift(jnp.int32(1), 30 - i)
        cnt = jnp.sum(jnp.where(aff_ref[0] >= pltpu.bitcast(cand, F32), 1.0, 0.0), axis=1, keepdims=True)
        return jnp.where(cnt >= cap, cand, thr)

    thr = lax.fori_loop(0, 31, search, jnp.zeros((n_exp, 1), I32))
    above = aff >= pltpu.bitcast(thr + 1, F32)
    tied = (aff >= pltpu.bitcast(thr, F32)) & ~above
    need = cap - jnp.sum(jnp.where(above, 1.0, 0.0), axis=1, keepdims=True)

    r = lax.broadcasted_iota(I32, (LANES, LANES), 0)
    c = lax.broadcasted_iota(I32, (LANES, LANES), 1)
    incl = (r <= c).astype(BF16)

    def exclusive_prefix(mask_f32):
        starts = [jnp.zeros((n_exp, 1), F32)]
        for j in range(n_blk):
            blk = mask_f32[:, j * LANES:(j + 1) * LANES]
            run = _dot(blk.astype(BF16), incl)
            work_ref[:, j * LANES:(j + 1) * LANES] = run - blk + starts[-1]
            starts.append(starts[-1] + jnp.sum(blk, axis=1, keepdims=True))
        return starts

    exclusive_prefix(jnp.where(tied, 1.0, 0.0))
    sel = above | (tied & (work_ref[...] < need))
    sel_f = jnp.where(sel, 1.0, 0.0)
    gate_ref[0] = jnp.where(sel, aff, 0.0)
    starts = exclusive_prefix(sel_f)
    key = jnp.where(sel, work_ref[...], -1.0)
    for j in range(n_blk):
        key_ref[j] = key[:, j * LANES:(j + 1) * LANES]

    lane = lax.broadcasted_iota(I32, (slot_block, LANES), 1).astype(F32)
    slot = lax.broadcasted_iota(I32, (slot_block, LANES), 0).astype(F32)

    for sb in range(cap // slot_block):
        first = sum(jnp.where(starts[j + 1] <= sb * slot_block, 1.0, 0.0) for j in range(n_blk))
        last = sum(jnp.where(starts[j] < (sb + 1) * slot_block, 1.0, 0.0) for j in range(n_blk))
        want = slot + float(sb * slot_block)
        for e in range(n_exp):
            def per_block(j, acc, e=e, want=want):
                key_row = key_ref[j, e:e + 1, :]
                tok = lane + lax.convert_element_type(j * LANES, F32)
                return acc + jnp.where(key_row == want, tok, 0.0)

            acc = lax.fori_loop(first[e, 0].astype(I32), last[e, 0].astype(I32), per_block,
                                jnp.zeros((slot_block, LANES), F32))
            idx_ref[0, e, sb * slot_block:(sb + 1) * slot_block, :] = (
                jnp.sum(acc, axis=1, keepdims=True).astype(I32))


def _route(aff_t):
    b, n_exp, n = aff_t.shape
    cap = CAPACITY_FACTOR * n // n_exp
    slot_block = min(LANES, cap)
    kern = functools.partial(_topk_kernel, cap=cap, slot_block=slot_block)
    idx, gate = pl.pallas_call(
        kern,
        out_shape=[jax.ShapeDtypeStruct((b, n_exp, cap, 1), I32), jax.ShapeDtypeStruct((b, n_exp, n), F32)],
        grid=(b,),
        in_specs=[pl.BlockSpec((1, n_exp, n), lambda i: (i, 0, 0))],
        out_specs=[pl.BlockSpec((1, n_exp, cap, 1), lambda i: (i, 0, 0, 0)),
                   pl.BlockSpec((1, n_exp, n), lambda i: (i, 0, 0))],
        scratch_shapes=[pltpu.VMEM((n_exp, n), F32), pltpu.VMEM((n // LANES, n_exp, LANES), F32)],
        compiler_params=_params(("parallel",)), name="expert_choice_route",
    )(aff_t)
    return idx.reshape(b * n_exp, 1, cap), gate.reshape(b * n_exp, 1, n)


def _moe_kernel(idx_ref, nxt_ref, h_ref, wg_ref, wu_ref, wd_ref, y_ref, xg_ref, *, cap, prefetch):
    sub = lax.broadcasted_iota(I32, (SUBLANES, h_ref.shape[2]), 0)

    def gather_group(src_ref, dst, base):
        tile = jnp.zeros(sub.shape, F32)
        for u in range(SUBLANES):
            t = src_ref[0, 0, base + u]
            r = t & (SUBLANES - 1)
            rows = h_ref[0, pl.ds(pl.multiple_of(t - r, SUBLANES), SUBLANES), :]
            rolled = pltpu.roll(rows, (u - r) & (SUBLANES - 1), axis=0)
            tile = jnp.where(sub == u, rolled, tile)
        dst[pl.ds(base, SUBLANES), :] = tile

    def gather_loop(dst):
        def body(i8, carry):
            gather_group(idx_ref, dst, pl.multiple_of(i8 * SUBLANES, SUBLANES))
            return carry
        lax.fori_loop(0, cap // SUBLANES, body, 0)

    if prefetch:
        e = pl.program_id(1)
        slot = e & 1

        @pl.when(e == 0)
        def _():
            gather_loop(xg_ref.at[0])
    else:
        slot = 0
        gather_loop(xg_ref.at[0])
    xg = xg_ref[slot].astype(BF16)
    gate = _dot(xg, wg_ref[0])
    up = _dot(xg, wu_ref[0])
    hid = (_silu(gate) * up).astype(BF16)
    y_ref[0, 0] = _dot(hid, wd_ref[0])
    if prefetch:
        for i8 in range(cap // SUBLANES):
            gather_group(nxt_ref, xg_ref.at[1 - slot], i8 * SUBLANES)


def _combine_kernel(idx_ref, gate_ref, y_ref, acc_ref, *, cap):
    @pl.when(pl.program_id(1) == 0)
    def _():
        acc_ref[...] = jnp.zeros_like(acc_ref)

    sub = lax.broadcasted_iota(I32, (SUBLANES, acc_ref.shape[2]), 0)

    def scatter(i8, carry):
        base = pl.multiple_of(i8 * SUBLANES, SUBLANES)
        y_tile = y_ref[0, 0, pl.ds(base, SUBLANES), :]
        for u0 in range(0, SUBLANES, 4):
            pending = []
            for u in range(u0, u0 + 4):
                t = idx_ref[0, 0, base + u]
                g = gate_ref[0, 0, t]
                r = t & (SUBLANES - 1)
                dst = acc_ref.at[0, pl.ds(pl.multiple_of(t - r, SUBLANES), SUBLANES), :]
                rolled = pltpu.roll(y_tile, (r - u) & (SUBLANES - 1), axis=0)
                pending.append((dst, dst[...] + g * rolled, sub == r))
            for dst, new, mask in pending:
                pltpu.store(dst, new, mask=mask)
        return carry

    lax.fori_loop(0, cap // SUBLANES, scatter, 0)


def _expert_ffn(h, idx, gate, wg, wu, wd):
    b, n, d = h.shape
    n_exp, _, ff = wg.shape
    cap = idx.shape[2]
    smem = lambda width: pl.BlockSpec((1, 1, width), lambda i, e: (i * n_exp + e, 0, 0), memory_space=pltpu.SMEM)
    sample_major = n * d * h.dtype.itemsize >= 3 * d * ff * wg.dtype.itemsize
    if sample_major:
        grid = (b, n_exp)
        be = lambda i, e: (i, e)
        nxt = pl.BlockSpec((1, 1, cap), lambda i, e: (i * n_exp + jnp.minimum(e + 1, n_exp - 1), 0, 0),
                           memory_space=pltpu.SMEM)
        h_spec = pl.BlockSpec((1, n, d), lambda i, e: (i, 0, 0), pipeline_mode=pl.Buffered(1))
    else:
        grid = (n_exp, b)
        be = lambda e, i: (i, e)
        nxt = pl.BlockSpec((1, 1, cap), lambda e, i: (i * n_exp + e, 0, 0), memory_space=pltpu.SMEM)
        h_spec = pl.BlockSpec((1, n, d), lambda e, i: (i, 0, 0))
    y = pl.pallas_call(
        functools.partial(_moe_kernel, cap=cap, prefetch=sample_major),
        out_shape=jax.ShapeDtypeStruct((b, n_exp, cap, d), F32),
        grid=grid,
        in_specs=[pl.BlockSpec((1, 1, cap), lambda *g: (be(*g)[0] * n_exp + be(*g)[1], 0, 0),
                               memory_space=pltpu.SMEM),
                  nxt, h_spec,
                  pl.BlockSpec((1, d, ff), lambda *g: (be(*g)[1], 0, 0)),
                  pl.BlockSpec((1, d, ff), lambda *g: (be(*g)[1], 0, 0)),
                  pl.BlockSpec((1, ff, d), lambda *g: (be(*g)[1], 0, 0))],
        out_specs=pl.BlockSpec((1, 1, cap, d), lambda *g: (*be(*g), 0, 0)),
        scratch_shapes=[pltpu.VMEM((2, cap, d), F32)],
        compiler_params=_params(("parallel", "arbitrary")), name="expert_ffn",
    )(idx, idx, h, wg, wu, wd)
    return pl.pallas_call(
        functools.partial(_combine_kernel, cap=cap),
        out_shape=jax.ShapeDtypeStruct((b, n, d), F32),
        grid=(b, n_exp),
        in_specs=[smem(cap), smem(n), pl.BlockSpec((1, 1, cap, d), lambda i, e: (i, e, 0, 0))],
        out_specs=pl.BlockSpec((1, n, d), lambda i, e: (i, 0, 0)),
        compiler_params=_params(("parallel", "arbitrary")), name="expert_combine",
    )(idx, gate, y)


def _residual_kernel(x_ref, f_ref, g_ref, gate_ref, o_ref):
    o_ref[0] = x_ref[0] + gate_ref[0] * _rms(f_ref[0], g_ref[...])


def _gated_residual(x, f, g, gate):
    b, n, d = x.shape
    tm = min(512, n)
    bm = gate.shape[0]
    mod_map = (lambda i, j: (i, 0, 0)) if bm > 1 else (lambda i, j: (0, 0, 0))
    tile = pl.BlockSpec((1, tm, d), lambda i, j: (i, j, 0))
    return pl.pallas_call(
        _residual_kernel, out_shape=jax.ShapeDtypeStruct((b, n, d), F32),
        grid=(b, n // tm),
        in_specs=[tile, tile, pl.BlockSpec((1, d), lambda i, j: (0, 0)), pl.BlockSpec((1, 1, d), mod_map)],
        out_specs=tile,
        compiler_params=_params(("parallel", "parallel")), name="gated_residual",
    )(x, f, g.reshape(1, d), gate)


def _rope_tables(n):
    t = jnp.arange(n, dtype=I32)
    row = (t // GRID_W).astype(F32)
    col = (t % GRID_W).astype(F32)
    n_freq = HEAD_DIM // 4
    inv_freq = ROPE_THETA ** (-jnp.arange(n_freq, dtype=F32) / n_freq)
    ang = jnp.concatenate([row[:, None] * inv_freq, col[:, None] * inv_freq], axis=-1)
    cos, sin = jnp.cos(ang), jnp.sin(ang)
    cos_t = jnp.tile(jnp.concatenate([cos, cos], axis=-1), (1, LANES // HEAD_DIM))
    sin_t = jnp.tile(jnp.concatenate([-sin, sin], axis=-1), (1, LANES // HEAD_DIM))
    return cos_t, sin_t


def _deinterleave_heads(w):
    d, width = w.shape
    w = w.reshape(d, width // HEAD_DIM, HEAD_DIM // 2, 2)
    return jnp.concatenate([w[..., 0], w[..., 1]], axis=-1).reshape(d, width)


def _dup_heads(w):
    d, width = w.shape
    w = w.reshape(d, width // HEAD_DIM, 1, HEAD_DIM)
    return jnp.concatenate([w, w], axis=2).reshape(d, 2 * width)


def _mod_split(mod_rows):
    sh1, sc1, gt1, sh2, sc2, gt2 = jnp.split(mod_rows[:, None, :], 6, axis=-1)
    return sh1, 1.0 + sc1, gt1, sh2, 1.0 + sc2, gt2


def _moe_block(x_mid, h2, aff_t, g_post_ffn, gt2, wg, wu, wd):
    b, n, d = h2.shape
    n_exp, _, ff = wg.shape
    idx, gate = _route(aff_t)
    if b > 1 and n * d * h2.dtype.itemsize < 3 * d * ff * wg.dtype.itemsize:
        cap = idx.shape[2]
        idx = idx.reshape(b, n_exp, cap) + (jnp.arange(b, dtype=I32) * n)[:, None, None]
        idx = idx.transpose(1, 0, 2).reshape(n_exp, 1, b * cap)
        gate = gate.reshape(b, n_exp, n).transpose(1, 0, 2).reshape(n_exp, 1, b * n)
        ffn = _expert_ffn(h2.reshape(1, b * n, d), idx, gate, wg, wu, wd).reshape(b, n, d)
    else:
        ffn = _expert_ffn(h2, idx, gate, wg, wu, wd)
    return _gated_residual(x_mid, ffn, g_post_ffn, gt2)


def kernel(x, c, ctx, c_ctx, l0_w_mod, l0_b_mod, l0_g_pre_mix, l0_g_post_mix, l0_g_pre_ffn, l0_g_post_ffn, l0_w_in, l0_w_out, l0_lam_q1, l0_lam_k1, l0_lam_q2, l0_lam_k2, l0_subln_g, l0_rpb, l0_w_router, l0_w_gate, l0_w_up, l0_w_down, l1_w_mod, l1_b_mod, l1_g_pre_mix, l1_g_post_mix, l1_g_pre_ffn, l1_g_post_ffn, l1_w_in, l1_w_out, l1_sinks, l1_conv_w, l1_conv_b, l1_a_log_f, l1_a_log_b, l1_dt_bias_f, l1_dt_bias_b, l1_d_skip, l1_gnorm_g, l1_w_router, l1_w_gate, l1_w_up, l1_w_down):
    b, n, d = x.shape
    m = ctx.shape[1]
    half = d // 2
    scale = HEAD_DIM ** -0.5 * math.log2(math.e)
    cos_t, sin_t = _rope_tables(n)
    cond = jnp.zeros((16, d), F32).at[:b].set(c).at[b].set(c_ctx)

    mod = _modulation(cond, l0_w_mod, l0_b_mod)
    sh1, sc1, gt1, sh2, sc2, gt2 = _mod_split(mod[:b])
    csh1, csc1, cgt1, csh2, csc2, cgt2 = _mod_split(mod[b:b + 1])

    a_qk = half
    wq, wk, rest = l0_w_in[:, :a_qk], l0_w_in[:, a_qk:2 * a_qk], l0_w_in[:, 2 * a_qk:]
    qb_cols = slice(half, 2 * half)
    rest = rest.at[:, qb_cols].multiply(scale)
    w_in0 = jnp.concatenate([_deinterleave_heads(wq) * scale, _deinterleave_heads(wk), rest], axis=1).astype(BF16)
    segs0 = [(half, BF16)] * 6
    qa, ka, va, qb, kb, vb = _project(x, l0_g_pre_mix, sc1, sh1, w_in0, segs0, rope=(2 * a_qk, cos_t, sin_t))
    qa_c, ka_c, va_c, qb_c, kb_c, vb_c = _project(ctx, l0_g_pre_mix, csc1, csh1, w_in0, segs0)

    lam_init = 0.8 - 0.6 * math.exp(-0.3 * 0)
    lam = (jnp.exp(jnp.sum(l0_lam_q1.astype(F32) * l0_lam_k1.astype(F32)))
           - jnp.exp(jnp.sum(l0_lam_q2.astype(F32) * l0_lam_k2.astype(F32))) + lam_init)
    o_a = _diff_attention(qa, [(ka, va), (ka_c, va_c)], lam, l0_subln_g, 1.0 - lam_init)
    o_b = _neighbourhood_attention(qb, kb, vb, kb_c, vb_c, l0_rpb)
    oc_a = _diff_attention(qa_c, [(ka_c, va_c)], lam, l0_subln_g, 1.0 - lam_init)
    oc_b = _context_attention(qb_c, kb_c, vb_c)

    w_out0 = l0_w_out.astype(BF16)
    wg0, wu0, wd0 = l0_w_gate.astype(BF16), l0_w_up.astype(BF16), l0_w_down.astype(BF16)
    x_mid, h2, aff_t = _out_project(o_a, o_b, w_out0, x, l0_g_post_mix, gt1, l0_g_pre_ffn, sc2, sh2, l0_w_router)
    x1 = _moe_block(x_mid, h2, aff_t, l0_g_post_ffn, gt2, wg0, wu0, wd0)
    c_mid, hc2, aff_c = _out_project(oc_a, oc_b, w_out0, ctx, l0_g_post_mix, cgt1, l0_g_pre_ffn, csc2, csh2,
                                     l0_w_router)
    ctx1 = _moe_block(c_mid, hc2, aff_c, l0_g_post_ffn, cgt2, wg0, wu0, wd0)

    mod = _modulation(cond, l1_w_mod, l1_b_mod)
    sh1, sc1, gt1, sh2, sc2, gt2 = _mod_split(mod[:b])
    csh1, csc1, _, _, _, _ = _mod_split(mod[b:b + 1])

    c_q = half
    d_inner = half
    d_heads = l1_a_log_f.shape[0]
    d_xbc = l1_conv_w.shape[2]
    c_kv = (l1_w_in.shape[1] - c_q - d_xbc - d_inner - 2 * d_heads) // 2
    n_groups = 2
    d_state = (d_xbc - d_inner) // (2 * n_groups)
    wq = l1_w_in[:, :c_q]
    wk = l1_w_in[:, c_q:c_q + c_kv]
    wv = l1_w_in[:, c_q + c_kv:c_q + 2 * c_kv]
    o_x = c_q + 2 * c_kv
    w_x = l1_w_in[:, o_x:o_x + d_xbc + d_inner]
    w_dt = l1_w_in[:, o_x + d_xbc + d_inner:]
    w_kv = jnp.concatenate([_dup_heads(_deinterleave_heads(wk)), _dup_heads(wv), w_x, w_dt], axis=1)
    w_lat = jnp.concatenate([_deinterleave_heads(wq) * scale, w_kv], axis=1).astype(BF16)
    w_ctx = w_kv.astype(BF16)
    w_dt_t = w_dt.T.astype(BF16)
    segs_lat = [(c_q, BF16), (2 * c_kv, BF16), (2 * c_kv, BF16), (d_xbc, F32), (d_inner, F32), (2 * d_heads, F32)]
    q, k2, v2, xbc, z, dt, dtt = _project(x1, l1_g_pre_mix, sc1, sh1, w_lat, segs_lat,
                                          rope=(c_q + 2 * c_kv, cos_t, sin_t), wt=w_dt_t)
    k2_c, v2_c, xbc_c, _, dt_c, dtt_c = _project(ctx1, l1_g_pre_mix, csc1, csh1, w_ctx, segs_lat[1:], wt=w_dt_t)

    o_c = _window_attention(q, k2, v2, k2_c, v2_c, l1_sinks.astype(F32) * math.log2(math.e))

    a_f = -jnp.exp(l1_a_log_f.astype(F32))
    a_b = -jnp.exp(l1_a_log_b.astype(F32))
    u = _conv_silu(xbc, l1_conv_w, l1_conv_b)
    u_c = _conv_silu(xbc_c, l1_conv_w, l1_conv_b)
    ssd = functools.partial(_ssd_scan, d_inner=d_inner, d_state=d_state, n_groups=n_groups)
    zero_state = jnp.zeros((b, d_heads // 2, d_state, LANES), F32)
    _, sc_f = ssd(u_c, dt_c[..., :d_heads], dtt_c[:, :d_heads], a_f, l1_dt_bias_f, zero_state, reverse=False)
    _, sc_b = ssd(u_c, dt_c[..., d_heads:], dtt_c[:, d_heads:], a_b, l1_dt_bias_b, zero_state, reverse=True)
    y_b, _ = ssd(u, dt[..., d_heads:], dtt[:, d_heads:], a_b, l1_dt_bias_b, sc_b, reverse=True)
    o_d, _ = ssd(u, dt[..., :d_heads], dtt[:, :d_heads], a_f, l1_dt_bias_f, sc_f, reverse=False,
                 final=(z, y_b, l1_d_skip, l1_gnorm_g))

    x_mid, h2, aff_t = _out_project(o_c, o_d, l1_w_out.astype(BF16), x1, l1_g_post_mix, gt1, l1_g_pre_ffn, sc2, sh2,
                                    l1_w_router)
    return _moe_block(x_mid, h2, aff_t, l1_g_post_ffn, gt2, l1_w_gate.astype(BF16), l1_w_up.astype(BF16),
                      l1_w_down.astype(BF16))
```

```python
import functools
import math

import jax
import jax.numpy as jnp
from jax import lax
from jax.experimental import pallas as pl
from jax.experimental.pallas import tpu as pltpu

F32 = jnp.float32
BF16 = jnp.bfloat16
I32 = jnp.int32

HEAD_DIM = 64
LANES = 128
SUBLANES = 8
GRID_W = 64
ROPE_THETA = 10000.0
NORM_EPS = 1e-6
NA_ROWS = 8
NA_COLS = 16
WINDOW = 128
Q_BLOCK = 128
SSD_CHUNK = 128
D_CONV = 5
CAPACITY_FACTOR = 2
NEG_BIG = -1e30
VMEM_LIMIT = 56 * 1024 * 1024
HIGHEST = lax.Precision.HIGHEST
NT_DIMS = (((1,), (1,)), ((), ()))


def _params(sem, vmem=VMEM_LIMIT):
    return pltpu.CompilerParams(dimension_semantics=sem, vmem_limit_bytes=vmem)


def _rms(u, g):
    return u * lax.rsqrt(jnp.mean(u * u, axis=-1, keepdims=True) + NORM_EPS) * g


def _silu(u):
    return u * (1.0 / (1.0 + jnp.exp(-u)))


def _softplus(u):
    return jnp.maximum(u, 0.0) + jnp.log(1.0 + jnp.exp(-jnp.abs(u)))


def _dot(a, b):
    return jnp.dot(a, b, preferred_element_type=F32)


def _dot_nt(a, b):
    return lax.dot_general(a, b, NT_DIMS, preferred_element_type=F32)


def _half_masks(shape):
    lane = lax.broadcasted_iota(I32, shape, 1)
    return (lane % LANES) < HEAD_DIM


def _mod_kernel(c_ref, w_ref, b_ref, o_ref):
    s = _silu(c_ref[...]).astype(BF16)
    o_ref[...] = _dot(s, w_ref[...].astype(BF16)) + b_ref[...]


def _modulation(cond, w_mod, b_mod):
    rows, d = cond.shape
    n_out = w_mod.shape[1]
    tn = 1024
    return pl.pallas_call(
        _mod_kernel,
        out_shape=jax.ShapeDtypeStruct((rows, n_out), F32),
        grid=(n_out // tn,),
        in_specs=[pl.BlockSpec((rows, d), lambda j: (0, 0)),
                  pl.BlockSpec((d, tn), lambda j: (0, j)),
                  pl.BlockSpec((1, tn), lambda j: (0, j))],
        out_specs=pl.BlockSpec((rows, tn), lambda j: (0, j)),
        compiler_params=_params(("parallel",)),
        name="modulation",
    )(cond, w_mod, b_mod.reshape(1, n_out))


def _rope_block(u, cos, sin_signed, first_half):
    width = u.shape[1]
    fwd = pltpu.roll(u, HEAD_DIM // 2, axis=1)
    bwd = pltpu.roll(u, width - HEAD_DIM // 2, axis=1)
    return u * cos + jnp.where(first_half, bwd, fwd) * sin_signed


def _proj_kernel(*refs, seg_widths, rope_cols, has_t, col_chunk):
    x_ref, g_ref, sc_ref, sh_ref, w_ref = refs[:5]
    pos = 5
    if rope_cols:
        cos_ref, sin_ref = refs[pos:pos + 2]
        pos += 2
    if has_t:
        wt_ref = refs[pos]
        pos += 1
    out_refs = refs[pos:]
    h = (_rms(x_ref[0], g_ref[...]) * sc_ref[0] + sh_ref[0]).astype(BF16)
    if rope_cols:
        cos = cos_ref[...]
        sin_signed = sin_ref[...]
        first_half = (lax.broadcasted_iota(I32, cos.shape, 1) % HEAD_DIM) < HEAD_DIM // 2
    col = 0
    for o_ref, width in zip(out_refs, seg_widths):
        for c0 in range(0, width, col_chunk):
            cw = min(col_chunk, width - c0)
            acc = _dot(h, w_ref[:, col + c0:col + c0 + cw])
            if col + c0 < rope_cols:
                acc = jnp.concatenate(
                    [_rope_block(acc[:, k:k + LANES], cos, sin_signed, first_half)
                     for k in range(0, cw, LANES)], axis=1)
            o_ref[0, :, c0:c0 + cw] = acc.astype(o_ref.dtype)
        col += width
    if has_t:
        out_refs[-1][0] = _dot_nt(wt_ref[...], h)


def _project(x, g, scale1p, shift, w, segs, rope=None, wt=None):
    b, n, d = x.shape
    tm = min(512, n)
    bm = scale1p.shape[0]
    mod_map = (lambda i, j: (i, 0, 0)) if bm > 1 else (lambda i, j: (0, 0, 0))
    in_specs = [pl.BlockSpec((1, tm, d), lambda i, j: (i, j, 0)),
                pl.BlockSpec((1, d), lambda i, j: (0, 0)),
                pl.BlockSpec((1, 1, d), mod_map),
                pl.BlockSpec((1, 1, d), mod_map),
                pl.BlockSpec(w.shape, lambda i, j: (0, 0))]
    args = [x, g.reshape(1, d), scale1p, shift, w]
    rope_cols = 0
    if rope is not None:
        rope_cols, cos, sin_signed = rope
        in_specs += [pl.BlockSpec((tm, LANES), lambda i, j: (j, 0))] * 2
        args += [cos, sin_signed]
    if wt is not None:
        in_specs.append(pl.BlockSpec(wt.shape, lambda i, j: (0, 0)))
        args.append(wt)
    out_shape = [jax.ShapeDtypeStruct((b, n, wd), dt) for wd, dt in segs]
    out_specs = [pl.BlockSpec((1, tm, wd), lambda i, j: (i, j, 0)) for wd, _ in segs]
    if wt is not None:
        out_shape.append(jax.ShapeDtypeStruct((b, wt.shape[0], n), F32))
        out_specs.append(pl.BlockSpec((1, wt.shape[0], tm), lambda i, j: (i, 0, j)))
    kern = functools.partial(_proj_kernel, seg_widths=tuple(wd for wd, _ in segs),
                             rope_cols=rope_cols, has_t=wt is not None, col_chunk=512)
    return pl.pallas_call(
        kern, out_shape=out_shape, grid=(b, n // tm), in_specs=in_specs, out_specs=out_specs,
        compiler_params=_params(("parallel", "parallel")), name="in_proj",
    )(*args)


def _diff_kernel(*refs, seg_lens, tk, out_scale, heads):
    n_seg = len(seg_lens)
    m_all, acc_all, s_all = refs[4 + 2 * n_seg:]
    for h in range(heads):
        _diff_head(refs[:4 + 2 * n_seg], m_all.at[h], acc_all.at[h], s_all.at[2 * (h % 2)],
                   s_all.at[2 * (h % 2) + 1], slice(h * LANES, (h + 1) * LANES), seg_lens, tk, out_scale)


def _diff_head(io_refs, m_ref, acc_ref, s0_ref, s1_ref, cols, seg_lens, tk, out_scale):
    lam_ref, q_ref, g_ref = io_refs[:3]
    n_seg = len(seg_lens)
    kv_refs = [r.at[:, :, cols] for r in io_refs[3:3 + 2 * n_seg]]
    o_ref = io_refs[3 + 2 * n_seg]
    qf = q_ref[0, :, cols].astype(F32)
    first = _half_masks(qf.shape)
    qs = (jnp.where(first, qf, 0.0).astype(BF16), jnp.where(first, 0.0, qf).astype(BF16))
    m_ref[...] = jnp.full(m_ref.shape, NEG_BIG, F32)
    acc_ref[...] = jnp.zeros(acc_ref.shape, F32)

    def scores(kc):
        return [_dot_nt(qs[t], kc) for t in range(2)]

    def absorb(sc, vc):
        v_ones = jnp.concatenate([vc, jnp.ones(vc.shape, BF16)], axis=1)
        for t in range(2):
            m_old = m_ref[t]
            m_new = jnp.maximum(m_old, jnp.max(sc[t], axis=1, keepdims=True))
            p = jnp.exp2(sc[t] - m_new).astype(BF16)
            acc_ref[t] = jnp.exp2(m_old - m_new) * acc_ref[t] + _dot(p, v_ones)
            m_ref[t] = m_new

    def put(buf, sc):
        buf[0] = sc[0]
        buf[1] = sc[1]

    main = max(range(n_seg), key=lambda i: seg_lens[i])
    n_chunks = seg_lens[main] // tk if seg_lens[main] >= 2 * tk else 1
    if n_chunks > 1:
        k_ref, v_ref = kv_refs[2 * main], kv_refs[2 * main + 1]
        rows = lambda ref, c: ref[0, pl.ds(pl.multiple_of(c * tk, tk), tk), :]
    singles = [s_i for s_i in range(n_seg) if s_i != main or n_chunks == 1]
    single_scores = [scores(kv_refs[2 * s_i][0]) for s_i in singles]
    if n_chunks > 1:
        put(s0_ref, scores(rows(k_ref, 0)))
    for s_i, sc in zip(singles, single_scores):
        absorb(sc, kv_refs[2 * s_i + 1][0])
    if n_chunks > 1:
        def pair(c, last):
            put(s1_ref, scores(rows(k_ref, c + 1)))
            absorb([s0_ref[0], s0_ref[1]], rows(v_ref, c))
            if not last:
                put(s0_ref, scores(rows(k_ref, c + 2)))
            absorb([s1_ref[0], s1_ref[1]], rows(v_ref, c + 1))

        def body(j, carry):
            pair(2 * j, False)
            return carry

        lax.fori_loop(0, n_chunks // 2 - 1, body, 0)
        pair(n_chunks - 2, True)
    ratio0 = acc_ref[0, :, :LANES] / acc_ref[0, :, LANES:]
    ratio1 = acc_ref[1, :, :LANES] / acc_ref[1, :, LANES:]
    o = ratio0 - lam_ref[0] * ratio1
    o_ref[0, :, cols] = (_rms(o, g_ref[...]) * out_scale).astype(o_ref.dtype)


def _diff_attention(q, kv_segs, lam, subln_g, out_scale):
    b, nq, width = q.shape
    heads = width // LANES
    tq = min(512, nq)
    in_specs = [pl.BlockSpec(memory_space=pltpu.SMEM),
                pl.BlockSpec((1, tq, width), lambda i, j: (i, j, 0)),
                pl.BlockSpec((1, LANES), lambda i, j: (0, 0))]
    args = [lam.reshape(1).astype(F32), q, subln_g.reshape(1, LANES).astype(F32)]
    seg_lens = []
    for k, v in kv_segs:
        length = k.shape[1]
        seg_lens.append(length)
        in_specs += [pl.BlockSpec((1, length, width), lambda i, j: (i, 0, 0))] * 2
        args += [k, v]
    tk = 512
    longest = max(seg_lens)
    assert longest < 2 * tk or longest % (2 * tk) == 0, (longest, tk)
    kern = functools.partial(_diff_kernel, seg_lens=tuple(seg_lens), tk=tk, out_scale=out_scale, heads=heads)
    return pl.pallas_call(
        kern, out_shape=jax.ShapeDtypeStruct((b, nq, width), BF16),
        grid=(b, nq // tq), in_specs=in_specs,
        out_specs=pl.BlockSpec((1, tq, width), lambda i, j: (i, j, 0)),
        scratch_shapes=[pltpu.VMEM((heads, 2, tq, 1), F32), pltpu.VMEM((heads, 2, tq, 2 * LANES), F32),
                        pltpu.VMEM((4, 2, tq, tk), F32)],
        compiler_params=_params(("parallel", "parallel")), name="diff_attention",
    )(*args)


NA_BLOCK_ROWS = 4
NA_WIN_ROWS = NA_BLOCK_ROWS + NA_ROWS


def _na_window_start(blk, rows):
    return jnp.clip(blk * NA_BLOCK_ROWS - NA_ROWS // 2, 0, rows - NA_WIN_ROWS)


def _natten_kernel(q_ref, k_ref, v_ref, kc_ref, vc_ref, bias_ref, o_ref, *, rows):
    nq = NA_BLOCK_ROWS * GRID_W
    per_step = q_ref.shape[1] // nq
    for sb in range(per_step):
        _natten_block(q_ref, k_ref, v_ref, kc_ref, vc_ref, bias_ref, o_ref, pl.program_id(1) * per_step + sb,
                      slice(sb * nq, (sb + 1) * nq), rows)


def _natten_block(q_ref, k_ref, v_ref, kc_ref, vc_ref, bias_ref, o_ref, blk, qrows, rows):
    start = pl.multiple_of(_na_window_start(blk, rows) * GRID_W, GRID_W)
    n_keys = NA_WIN_ROWS * GRID_W
    nq = NA_BLOCK_ROWS * GRID_W
    case = jnp.where(blk == 0, 0, jnp.where(blk == rows // NA_BLOCK_ROWS - 1, 2, 1))
    q = q_ref[0, qrows, :].astype(F32)
    kwin = k_ref[0, pl.ds(start, n_keys), :]
    vwin = v_ref[0, pl.ds(start, n_keys), :]
    kc = kc_ref[0]
    vc = vc_ref[0]
    first = _half_masks((nq, LANES))
    for p in range(q.shape[1] // LANES):
        cols = slice(p * LANES, (p + 1) * LANES)
        qs = jnp.concatenate([jnp.where(first, q[:, cols], 0.0), jnp.where(first, 0.0, q[:, cols])],
                             axis=0).astype(BF16)
        s_nb = _dot_nt(qs, kwin[:, cols]) + jnp.concatenate([bias_ref[case, 2 * p], bias_ref[case, 2 * p + 1]],
                                                            axis=0)
        s_cx = _dot_nt(qs, kc[:, cols])
        m = jnp.maximum(jnp.max(s_nb, axis=1, keepdims=True), jnp.max(s_cx, axis=1, keepdims=True))
        p_nb = jnp.exp2(s_nb - m).astype(BF16)
        p_cx = jnp.exp2(s_cx - m).astype(BF16)
        v_nb = jnp.concatenate([vwin[:, cols], jnp.ones((n_keys, LANES), BF16)], axis=1)
        v_cx = jnp.concatenate([vc[:, cols], jnp.ones((vc.shape[0], LANES), BF16)], axis=1)
        acc = _dot(p_nb, v_nb) + _dot(p_cx, v_cx)
        ratio = acc[:, :LANES] / acc[:, LANES:]
        o_ref[0, qrows, cols] = jnp.where(first, ratio[:nq], ratio[nq:]).astype(o_ref.dtype)


def _na_bias_table(rpb, rows):
    heads = rpb.shape[0]
    w = jnp.arange(GRID_W)
    kc = jnp.arange(GRID_W)
    col_start = jnp.clip(w - NA_COLS // 2, 0, GRID_W - NA_COLS)
    valid = (kc[None, :] >= col_start[:, None]) & (kc[None, :] < col_start[:, None] + NA_COLS)
    pad = GRID_W - NA_COLS
    padded = jnp.pad(rpb.astype(F32) * math.log2(math.e), ((0, 0), (0, 0), (pad, pad)))
    cols = jnp.stack([padded[:, :, GRID_W - 1 - wi:2 * GRID_W - 1 - wi] for wi in range(GRID_W)], axis=2)
    cols = jnp.where(valid[None, None], cols, NEG_BIG)
    masked = jnp.full((heads, GRID_W, GRID_W), NEG_BIG, F32)
    cases = []
    for first_row in (0, NA_BLOCK_ROWS, rows - NA_BLOCK_ROWS):
        win0 = min(max(first_row - NA_ROWS // 2, 0), rows - NA_WIN_ROWS)
        per_q = []
        for qr in range(NA_BLOCK_ROWS):
            r = first_row + qr
            rs = min(max(r - NA_ROWS // 2, 0), rows - NA_ROWS)
            per_k = [cols[:, win0 + kr - r + NA_ROWS - 1] if rs <= win0 + kr < rs + NA_ROWS else masked
                     for kr in range(NA_WIN_ROWS)]
            per_q.append(jnp.stack(per_k, axis=2))
        cases.append(jnp.stack(per_q, axis=1))
    return jnp.stack(cases).reshape(3, heads, NA_BLOCK_ROWS * GRID_W, NA_WIN_ROWS * GRID_W)


def _neighbourhood_attention(q, k, v, kc, vc, rpb):
    b, n, width = q.shape
    m = kc.shape[1]
    rows = n // GRID_W
    n_blk = rows // NA_BLOCK_ROWS
    assert rows % NA_BLOCK_ROWS == 0 and rows >= NA_WIN_ROWS + NA_BLOCK_ROWS
    bias = _na_bias_table(rpb, rows)
    heads = rpb.shape[0]
    step_q = NA_BLOCK_ROWS * GRID_W * math.gcd(n_blk, 4)
    kern = functools.partial(_natten_kernel, rows=rows)
    return pl.pallas_call(
        kern, out_shape=jax.ShapeDtypeStruct((b, n, width), BF16),
        grid=(b, n // step_q),
        in_specs=[pl.BlockSpec((1, step_q, width), lambda i, r: (i, r, 0)),
                  pl.BlockSpec((1, n, width), lambda i, r: (i, 0, 0)),
                  pl.BlockSpec((1, n, width), lambda i, r: (i, 0, 0)),
                  pl.BlockSpec((1, m, width), lambda i, r: (i, 0, 0)),
                  pl.BlockSpec((1, m, width), lambda i, r: (i, 0, 0)),
                  pl.BlockSpec(bias.shape, lambda i, r: (0, 0, 0, 0), pipeline_mode=pl.Buffered(1))],
        out_specs=pl.BlockSpec((1, step_q, width), lambda i, r: (i, r, 0)),
        compiler_params=_params(("parallel", "parallel")), name="neighbourhood_attention",
    )(q, k, v, kc, vc, bias)


def _ctx_attn_kernel(q_ref, k_ref, v_ref, o_ref):
    q = q_ref[0].astype(F32)
    k = k_ref[0]
    v = v_ref[0]
    first = _half_masks((q.shape[0], LANES))
    for p in range(q.shape[1] // LANES):
        cols = slice(p * LANES, (p + 1) * LANES)
        outs = []
        for sub in range(2):
            qm = jnp.where(first if sub == 0 else ~first, q[:, cols], 0.0).astype(BF16)
            s = _dot_nt(qm, k[:, cols])
            e = jnp.exp2(s - jnp.max(s, axis=1, keepdims=True))
            outs.append(_dot(e.astype(BF16), v[:, cols]) / jnp.sum(e, axis=1, keepdims=True))
        o_ref[0, :, cols] = jnp.where(first, outs[0], outs[1]).astype(o_ref.dtype)


def _context_attention(q, k, v):
    b, m, width = q.shape
    spec = pl.BlockSpec((1, m, width), lambda i: (i, 0, 0))
    return pl.pallas_call(
        _ctx_attn_kernel, out_shape=jax.ShapeDtypeStruct((b, m, width), BF16),
        grid=(b,), in_specs=[spec, spec, spec], out_specs=spec,
        compiler_params=_params(("parallel",)), name="context_attention",
    )(q, k, v)


def _swa_kernel(sink_ref, q_ref, k_ref, v_ref, kc_ref, vc_ref, o_ref, *, n_blocks):
    for sb in range(q_ref.shape[1] // Q_BLOCK):
        _swa_block(sink_ref, q_ref, k_ref, v_ref, kc_ref, vc_ref, o_ref, pl.program_id(1) * (q_ref.shape[1] // Q_BLOCK) + sb,
                   slice(sb * Q_BLOCK, (sb + 1) * Q_BLOCK), n_blocks)


def _swa_block(sink_ref, q_ref, k_ref, v_ref, kc_ref, vc_ref, o_ref, blk, qrows, n_blocks):
    span = 3 * Q_BLOCK
    start = pl.multiple_of(jnp.clip(blk - 1, 0, n_blocks - 3) * Q_BLOCK, Q_BLOCK)
    q = q_ref[0, qrows, :].astype(F32)
    kwin = k_ref[0, pl.ds(start, span), :]
    vwin = v_ref[0, pl.ds(start, span), :]
    kc = kc_ref[0]
    vc = vc_ref[0]
    first = _half_masks((Q_BLOCK, LANES))
    pairs = q.shape[1] // LANES
    n_groups = k_ref.shape[2] // LANES
    pairs_per_group = pairs // n_groups
    stack = 2 * pairs_per_group
    row = lax.broadcasted_iota(I32, (stack * Q_BLOCK, span), 0)
    qpos = blk * Q_BLOCK + (row & (Q_BLOCK - 1))
    kpos = start + lax.broadcasted_iota(I32, (stack * Q_BLOCK, span), 1)
    valid = jnp.abs(qpos - kpos) <= WINDOW
    head_of_row = lax.broadcasted_iota(I32, (stack * Q_BLOCK, 1), 0) >> (Q_BLOCK.bit_length() - 1)
    for g in range(n_groups):
        gcols = slice(g * LANES, (g + 1) * LANES)
        parts = []
        for p in range(g * pairs_per_group, (g + 1) * pairs_per_group):
            q_pair = q[:, p * LANES:(p + 1) * LANES]
            parts += [jnp.where(first, q_pair, 0.0), jnp.where(first, 0.0, q_pair)]
        qs = jnp.concatenate(parts, axis=0).astype(BF16)
        sink = jnp.zeros((stack * Q_BLOCK, 1), F32)
        for i in range(stack):
            sink = jnp.where(head_of_row == i, sink_ref[g * stack + i], sink)
        s_loc = jnp.where(valid, _dot_nt(qs, kwin[:, gcols]), NEG_BIG)
        s_cx = _dot_nt(qs, kc[:, gcols])
        m = jnp.maximum(jnp.maximum(jnp.max(s_loc, axis=1, keepdims=True),
                                    jnp.max(s_cx, axis=1, keepdims=True)), sink)
        p_loc = jnp.exp2(s_loc - m).astype(BF16)
        p_cx = jnp.exp2(s_cx - m).astype(BF16)
        v_loc = jnp.concatenate([vwin[:, gcols], jnp.ones((span, LANES), BF16)], axis=1)
        v_cx = jnp.concatenate([vc[:, gcols], jnp.ones((vc.shape[0], LANES), BF16)], axis=1)
        acc = _dot(p_loc, v_loc) + _dot(p_cx, v_cx)
        ratio = acc[:, :LANES] / (acc[:, LANES:] + jnp.exp2(sink - m))
        for i in range(pairs_per_group):
            p = g * pairs_per_group + i
            lo = ratio[2 * i * Q_BLOCK:(2 * i + 1) * Q_BLOCK]
            hi = ratio[(2 * i + 1) * Q_BLOCK:(2 * i + 2) * Q_BLOCK]
            o_ref[0, qrows, p * LANES:(p + 1) * LANES] = jnp.where(first, lo, hi).astype(o_ref.dtype)


def _window_attention(q, k2, v2, kc2, vc2, sinks):
    b, n, width = q.shape
    m = kc2.shape[1]
    kvw = k2.shape[2]
    n_blocks = n // Q_BLOCK
    step_rows = Q_BLOCK * math.gcd(n_blocks, 8)
    kern = functools.partial(_swa_kernel, n_blocks=n_blocks)
    return pl.pallas_call(
        kern, out_shape=jax.ShapeDtypeStruct((b, n, width), BF16),
        grid=(b, n // step_rows),
        in_specs=[pl.BlockSpec(memory_space=pltpu.SMEM),
                  pl.BlockSpec((1, step_rows, width), lambda i, j: (i, j, 0)),
                  pl.BlockSpec((1, n, kvw), lambda i, j: (i, 0, 0)),
                  pl.BlockSpec((1, n, kvw), lambda i, j: (i, 0, 0)),
                  pl.BlockSpec((1, m, kvw), lambda i, j: (i, 0, 0)),
                  pl.BlockSpec((1, m, kvw), lambda i, j: (i, 0, 0))],
        out_specs=pl.BlockSpec((1, step_rows, width), lambda i, j: (i, j, 0)),
        compiler_params=_params(("parallel", "parallel")), name="window_attention",
    )(sinks.astype(F32), q, k2, v2, kc2, vc2)


def _conv_kernel(x_ref, prev_ref, next_ref, w_ref, b_ref, o_ref):
    i = pl.program_id(1)
    tm = x_ref.shape[1]
    keep_prev = jnp.where(i > 0, 1.0, 0.0)
    keep_next = jnp.where(i < pl.num_programs(1) - 1, 1.0, 0.0)
    xe = jnp.concatenate([prev_ref[0] * keep_prev, x_ref[0], next_ref[0] * keep_next], axis=0)
    acc = jnp.zeros((tm, x_ref.shape[2]), F32) + b_ref[...]
    for k in range(D_CONV):
        off = 8 - D_CONV // 2 + k
        acc = acc + w_ref[k:k + 1, :] * xe[off:off + tm, :]
    o_ref[0] = _silu(acc)


def _conv_silu(x, conv_w, conv_b):
    b, length, ch = x.shape
    tm = min(512, length)
    per = tm // 8
    last = length // 8 - 1
    return pl.pallas_call(
        _conv_kernel, out_shape=jax.ShapeDtypeStruct((b, length, ch), F32),
        grid=(b, length // tm),
        in_specs=[pl.BlockSpec((1, tm, ch), lambda i, j: (i, j, 0)),
                  pl.BlockSpec((1, 8, ch), lambda i, j: (i, jnp.maximum(j * per - 1, 0), 0)),
                  pl.BlockSpec((1, 8, ch), lambda i, j: (i, jnp.minimum((j + 1) * per, last), 0)),
                  pl.BlockSpec((D_CONV, ch), lambda i, j: (0, 0)),
                  pl.BlockSpec((1, ch), lambda i, j: (0, 0))],
        out_specs=pl.BlockSpec((1, tm, ch), lambda i, j: (i, j, 0)),
        compiler_params=_params(("parallel", "parallel")), name="conv_silu",
    )(x, x, x, conv_w.reshape(D_CONV, ch).astype(F32), conv_b.reshape(1, ch).astype(F32))


def _ssd_kernel(*refs, reverse, finalize, d_inner, d_state, n_groups):
    u_ref, dt_ref, dtt_ref, a_ref, at_ref, bias_ref, biast_ref, init_ref = refs[:8]
    pos = 8
    if finalize:
        z_ref, yb_ref, skip_ref, gn_ref = refs[pos:pos + 4]
        pos += 4
    y_ref, fin_ref, st_ref = refs[pos:pos + 3]
    c = pl.program_id(1)

    @pl.when(c == 0)
    def _():
        st_ref[...] = init_ref[0]

    q_len = SSD_CHUNK
    row = lax.broadcasted_iota(I32, (q_len, q_len), 0)
    col = lax.broadcasted_iota(I32, (q_len, q_len), 1)
    lower = row >= col
    upper = row <= col
    causal = upper if reverse else lower
    tri_col = causal.astype(F32)
    tri_row = (lower if reverse else upper).astype(F32)
    first = _half_masks((q_len, LANES))
    heads_per_group = (d_inner // HEAD_DIM) // n_groups
    pairs_per_group = heads_per_group // 2
    n_sub = u_ref.shape[1] // q_len
    for sc in (reversed(range(n_sub)) if reverse else range(n_sub)):
        _ssd_chunk(slice(sc * q_len, (sc + 1) * q_len), refs, causal, tri_col, tri_row, first, pairs_per_group,
                   finalize, d_inner, d_state, n_groups)

    @pl.when(c == pl.num_programs(1) - 1)
    def _():
        fin_ref[0] = st_ref[...]


def _ssd_chunk(rows, refs, causal, tri_col, tri_row, first, pairs_per_group, finalize, d_inner, d_state, n_groups):
    u_ref, dt_ref, dtt_ref, a_ref, at_ref, bias_ref, biast_ref, _ = refs[:8]
    pos = 8
    if finalize:
        z_ref, yb_ref, skip_ref, gn_ref = refs[pos:pos + 4]
        pos += 4
    y_ref, _, st_ref = refs[pos:pos + 3]
    u = u_ref[0, rows, :]
    xs = u[:, :d_inner]
    dt = _softplus(dt_ref[0, rows, :] + bias_ref[...])
    dtt = _softplus(dtt_ref[0, :, rows] + biast_ref[...])
    da = dt * a_ref[...]
    dat = dtt * at_ref[...]
    q_col = jnp.dot(tri_col, da, precision=HIGHEST, preferred_element_type=F32)
    q_row = jnp.dot(dat, tri_row, precision=HIGHEST, preferred_element_type=F32)
    total = jnp.sum(da, axis=0, keepdims=True)
    y_pairs = []
    for g in range(n_groups):
        b_g = u[:, d_inner + g * d_state:d_inner + (g + 1) * d_state]
        c_g = u[:, d_inner + (n_groups + g) * d_state:d_inner + (n_groups + g + 1) * d_state]
        c_bf = c_g.astype(BF16)
        cb = _dot_nt(c_bf, b_g.astype(BF16))
        b_t = b_g.T.astype(BF16)
        for pp in range(pairs_per_group):
            p = g * pairs_per_group + pp
            h0, h1 = 2 * p, 2 * p + 1
            x_pair = xs[:, p * LANES:(p + 1) * LANES]
            dt_pair = jnp.where(first, dt[:, h0:h0 + 1], dt[:, h1:h1 + 1])
            qc_pair = jnp.where(first, q_col[:, h0:h0 + 1], q_col[:, h1:h1 + 1])
            tot_pair = jnp.where(first[:1], total[:, h0:h0 + 1], total[:, h1:h1 + 1])
            xdt = x_pair * dt_pair
            xdt_bf = xdt.astype(BF16)
            diag = []
            for h in (h0, h1):
                diff = q_col[:, h:h + 1] - q_row[h:h + 1, :]
                lmat = jnp.exp(jnp.where(causal, diff, NEG_BIG))
                diag.append(_dot((cb * lmat).astype(BF16), xdt_bf))
            y_diag = jnp.where(first, diag[0], diag[1])
            state = st_ref[p]
            y_off = _dot(c_bf, state.astype(BF16)) * jnp.exp(qc_pair)
            y_pairs.append(y_diag + y_off)
            carry_in = (xdt * jnp.exp(tot_pair - qc_pair)).astype(BF16)
            st_ref[p] = jnp.exp(tot_pair) * state + _dot(b_t, carry_in)
    y = jnp.concatenate(y_pairs, axis=1)
    if finalize:
        y = y + yb_ref[0, rows, :] + skip_ref[...] * xs
        y = y * _silu(z_ref[0, rows, :])
        gw = d_inner // n_groups
        y = jnp.concatenate(
            [y[:, g * gw:(g + 1) * gw] * lax.rsqrt(
                jnp.mean(y[:, g * gw:(g + 1) * gw] ** 2, axis=1, keepdims=True) + NORM_EPS)
             for g in range(n_groups)], axis=1)
        y = y * gn_ref[...]
    y_ref[0, rows, :] = y.astype(y_ref.dtype)


def _ssd_scan(u, dt, dtt, a, dt_bias, init, *, reverse, d_inner, d_state, n_groups, final=None):
    b, length, ch = u.shape
    heads = dt.shape[2]
    step_rows = SSD_CHUNK * math.gcd(length // SSD_CHUNK, 4)
    nc = length // step_rows
    pairs = heads // 2
    cidx = (lambda j: nc - 1 - j) if reverse else (lambda j: j)
    small = lambda shape: pl.BlockSpec(shape, lambda i, j: (0,) * len(shape))
    in_specs = [pl.BlockSpec((1, step_rows, ch), lambda i, j: (i, cidx(j), 0)),
                pl.BlockSpec((1, step_rows, heads), lambda i, j: (i, cidx(j), 0)),
                pl.BlockSpec((1, heads, step_rows), lambda i, j: (i, 0, cidx(j))),
                small((1, heads)), small((heads, 1)), small((1, heads)), small((heads, 1)),
                pl.BlockSpec((1, pairs, d_state, LANES), lambda i, j: (i, 0, 0, 0))]
    args = [u, dt, dtt, a.reshape(1, heads), a.reshape(heads, 1),
            dt_bias.reshape(1, heads).astype(F32), dt_bias.reshape(heads, 1).astype(F32), init]
    out_dtype = F32
    if final is not None:
        z, yb, d_skip, gnorm_g = final
        in_specs += [pl.BlockSpec((1, step_rows, d_inner), lambda i, j: (i, cidx(j), 0)),
                     pl.BlockSpec((1, step_rows, d_inner), lambda i, j: (i, cidx(j), 0)),
                     small((1, d_inner)), small((1, d_inner))]
        args += [z, yb, jnp.repeat(d_skip.astype(F32), HEAD_DIM).reshape(1, d_inner),
                 gnorm_g.reshape(1, d_inner).astype(F32)]
        out_dtype = BF16
    kern = functools.partial(_ssd_kernel, reverse=reverse, finalize=final is not None,
                             d_inner=d_inner, d_state=d_state, n_groups=n_groups)
    return pl.pallas_call(
        kern,
        out_shape=[jax.ShapeDtypeStruct((b, length, d_inner), out_dtype),
                   jax.ShapeDtypeStruct((b, pairs, d_state, LANES), F32)],
        grid=(b, nc), in_specs=in_specs,
        out_specs=[pl.BlockSpec((1, step_rows, d_inner), lambda i, j: (i, cidx(j), 0)),
                   pl.BlockSpec((1, pairs, d_state, LANES), lambda i, j: (i, 0, 0, 0))],
        scratch_shapes=[pltpu.VMEM((pairs, d_state, LANES), F32)],
        compiler_params=_params(("parallel", "arbitrary")), name="ssd_scan",
    )(*args)


def _outproj_kernel(oa_ref, ob_ref, w_ref, x_ref, gpost_ref, gate_ref, gpre_ref, sc_ref, sh_ref, wrh_ref, wrl_ref,
                    xo_ref, h_ref, aff_ref, *, row_chunk):
    half = oa_ref.shape[2]
    for r0 in range(0, x_ref.shape[1], row_chunk):
        rows = slice(r0, r0 + row_chunk)
        mix = _dot(oa_ref[0, rows, :], w_ref[:half, :]) + _dot(ob_ref[0, rows, :], w_ref[half:, :])
        x_new = x_ref[0, rows, :] + gate_ref[0] * _rms(mix, gpost_ref[...])
        xo_ref[0, rows, :] = x_new
        h = _rms(x_new, gpre_ref[...]) * sc_ref[0] + sh_ref[0]
        h_ref[0, rows, :] = h
        h_hi = h.astype(BF16)
        h_lo = (h - h_hi.astype(F32)).astype(BF16)
        logits = _dot_nt(wrh_ref[...], h_hi) + _dot_nt(wrh_ref[...], h_lo) + _dot_nt(wrl_ref[...], h_hi)
        e = jnp.exp(logits - jnp.max(logits, axis=0, keepdims=True))
        aff_ref[0, :, rows] = e / jnp.sum(e, axis=0, keepdims=True)


def _out_project(oa, ob, w_out, x, g_post, gate, g_pre, scale1p, shift, w_router):
    b, n, d = x.shape
    half = oa.shape[2]
    n_exp = w_router.shape[1]
    tm = min(1024, n)
    bm = gate.shape[0]
    mod_map = (lambda i, j: (i, 0, 0)) if bm > 1 else (lambda i, j: (0, 0, 0))
    row = pl.BlockSpec((1, d), lambda i, j: (0, 0))
    mod = pl.BlockSpec((1, 1, d), mod_map)
    tile = pl.BlockSpec((1, tm, d), lambda i, j: (i, j, 0))
    act = pl.BlockSpec((1, tm, half), lambda i, j: (i, j, 0))
    wr_spec = pl.BlockSpec((n_exp, d), lambda i, j: (0, 0))
    wr = w_router.T.astype(F32)
    wr_hi = wr.astype(BF16)
    wr_lo = (wr - wr_hi.astype(F32)).astype(BF16)
    return pl.pallas_call(
        functools.partial(_outproj_kernel, row_chunk=min(512, tm)),
        out_shape=[jax.ShapeDtypeStruct((b, n, d), F32), jax.ShapeDtypeStruct((b, n, d), F32),
                   jax.ShapeDtypeStruct((b, n_exp, n), F32)],
        grid=(b, n // tm),
        in_specs=[act, act, pl.BlockSpec((2 * half, d), lambda i, j: (0, 0)), tile, row, mod, row, mod, mod,
                  wr_spec, wr_spec],
        out_specs=[tile, tile, pl.BlockSpec((1, n_exp, tm), lambda i, j: (i, 0, j))],
        compiler_params=_params(("parallel", "parallel")), name="out_proj_router",
    )(oa, ob, w_out, x, g_post.reshape(1, d), gate, g_pre.reshape(1, d), scale1p, shift, wr_hi, wr_lo)


def _topk_kernel(aff_ref, idx_ref, gate_ref, work_ref, key_ref, *, cap):
    n_exp, n = aff_ref.shape[1], aff_ref.shape[2]
    n_blk = n // LANES
    aff = aff_ref[0]

    def search(i, thr):
        cand = thr | jnp.left_shift(jnp.int32(1), 30 - i)
        cnt = jnp.sum(jnp.where(aff_ref[0] >= pltpu.bitcast(cand, F32), 1.0, 0.0), axis=1, keepdims=True)
        return jnp.where(cnt >= cap, cand, thr)

    thr = lax.fori_loop(0, 31, search, jnp.zeros((n_exp, 1), I32))
    thr_f = pltpu.bitcast(thr, F32)
    above = aff > thr_f
    tied = (aff >= thr_f) & ~above
    need = cap - jnp.sum(jnp.where(above, 1.0, 0.0), axis=1, keepdims=True)

    r = lax.broadcasted_iota(I32, (LANES, LANES), 0)
    c = lax.broadcasted_iota(I32, (LANES, LANES), 1)
    incl = (r <= c).astype(BF16)

    def exclusive_prefix(mask_f32):
        starts = [jnp.zeros((n_exp, 1), F32)]
        for j in range(n_blk):
            blk = mask_f32[:, j * LANES:(j + 1) * LANES]
            run = _dot(blk.astype(BF16), incl)
            work_ref[:, j * LANES:(j + 1) * LANES] = run - blk + starts[-1]
            starts.append(starts[-1] + jnp.sum(blk, axis=1, keepdims=True))
        return starts

    exclusive_prefix(jnp.where(tied, 1.0, 0.0))
    sel = above | (tied & (work_ref[...] < need))
    sel_f = jnp.where(sel, 1.0, 0.0)
    gate_ref[0] = jnp.where(sel, aff, 0.0)
    starts = exclusive_prefix(sel_f)
    key = jnp.where(sel, work_ref[...], -1.0)
    for j in range(n_blk):
        key_ref[j] = key[:, j * LANES:(j + 1) * LANES]

    lane = lax.broadcasted_iota(I32, (LANES, LANES), 1).astype(F32)
    slot = lax.broadcasted_iota(I32, (LANES, LANES), 0).astype(F32)

    for s0 in range(0, cap, LANES):
        width = min(LANES, cap - s0)
        first = sum(jnp.where(starts[j + 1] <= s0, 1.0, 0.0) for j in range(n_blk))
        last = sum(jnp.where(starts[j] < s0 + width, 1.0, 0.0) for j in range(n_blk))
        want = slot + float(s0)
        for e in range(n_exp):
            def per_block(j, acc, e=e, want=want):
                key_row = key_ref[j, e:e + 1, :]
                tok = lane + lax.convert_element_type(j * LANES, F32)
                return acc + jnp.where(key_row == want, tok, 0.0)

            acc = lax.fori_loop(first[e, 0].astype(I32), last[e, 0].astype(I32), per_block,
                                jnp.zeros((LANES, LANES), F32))
            row = jnp.sum(acc.T, axis=0, keepdims=True)
            idx_ref[0, e:e + 1, s0:s0 + width] = row[:, :width].astype(I32)


def _route(aff_t):
    b, n_exp, n = aff_t.shape
    cap = CAPACITY_FACTOR * n // n_exp
    kern = functools.partial(_topk_kernel, cap=cap)
    idx, gate = pl.pallas_call(
        kern,
        out_shape=[jax.ShapeDtypeStruct((b, n_exp, cap), I32), jax.ShapeDtypeStruct((b, n_exp, n), F32)],
        grid=(b,),
        in_specs=[pl.BlockSpec((1, n_exp, n), lambda i: (i, 0, 0))],
        out_specs=[pl.BlockSpec((1, n_exp, cap), lambda i: (i, 0, 0)),
                   pl.BlockSpec((1, n_exp, n), lambda i: (i, 0, 0))],
        scratch_shapes=[pltpu.VMEM((n_exp, n), F32), pltpu.VMEM((n // LANES, n_exp, LANES), F32)],
        compiler_params=_params(("parallel",)), name="expert_choice_route",
    )(aff_t)
    return idx.reshape(b * n_exp, 1, cap), gate.reshape(b * n_exp, 1, n)


def _moe_kernel(idx_ref, nxt_ref, h_ref, wg_ref, wu_ref, wd_ref, y_ref, xg_ref, *, cap, prefetch):
    sub = lax.broadcasted_iota(I32, (SUBLANES, h_ref.shape[2]), 0)

    def gather_group(src_ref, dst, base):
        tile = jnp.zeros(sub.shape, F32)
        for u in range(SUBLANES):
            t = src_ref[0, 0, base + u]
            r = t & (SUBLANES - 1)
            rows = h_ref[0, pl.ds(pl.multiple_of(t - r, SUBLANES), SUBLANES), :]
            rolled = pltpu.roll(rows, (u - r) & (SUBLANES - 1), axis=0)
            tile = jnp.where(sub == u, rolled, tile)
        dst[pl.ds(base, SUBLANES), :] = tile

    def gather_loop(dst):
        def body(i8, carry):
            gather_group(idx_ref, dst, pl.multiple_of(i8 * SUBLANES, SUBLANES))
            return carry
        lax.fori_loop(0, cap // SUBLANES, body, 0)

    if prefetch:
        e = pl.program_id(1)
        slot = e & 1

        @pl.when(e == 0)
        def _():
            gather_loop(xg_ref.at[0])
    else:
        slot = 0
        gather_loop(xg_ref.at[0])
    xg = xg_ref[slot].astype(BF16)
    gate = _dot(xg, wg_ref[0])
    up = _dot(xg, wu_ref[0])
    hid = (_silu(gate) * up).astype(BF16)
    y_ref[0, 0] = _dot(hid, wd_ref[0])
    if prefetch:
        for i8 in range(cap // SUBLANES):
            gather_group(nxt_ref, xg_ref.at[1 - slot], i8 * SUBLANES)


def _combine_kernel(idx_ref, gate_ref, y_ref, acc_ref, *, cap):
    @pl.when(pl.program_id(1) == 0)
    def _():
        acc_ref[...] = jnp.zeros_like(acc_ref)

    sub = lax.broadcasted_iota(I32, (SUBLANES, acc_ref.shape[2]), 0)

    def scatter(i8, carry):
        base = pl.multiple_of(i8 * SUBLANES, SUBLANES)
        y_tile = y_ref[0, 0, pl.ds(base, SUBLANES), :]
        for u0 in range(0, SUBLANES, 4):
            pending = []
            for u in range(u0, u0 + 4):
                t = idx_ref[0, 0, base + u]
                g = gate_ref[0, 0, t]
                r = t & (SUBLANES - 1)
                dst = acc_ref.at[0, pl.ds(pl.multiple_of(t - r, SUBLANES), SUBLANES), :]
                rolled = pltpu.roll(y_tile, (r - u) & (SUBLANES - 1), axis=0)
                pending.append((dst, dst[...] + g * rolled, sub == r))
            for dst, new, mask in pending:
                pltpu.store(dst, new, mask=mask)
        return carry

    lax.fori_loop(0, cap // SUBLANES, scatter, 0)


def _expert_ffn(h, idx, gate, wg, wu, wd):
    b, n, d = h.shape
    n_exp, _, ff = wg.shape
    cap = idx.shape[2]
    smem = lambda width: pl.BlockSpec((1, 1, width), lambda i, e: (i * n_exp + e, 0, 0), memory_space=pltpu.SMEM)
    sample_major = n * d * h.dtype.itemsize >= 3 * d * ff * wg.dtype.itemsize
    if sample_major:
        grid = (b, n_exp)
        be = lambda i, e: (i, e)
        nxt = pl.BlockSpec((1, 1, cap), lambda i, e: (i * n_exp + jnp.minimum(e + 1, n_exp - 1), 0, 0),
                           memory_space=pltpu.SMEM)
        h_spec = pl.BlockSpec((1, n, d), lambda i, e: (i, 0, 0), pipeline_mode=pl.Buffered(1))
    else:
        grid = (n_exp, b)
        be = lambda e, i: (i, e)
        nxt = pl.BlockSpec((1, 1, cap), lambda e, i: (i * n_exp + e, 0, 0), memory_space=pltpu.SMEM)
        h_spec = pl.BlockSpec((1, n, d), lambda e, i: (i, 0, 0))
    y = pl.pallas_call(
        functools.partial(_moe_kernel, cap=cap, prefetch=sample_major),
        out_shape=jax.ShapeDtypeStruct((b, n_exp, cap, d), F32),
        grid=grid,
        in_specs=[pl.BlockSpec((1, 1, cap), lambda *g: (be(*g)[0] * n_exp + be(*g)[1], 0, 0),
                               memory_space=pltpu.SMEM),
                  nxt, h_spec,
                  pl.BlockSpec((1, d, ff), lambda *g: (be(*g)[1], 0, 0)),
                  pl.BlockSpec((1, d, ff), lambda *g: (be(*g)[1], 0, 0)),
                  pl.BlockSpec((1, ff, d), lambda *g: (be(*g)[1], 0, 0))],
        out_specs=pl.BlockSpec((1, 1, cap, d), lambda *g: (*be(*g), 0, 0)),
        scratch_shapes=[pltpu.VMEM((2, cap, d), F32)],
        compiler_params=_params(("parallel", "arbitrary")), name="expert_ffn",
    )(idx, idx, h, wg, wu, wd)
    return pl.pallas_call(
        functools.partial(_combine_kernel, cap=cap),
        out_shape=jax.ShapeDtypeStruct((b, n, d), F32),
        grid=(b, n_exp),
        in_specs=[smem(cap), smem(n), pl.BlockSpec((1, 1, cap, d), lambda i, e: (i, e, 0, 0))],
        out_specs=pl.BlockSpec((1, n, d), lambda i, e: (i, 0, 0)),
        compiler_params=_params(("parallel", "arbitrary")), name="expert_combine",
    )(idx, gate, y)


def _residual_kernel(x_ref, f_ref, g_ref, gate_ref, o_ref):
    o_ref[0] = x_ref[0] + gate_ref[0] * _rms(f_ref[0], g_ref[...])


def _gated_residual(x, f, g, gate):
    b, n, d = x.shape
    tm = min(512, n)
    bm = gate.shape[0]
    mod_map = (lambda i, j: (i, 0, 0)) if bm > 1 else (lambda i, j: (0, 0, 0))
    tile = pl.BlockSpec((1, tm, d), lambda i, j: (i, j, 0))
    return pl.pallas_call(
        _residual_kernel, out_shape=jax.ShapeDtypeStruct((b, n, d), F32),
        grid=(b, n // tm),
        in_specs=[tile, tile, pl.BlockSpec((1, d), lambda i, j: (0, 0)), pl.BlockSpec((1, 1, d), mod_map)],
        out_specs=tile,
        compiler_params=_params(("parallel", "parallel")), name="gated_residual",
    )(x, f, g.reshape(1, d), gate)


def _rope_tables(n):
    t = jnp.arange(n, dtype=I32)
    row = (t // GRID_W).astype(F32)
    col = (t % GRID_W).astype(F32)
    n_freq = HEAD_DIM // 4
    inv_freq = ROPE_THETA ** (-jnp.arange(n_freq, dtype=F32) / n_freq)
    ang = jnp.concatenate([row[:, None] * inv_freq, col[:, None] * inv_freq], axis=-1)
    cos, sin = jnp.cos(ang), jnp.sin(ang)
    cos_t = jnp.tile(jnp.concatenate([cos, cos], axis=-1), (1, LANES // HEAD_DIM))
    sin_t = jnp.tile(jnp.concatenate([-sin, sin], axis=-1), (1, LANES // HEAD_DIM))
    return cos_t, sin_t


def _deinterleave_heads(w):
    d, width = w.shape
    w = w.reshape(d, width // HEAD_DIM, HEAD_DIM // 2, 2)
    return jnp.concatenate([w[..., 0], w[..., 1]], axis=-1).reshape(d, width)


def _dup_heads(w):
    d, width = w.shape
    w = w.reshape(d, width // HEAD_DIM, 1, HEAD_DIM)
    return jnp.concatenate([w, w], axis=2).reshape(d, 2 * width)


def _mod_split(mod_rows):
    sh1, sc1, gt1, sh2, sc2, gt2 = jnp.split(mod_rows[:, None, :], 6, axis=-1)
    return sh1, 1.0 + sc1, gt1, sh2, 1.0 + sc2, gt2


def _moe_block(x_mid, h2, aff_t, g_post_ffn, gt2, wg, wu, wd):
    b, n, d = h2.shape
    n_exp, _, ff = wg.shape
    idx, gate = _route(aff_t)
    if b > 1 and n * d * h2.dtype.itemsize < 3 * d * ff * wg.dtype.itemsize:
        cap = idx.shape[2]
        idx = idx.reshape(b, n_exp, cap) + (jnp.arange(b, dtype=I32) * n)[:, None, None]
        idx = idx.transpose(1, 0, 2).reshape(n_exp, 1, b * cap)
        gate = gate.reshape(b, n_exp, n).transpose(1, 0, 2).reshape(n_exp, 1, b * n)
        ffn = _expert_ffn(h2.reshape(1, b * n, d), idx, gate, wg, wu, wd).reshape(b, n, d)
    else:
        ffn = _expert_ffn(h2, idx, gate, wg, wu, wd)
    return _gated_residual(x_mid, ffn, g_post_ffn, gt2)


def kernel(x, c, ctx, c_ctx, l0_w_mod, l0_b_mod, l0_g_pre_mix, l0_g_post_mix, l0_g_pre_ffn, l0_g_post_ffn, l0_w_in, l0_w_out, l0_lam_q1, l0_lam_k1, l0_lam_q2, l0_lam_k2, l0_subln_g, l0_rpb, l0_w_router, l0_w_gate, l0_w_up, l0_w_down, l1_w_mod, l1_b_mod, l1_g_pre_mix, l1_g_post_mix, l1_g_pre_ffn, l1_g_post_ffn, l1_w_in, l1_w_out, l1_sinks, l1_conv_w, l1_conv_b, l1_a_log_f, l1_a_log_b, l1_dt_bias_f, l1_dt_bias_b, l1_d_skip, l1_gnorm_g, l1_w_router, l1_w_gate, l1_w_up, l1_w_down):
    b, n, d = x.shape
    m = ctx.shape[1]
    half = d // 2
    scale = HEAD_DIM ** -0.5 * math.log2(math.e)
    cos_t, sin_t = _rope_tables(n)
    cond = jnp.zeros((16, d), F32).at[:b].set(c).at[b].set(c_ctx)

    mod = _modulation(cond, l0_w_mod, l0_b_mod)
    sh1, sc1, gt1, sh2, sc2, gt2 = _mod_split(mod[:b])
    csh1, csc1, cgt1, csh2, csc2, cgt2 = _mod_split(mod[b:b + 1])

    a_qk = half
    wq, wk, rest = l0_w_in[:, :a_qk], l0_w_in[:, a_qk:2 * a_qk], l0_w_in[:, 2 * a_qk:]
    qb_cols = slice(half, 2 * half)
    rest = rest.at[:, qb_cols].multiply(scale)
    w_in0 = jnp.concatenate([_deinterleave_heads(wq) * scale, _deinterleave_heads(wk), rest], axis=1).astype(BF16)
    segs0 = [(half, BF16)] * 6
    qa, ka, va, qb, kb, vb = _project(x, l0_g_pre_mix, sc1, sh1, w_in0, segs0, rope=(2 * a_qk, cos_t, sin_t))
    qa_c, ka_c, va_c, qb_c, kb_c, vb_c = _project(ctx, l0_g_pre_mix, csc1, csh1, w_in0, segs0)

    lam_init = 0.8 - 0.6 * math.exp(-0.3 * 0)
    lam = (jnp.exp(jnp.sum(l0_lam_q1.astype(F32) * l0_lam_k1.astype(F32)))
           - jnp.exp(jnp.sum(l0_lam_q2.astype(F32) * l0_lam_k2.astype(F32))) + lam_init)
    o_a = _diff_attention(qa, [(ka, va), (ka_c, va_c)], lam, l0_subln_g, 1.0 - lam_init)
    o_b = _neighbourhood_attention(qb, kb, vb, kb_c, vb_c, l0_rpb)
    oc_a = _diff_attention(qa_c, [(ka_c, va_c)], lam, l0_subln_g, 1.0 - lam_init)
    oc_b = _context_attention(qb_c, kb_c, vb_c)

    w_out0 = l0_w_out.astype(BF16)
    wg0, wu0, wd0 = l0_w_gate.astype(BF16), l0_w_up.astype(BF16), l0_w_down.astype(BF16)
    x_mid, h2, aff_t = _out_project(o_a, o_b, w_out0, x, l0_g_post_mix, gt1, l0_g_pre_ffn, sc2, sh2, l0_w_router)
    x1 = _moe_block(x_mid, h2, aff_t, l0_g_post_ffn, gt2, wg0, wu0, wd0)
    c_mid, hc2, aff_c = _out_project(oc_a, oc_b, w_out0, ctx, l0_g_post_mix, cgt1, l0_g_pre_ffn, csc2, csh2,
                                     l0_w_router)
    ctx1 = _moe_block(c_mid, hc2, aff_c, l0_g_post_ffn, cgt2, wg0, wu0, wd0)

    mod = _modulation(cond, l1_w_mod, l1_b_mod)
    sh1, sc1, gt1, sh2, sc2, gt2 = _mod_split(mod[:b])
    csh1, csc1, _, _, _, _ = _mod_split(mod[b:b + 1])

    c_q = half
    d_inner = half
    d_heads = l1_a_log_f.shape[0]
    d_xbc = l1_conv_w.shape[2]
    c_kv = (l1_w_in.shape[1] - c_q - d_xbc - d_inner - 2 * d_heads) // 2
    n_groups = 2
    d_state = (d_xbc - d_inner) // (2 * n_groups)
    wq = l1_w_in[:, :c_q]
    wk = l1_w_in[:, c_q:c_q + c_kv]
    wv = l1_w_in[:, c_q + c_kv:c_q + 2 * c_kv]
    o_x = c_q + 2 * c_kv
    w_x = l1_w_in[:, o_x:o_x + d_xbc + d_inner]
    w_dt = l1_w_in[:, o_x + d_xbc + d_inner:]
    w_kv = jnp.concatenate([_dup_heads(_deinterleave_heads(wk)), _dup_heads(wv), w_x, w_dt], axis=1)
    w_lat = jnp.concatenate([_deinterleave_heads(wq) * scale, w_kv], axis=1).astype(BF16)
    w_ctx = w_kv.astype(BF16)
    w_dt_t = w_dt.T.astype(BF16)
    segs_lat = [(c_q, BF16), (2 * c_kv, BF16), (2 * c_kv, BF16), (d_xbc, F32), (d_inner, F32), (2 * d_heads, F32)]
    q, k2, v2, xbc, z, dt, dtt = _project(x1, l1_g_pre_mix, sc1, sh1, w_lat, segs_lat,
                                          rope=(c_q + 2 * c_kv, cos_t, sin_t), wt=w_dt_t)
    k2_c, v2_c, xbc_c, _, dt_c, dtt_c = _project(ctx1, l1_g_pre_mix, csc1, csh1, w_ctx, segs_lat[1:], wt=w_dt_t)

    o_c = _window_attention(q, k2, v2, k2_c, v2_c, l1_sinks.astype(F32) * math.log2(math.e))

    a_f = -jnp.exp(l1_a_log_f.astype(F32))
    a_b = -jnp.exp(l1_a_log_b.astype(F32))
    u = _conv_silu(xbc, l1_conv_w, l1_conv_b)
    u_c = _conv_silu(xbc_c, l1_conv_w, l1_conv_b)
    ssd = functools.partial(_ssd_scan, d_inner=d_inner, d_state=d_state, n_groups=n_groups)
    zero_state = jnp.zeros((b, d_heads // 2, d_state, LANES), F32)
    _, sc_f = ssd(u_c, dt_c[..., :d_heads], dtt_c[:, :d_heads], a_f, l1_dt_bias_f, zero_state, reverse=False)
    _, sc_b = ssd(u_c, dt_c[..., d_heads:], dtt_c[:, d_heads:], a_b, l1_dt_bias_b, zero_state, reverse=True)
    y_b, _ = ssd(u, dt[..., d_heads:], dtt[:, d_heads:], a_b, l1_dt_bias_b, sc_b, reverse=True)
    o_d, _ = ssd(u, dt[..., :d_heads], dtt[:, :d_heads], a_f, l1_dt_bias_f, sc_f, reverse=False,
                 final=(z, y_b, l1_d_skip, l1_gnorm_g))

    x_mid, h2, aff_t = _out_project(o_c, o_d, l1_w_out.astype(BF16), x1, l1_g_post_mix, gt1, l1_g_pre_ffn, sc2, sh2,
                                    l1_w_router)
    return _moe_block(x_mid, h2, aff_t, l1_g_post_ffn, gt2, l1_w_gate.astype(BF16), l1_w_up.astype(BF16),
                      l1_w_down.astype(BF16))
```

```python
import functools
import math

import jax
import jax.numpy as jnp
from jax import lax
from jax.experimental import pallas as pl
from jax.experimental.pallas import tpu as pltpu

F32 = jnp.float32
BF16 = jnp.bfloat16
I32 = jnp.int32

HEAD_DIM = 64
LANES = 128
SUBLANES = 8
GRID_W = 64
ROPE_THETA = 10000.0
NORM_EPS = 1e-6
NA_ROWS = 8
NA_COLS = 16
WINDOW = 128
Q_BLOCK = 128
SSD_CHUNK = 128
D_CONV = 5
CAPACITY_FACTOR = 2
NEG_BIG = -1e30
VMEM_LIMIT = 56 * 1024 * 1024

ROW_TILE = 512
COL_CHUNK = 512
MOD_COLS = 1024
OUT_PROJ_ROWS = 1024
ATTN_Q_TILE = 512
ATTN_KEY_CHUNK = 512
NA_BLOCKS_PER_STEP = 4
SWA_BLOCKS_PER_STEP = 8
SSD_CHUNKS_PER_STEP = 4
HIGHEST = lax.Precision.HIGHEST
NT_DIMS = (((1,), (1,)), ((), ()))


def _params(sem, vmem=VMEM_LIMIT):
    return pltpu.CompilerParams(dimension_semantics=sem, vmem_limit_bytes=vmem)


def _rms(u, g):
    return u * lax.rsqrt(jnp.mean(u * u, axis=-1, keepdims=True) + NORM_EPS) * g


def _silu(u):
    return u * (1.0 / (1.0 + jnp.exp(-u)))


def _softplus(u):
    return jnp.maximum(u, 0.0) + jnp.log(1.0 + jnp.exp(-jnp.abs(u)))


def _dot(a, b):
    return jnp.dot(a, b, preferred_element_type=F32)


def _dot_nt(a, b):
    return lax.dot_general(a, b, NT_DIMS, preferred_element_type=F32)


def _half_masks(shape):
    lane = lax.broadcasted_iota(I32, shape, 1)
    return (lane % LANES) < HEAD_DIM


def _mod_kernel(c_ref, w_ref, b_ref, o_ref):
    s = _silu(c_ref[...]).astype(BF16)
    o_ref[...] = _dot(s, w_ref[...].astype(BF16)) + b_ref[...]


def _modulation(cond, w_mod, b_mod):
    rows, d = cond.shape
    n_out = w_mod.shape[1]
    tn = MOD_COLS
    return pl.pallas_call(
        _mod_kernel,
        out_shape=jax.ShapeDtypeStruct((rows, n_out), F32),
        grid=(n_out // tn,),
        in_specs=[pl.BlockSpec((rows, d), lambda j: (0, 0)),
                  pl.BlockSpec((d, tn), lambda j: (0, j)),
                  pl.BlockSpec((1, tn), lambda j: (0, j))],
        out_specs=pl.BlockSpec((rows, tn), lambda j: (0, j)),
        compiler_params=_params(("parallel",)),
        name="modulation",
    )(cond, w_mod, b_mod.reshape(1, n_out))


def _rope_block(u, cos, sin_signed, first_half):
    width = u.shape[1]
    fwd = pltpu.roll(u, HEAD_DIM // 2, axis=1)
    bwd = pltpu.roll(u, width - HEAD_DIM // 2, axis=1)
    return u * cos + jnp.where(first_half, bwd, fwd) * sin_signed


def _proj_kernel(*refs, seg_widths, rope_cols, has_t, col_chunk):
    x_ref, g_ref, sc_ref, sh_ref, w_ref = refs[:5]
    pos = 5
    if rope_cols:
        cos_ref, sin_ref = refs[pos:pos + 2]
        pos += 2
    if has_t:
        wt_ref = refs[pos]
        pos += 1
    out_refs = refs[pos:]
    h = (_rms(x_ref[0], g_ref[...]) * sc_ref[0] + sh_ref[0]).astype(BF16)
    if rope_cols:
        cos = cos_ref[...]
        sin_signed = sin_ref[...]
        first_half = (lax.broadcasted_iota(I32, cos.shape, 1) % HEAD_DIM) < HEAD_DIM // 2
    col = 0
    for o_ref, width in zip(out_refs, seg_widths):
        for c0 in range(0, width, col_chunk):
            cw = min(col_chunk, width - c0)
            acc = _dot(h, w_ref[:, col + c0:col + c0 + cw])
            if col + c0 < rope_cols:
                acc = jnp.concatenate(
                    [_rope_block(acc[:, k:k + LANES], cos, sin_signed, first_half)
                     for k in range(0, cw, LANES)], axis=1)
            o_ref[0, :, c0:c0 + cw] = acc.astype(o_ref.dtype)
        col += width
    if has_t:
        out_refs[-1][0] = _dot_nt(wt_ref[...], h)


def _project(x, g, scale1p, shift, w, segs, rope=None, wt=None):
    b, n, d = x.shape
    tm = min(ROW_TILE, n)
    bm = scale1p.shape[0]
    mod_map = (lambda i, j: (i, 0, 0)) if bm > 1 else (lambda i, j: (0, 0, 0))
    in_specs = [pl.BlockSpec((1, tm, d), lambda i, j: (i, j, 0)),
                pl.BlockSpec((1, d), lambda i, j: (0, 0)),
                pl.BlockSpec((1, 1, d), mod_map),
                pl.BlockSpec((1, 1, d), mod_map),
                pl.BlockSpec(w.shape, lambda i, j: (0, 0))]
    args = [x, g.reshape(1, d), scale1p, shift, w]
    rope_cols = 0
    if rope is not None:
        rope_cols, cos, sin_signed = rope
        in_specs += [pl.BlockSpec((tm, LANES), lambda i, j: (j, 0))] * 2
        args += [cos, sin_signed]
    if wt is not None:
        in_specs.append(pl.BlockSpec(wt.shape, lambda i, j: (0, 0)))
        args.append(wt)
    out_shape = [jax.ShapeDtypeStruct((b, n, wd), dt) for wd, dt in segs]
    out_specs = [pl.BlockSpec((1, tm, wd), lambda i, j: (i, j, 0)) for wd, _ in segs]
    if wt is not None:
        out_shape.append(jax.ShapeDtypeStruct((b, wt.shape[0], n), F32))
        out_specs.append(pl.BlockSpec((1, wt.shape[0], tm), lambda i, j: (i, 0, j)))
    kern = functools.partial(_proj_kernel, seg_widths=tuple(wd for wd, _ in segs),
                             rope_cols=rope_cols, has_t=wt is not None, col_chunk=COL_CHUNK)
    return pl.pallas_call(
        kern, out_shape=out_shape, grid=(b, n // tm), in_specs=in_specs, out_specs=out_specs,
        compiler_params=_params(("parallel", "parallel")), name="in_proj",
    )(*args)


def _diff_kernel(*refs, seg_lens, tk, out_scale, heads):
    n_seg = len(seg_lens)
    m_all, acc_all, s_all = refs[4 + 2 * n_seg:]
    for h in range(heads):
        _diff_head(refs[:4 + 2 * n_seg], m_all.at[h], acc_all.at[h], s_all.at[2 * (h % 2)],
                   s_all.at[2 * (h % 2) + 1], slice(h * LANES, (h + 1) * LANES), seg_lens, tk, out_scale)


def _diff_head(io_refs, m_ref, acc_ref, s0_ref, s1_ref, cols, seg_lens, tk, out_scale):
    lam_ref, q_ref, g_ref = io_refs[:3]
    n_seg = len(seg_lens)
    kv_refs = [r.at[:, :, cols] for r in io_refs[3:3 + 2 * n_seg]]
    o_ref = io_refs[3 + 2 * n_seg]
    qf = q_ref[0, :, cols].astype(F32)
    first = _half_masks(qf.shape)
    qs = (jnp.where(first, qf, 0.0).astype(BF16), jnp.where(first, 0.0, qf).astype(BF16))
    m_ref[...] = jnp.full(m_ref.shape, NEG_BIG, F32)
    acc_ref[...] = jnp.zeros(acc_ref.shape, F32)

    def scores(kc):
        return [_dot_nt(qs[t], kc) for t in range(2)]

    def absorb(sc, vc):
        v_ones = jnp.concatenate([vc, jnp.ones(vc.shape, BF16)], axis=1)
        for t in range(2):
            m_old = m_ref[t]
            m_new = jnp.maximum(m_old, jnp.max(sc[t], axis=1, keepdims=True))
            p = jnp.exp2(sc[t] - m_new).astype(BF16)
            acc_ref[t] = jnp.exp2(m_old - m_new) * acc_ref[t] + _dot(p, v_ones)
            m_ref[t] = m_new

    def put(buf, sc):
        buf[0] = sc[0]
        buf[1] = sc[1]

    main = max(range(n_seg), key=lambda i: seg_lens[i])
    n_chunks = seg_lens[main] // tk if seg_lens[main] >= 2 * tk else 1
    if n_chunks > 1:
        k_ref, v_ref = kv_refs[2 * main], kv_refs[2 * main + 1]
        rows = lambda ref, c: ref[0, pl.ds(pl.multiple_of(c * tk, tk), tk), :]
    singles = [s_i for s_i in range(n_seg) if s_i != main or n_chunks == 1]
    single_scores = [scores(kv_refs[2 * s_i][0]) for s_i in singles]
    if n_chunks > 1:
        put(s0_ref, scores(rows(k_ref, 0)))
    for s_i, sc in zip(singles, single_scores):
        absorb(sc, kv_refs[2 * s_i + 1][0])
    if n_chunks > 1:
        def pair(c, last):
            put(s1_ref, scores(rows(k_ref, c + 1)))
            absorb([s0_ref[0], s0_ref[1]], rows(v_ref, c))
            if not last:
                put(s0_ref, scores(rows(k_ref, c + 2)))
            absorb([s1_ref[0], s1_ref[1]], rows(v_ref, c + 1))

        def body(j, carry):
            pair(2 * j, False)
            return carry

        lax.fori_loop(0, n_chunks // 2 - 1, body, 0)
        pair(n_chunks - 2, True)
    ratio0 = acc_ref[0, :, :LANES] / acc_ref[0, :, LANES:]
    ratio1 = acc_ref[1, :, :LANES] / acc_ref[1, :, LANES:]
    o = ratio0 - lam_ref[0] * ratio1
    o_ref[0, :, cols] = (_rms(o, g_ref[...]) * out_scale).astype(o_ref.dtype)


def _diff_attention(q, kv_segs, lam, subln_g, out_scale):
    b, nq, width = q.shape
    heads = width // LANES
    tq = min(ATTN_Q_TILE, nq)
    in_specs = [pl.BlockSpec(memory_space=pltpu.SMEM),
                pl.BlockSpec((1, tq, width), lambda i, j: (i, j, 0)),
                pl.BlockSpec((1, LANES), lambda i, j: (0, 0))]
    args = [lam.reshape(1).astype(F32), q, subln_g.reshape(1, LANES).astype(F32)]
    seg_lens = []
    for k, v in kv_segs:
        length = k.shape[1]
        seg_lens.append(length)
        in_specs += [pl.BlockSpec((1, length, width), lambda i, j: (i, 0, 0))] * 2
        args += [k, v]
    tk = ATTN_KEY_CHUNK
    longest = max(seg_lens)
    assert longest < 2 * tk or longest % (2 * tk) == 0, (longest, tk)
    kern = functools.partial(_diff_kernel, seg_lens=tuple(seg_lens), tk=tk, out_scale=out_scale, heads=heads)
    return pl.pallas_call(
        kern, out_shape=jax.ShapeDtypeStruct((b, nq, width), BF16),
        grid=(b, nq // tq), in_specs=in_specs,
        out_specs=pl.BlockSpec((1, tq, width), lambda i, j: (i, j, 0)),
        scratch_shapes=[pltpu.VMEM((heads, 2, tq, 1), F32), pltpu.VMEM((heads, 2, tq, 2 * LANES), F32),
                        pltpu.VMEM((4, 2, tq, tk), F32)],
        compiler_params=_params(("parallel", "parallel")), name="diff_attention",
    )(*args)


NA_BLOCK_ROWS = 4
NA_WIN_ROWS = NA_BLOCK_ROWS + NA_ROWS


def _na_window_start(blk, rows):
    return jnp.clip(blk * NA_BLOCK_ROWS - NA_ROWS // 2, 0, rows - NA_WIN_ROWS)


def _natten_kernel(q_ref, k_ref, v_ref, kc_ref, vc_ref, bias_ref, o_ref, *, rows):
    nq = NA_BLOCK_ROWS * GRID_W
    per_step = q_ref.shape[1] // nq
    for sb in range(per_step):
        _natten_block(q_ref, k_ref, v_ref, kc_ref, vc_ref, bias_ref, o_ref, pl.program_id(1) * per_step + sb,
                      slice(sb * nq, (sb + 1) * nq), rows)


def _natten_block(q_ref, k_ref, v_ref, kc_ref, vc_ref, bias_ref, o_ref, blk, qrows, rows):
    start = pl.multiple_of(_na_window_start(blk, rows) * GRID_W, GRID_W)
    n_keys = NA_WIN_ROWS * GRID_W
    nq = NA_BLOCK_ROWS * GRID_W
    case = jnp.where(blk == 0, 0, jnp.where(blk == rows // NA_BLOCK_ROWS - 1, 2, 1))
    q = q_ref[0, qrows, :].astype(F32)
    kwin = k_ref[0, pl.ds(start, n_keys), :]
    vwin = v_ref[0, pl.ds(start, n_keys), :]
    kc = kc_ref[0]
    vc = vc_ref[0]
    first = _half_masks((nq, LANES))
    for p in range(q.shape[1] // LANES):
        cols = slice(p * LANES, (p + 1) * LANES)
        qs = jnp.concatenate([jnp.where(first, q[:, cols], 0.0), jnp.where(first, 0.0, q[:, cols])],
                             axis=0).astype(BF16)
        s_nb = _dot_nt(qs, kwin[:, cols]) + jnp.concatenate([bias_ref[case, 2 * p], bias_ref[case, 2 * p + 1]],
                                                            axis=0)
        s_cx = _dot_nt(qs, kc[:, cols])
        m = jnp.maximum(jnp.max(s_nb, axis=1, keepdims=True), jnp.max(s_cx, axis=1, keepdims=True))
        p_nb = jnp.exp2(s_nb - m).astype(BF16)
        p_cx = jnp.exp2(s_cx - m).astype(BF16)
        v_nb = jnp.concatenate([vwin[:, cols], jnp.ones((n_keys, LANES), BF16)], axis=1)
        v_cx = jnp.concatenate([vc[:, cols], jnp.ones((vc.shape[0], LANES), BF16)], axis=1)
        acc = _dot(p_nb, v_nb) + _dot(p_cx, v_cx)
        ratio = acc[:, :LANES] / acc[:, LANES:]
        o_ref[0, qrows, cols] = jnp.where(first, ratio[:nq], ratio[nq:]).astype(o_ref.dtype)


def _na_bias_table(rpb, rows):
    heads = rpb.shape[0]
    w = jnp.arange(GRID_W)
    kc = jnp.arange(GRID_W)
    col_start = jnp.clip(w - NA_COLS // 2, 0, GRID_W - NA_COLS)
    valid = (kc[None, :] >= col_start[:, None]) & (kc[None, :] < col_start[:, None] + NA_COLS)
    pad = GRID_W - NA_COLS
    padded = jnp.pad(rpb.astype(F32) * math.log2(math.e), ((0, 0), (0, 0), (pad, pad)))
    cols = jnp.stack([padded[:, :, GRID_W - 1 - wi:2 * GRID_W - 1 - wi] for wi in range(GRID_W)], axis=2)
    cols = jnp.where(valid[None, None], cols, NEG_BIG)
    masked = jnp.full((heads, GRID_W, GRID_W), NEG_BIG, F32)
    cases = []
    for first_row in (0, NA_BLOCK_ROWS, rows - NA_BLOCK_ROWS):
        win0 = min(max(first_row - NA_ROWS // 2, 0), rows - NA_WIN_ROWS)
        per_q = []
        for qr in range(NA_BLOCK_ROWS):
            r = first_row + qr
            rs = min(max(r - NA_ROWS // 2, 0), rows - NA_ROWS)
            per_k = [cols[:, win0 + kr - r + NA_ROWS - 1] if rs <= win0 + kr < rs + NA_ROWS else masked
                     for kr in range(NA_WIN_ROWS)]
            per_q.append(jnp.stack(per_k, axis=2))
        cases.append(jnp.stack(per_q, axis=1))
    return jnp.stack(cases).reshape(3, heads, NA_BLOCK_ROWS * GRID_W, NA_WIN_ROWS * GRID_W)


def _neighbourhood_attention(q, k, v, kc, vc, rpb):
    b, n, width = q.shape
    m = kc.shape[1]
    rows = n // GRID_W
    n_blk = rows // NA_BLOCK_ROWS
    assert rows % NA_BLOCK_ROWS == 0 and rows >= NA_WIN_ROWS + NA_BLOCK_ROWS
    bias = _na_bias_table(rpb, rows)
    heads = rpb.shape[0]
    step_q = NA_BLOCK_ROWS * GRID_W * math.gcd(n_blk, NA_BLOCKS_PER_STEP)
    kern = functools.partial(_natten_kernel, rows=rows)
    return pl.pallas_call(
        kern, out_shape=jax.ShapeDtypeStruct((b, n, width), BF16),
        grid=(b, n // step_q),
        in_specs=[pl.BlockSpec((1, step_q, width), lambda i, r: (i, r, 0)),
                  pl.BlockSpec((1, n, width), lambda i, r: (i, 0, 0)),
                  pl.BlockSpec((1, n, width), lambda i, r: (i, 0, 0)),
                  pl.BlockSpec((1, m, width), lambda i, r: (i, 0, 0)),
                  pl.BlockSpec((1, m, width), lambda i, r: (i, 0, 0)),
                  pl.BlockSpec(bias.shape, lambda i, r: (0, 0, 0, 0), pipeline_mode=pl.Buffered(1))],
        out_specs=pl.BlockSpec((1, step_q, width), lambda i, r: (i, r, 0)),
        compiler_params=_params(("parallel", "parallel")), name="neighbourhood_attention",
    )(q, k, v, kc, vc, bias)


def _ctx_attn_kernel(q_ref, k_ref, v_ref, o_ref):
    q = q_ref[0].astype(F32)
    k = k_ref[0]
    v = v_ref[0]
    first = _half_masks((q.shape[0], LANES))
    for p in range(q.shape[1] // LANES):
        cols = slice(p * LANES, (p + 1) * LANES)
        outs = []
        for sub in range(2):
            qm = jnp.where(first if sub == 0 else ~first, q[:, cols], 0.0).astype(BF16)
            s = _dot_nt(qm, k[:, cols])
            e = jnp.exp2(s - jnp.max(s, axis=1, keepdims=True))
            outs.append(_dot(e.astype(BF16), v[:, cols]) / jnp.sum(e, axis=1, keepdims=True))
        o_ref[0, :, cols] = jnp.where(first, outs[0], outs[1]).astype(o_ref.dtype)


def _context_attention(q, k, v):
    b, m, width = q.shape
    spec = pl.BlockSpec((1, m, width), lambda i: (i, 0, 0))
    return pl.pallas_call(
        _ctx_attn_kernel, out_shape=jax.ShapeDtypeStruct((b, m, width), BF16),
        grid=(b,), in_specs=[spec, spec, spec], out_specs=spec,
        compiler_params=_params(("parallel",)), name="context_attention",
    )(q, k, v)


def _swa_kernel(sink_ref, q_ref, k_ref, v_ref, kc_ref, vc_ref, o_ref, *, n_blocks):
    for sb in range(q_ref.shape[1] // Q_BLOCK):
        _swa_block(sink_ref, q_ref, k_ref, v_ref, kc_ref, vc_ref, o_ref, pl.program_id(1) * (q_ref.shape[1] // Q_BLOCK) + sb,
                   slice(sb * Q_BLOCK, (sb + 1) * Q_BLOCK), n_blocks)


def _swa_block(sink_ref, q_ref, k_ref, v_ref, kc_ref, vc_ref, o_ref, blk, qrows, n_blocks):
    span = 3 * Q_BLOCK
    start = pl.multiple_of(jnp.clip(blk - 1, 0, n_blocks - 3) * Q_BLOCK, Q_BLOCK)
    q = q_ref[0, qrows, :].astype(F32)
    kwin = k_ref[0, pl.ds(start, span), :]
    vwin = v_ref[0, pl.ds(start, span), :]
    kc = kc_ref[0]
    vc = vc_ref[0]
    first = _half_masks((Q_BLOCK, LANES))
    pairs = q.shape[1] // LANES
    n_groups = k_ref.shape[2] // LANES
    pairs_per_group = pairs // n_groups
    stack = 2 * pairs_per_group
    row = lax.broadcasted_iota(I32, (stack * Q_BLOCK, span), 0)
    qpos = blk * Q_BLOCK + (row & (Q_BLOCK - 1))
    kpos = start + lax.broadcasted_iota(I32, (stack * Q_BLOCK, span), 1)
    valid = jnp.abs(qpos - kpos) <= WINDOW
    head_of_row = lax.broadcasted_iota(I32, (stack * Q_BLOCK, 1), 0) >> (Q_BLOCK.bit_length() - 1)
    for g in range(n_groups):
        gcols = slice(g * LANES, (g + 1) * LANES)
        parts = []
        for p in range(g * pairs_per_group, (g + 1) * pairs_per_group):
            q_pair = q[:, p * LANES:(p + 1) * LANES]
            parts += [jnp.where(first, q_pair, 0.0), jnp.where(first, 0.0, q_pair)]
        qs = jnp.concatenate(parts, axis=0).astype(BF16)
        sink = jnp.zeros((stack * Q_BLOCK, 1), F32)
        for i in range(stack):
            sink = jnp.where(head_of_row == i, sink_ref[g * stack + i], sink)
        s_loc = jnp.where(valid, _dot_nt(qs, kwin[:, gcols]), NEG_BIG)
        s_cx = _dot_nt(qs, kc[:, gcols])
        m = jnp.maximum(jnp.maximum(jnp.max(s_loc, axis=1, keepdims=True),
                                    jnp.max(s_cx, axis=1, keepdims=True)), sink)
        p_loc = jnp.exp2(s_loc - m).astype(BF16)
        p_cx = jnp.exp2(s_cx - m).astype(BF16)
        v_loc = jnp.concatenate([vwin[:, gcols], jnp.ones((span, LANES), BF16)], axis=1)
        v_cx = jnp.concatenate([vc[:, gcols], jnp.ones((vc.shape[0], LANES), BF16)], axis=1)
        acc = _dot(p_loc, v_loc) + _dot(p_cx, v_cx)
        ratio = acc[:, :LANES] / (acc[:, LANES:] + jnp.exp2(sink - m))
        for i in range(pairs_per_group):
            p = g * pairs_per_group + i
            lo = ratio[2 * i * Q_BLOCK:(2 * i + 1) * Q_BLOCK]
            hi = ratio[(2 * i + 1) * Q_BLOCK:(2 * i + 2) * Q_BLOCK]
            o_ref[0, qrows, p * LANES:(p + 1) * LANES] = jnp.where(first, lo, hi).astype(o_ref.dtype)


def _window_attention(q, k2, v2, kc2, vc2, sinks):
    b, n, width = q.shape
    m = kc2.shape[1]
    kvw = k2.shape[2]
    n_blocks = n // Q_BLOCK
    step_rows = Q_BLOCK * math.gcd(n_blocks, SWA_BLOCKS_PER_STEP)
    kern = functools.partial(_swa_kernel, n_blocks=n_blocks)
    return pl.pallas_call(
        kern, out_shape=jax.ShapeDtypeStruct((b, n, width), BF16),
        grid=(b, n // step_rows),
        in_specs=[pl.BlockSpec(memory_space=pltpu.SMEM),
                  pl.BlockSpec((1, step_rows, width), lambda i, j: (i, j, 0)),
                  pl.BlockSpec((1, n, kvw), lambda i, j: (i, 0, 0)),
                  pl.BlockSpec((1, n, kvw), lambda i, j: (i, 0, 0)),
                  pl.BlockSpec((1, m, kvw), lambda i, j: (i, 0, 0)),
                  pl.BlockSpec((1, m, kvw), lambda i, j: (i, 0, 0))],
        out_specs=pl.BlockSpec((1, step_rows, width), lambda i, j: (i, j, 0)),
        compiler_params=_params(("parallel", "parallel")), name="window_attention",
    )(sinks.astype(F32), q, k2, v2, kc2, vc2)


def _conv_kernel(x_ref, prev_ref, next_ref, w_ref, b_ref, o_ref):
    i = pl.program_id(1)
    tm = x_ref.shape[1]
    keep_prev = jnp.where(i > 0, 1.0, 0.0)
    keep_next = jnp.where(i < pl.num_programs(1) - 1, 1.0, 0.0)
    xe = jnp.concatenate([prev_ref[0] * keep_prev, x_ref[0], next_ref[0] * keep_next], axis=0)
    acc = jnp.zeros((tm, x_ref.shape[2]), F32) + b_ref[...]
    for k in range(D_CONV):
        off = 8 - D_CONV // 2 + k
        acc = acc + w_ref[k:k + 1, :] * xe[off:off + tm, :]
    o_ref[0] = _silu(acc)


def _conv_silu(x, conv_w, conv_b):
    b, length, ch = x.shape
    tm = min(ROW_TILE, length)
    per = tm // 8
    last = length // 8 - 1
    return pl.pallas_call(
        _conv_kernel, out_shape=jax.ShapeDtypeStruct((b, length, ch), F32),
        grid=(b, length // tm),
        in_specs=[pl.BlockSpec((1, tm, ch), lambda i, j: (i, j, 0)),
                  pl.BlockSpec((1, 8, ch), lambda i, j: (i, jnp.maximum(j * per - 1, 0), 0)),
                  pl.BlockSpec((1, 8, ch), lambda i, j: (i, jnp.minimum((j + 1) * per, last), 0)),
                  pl.BlockSpec((D_CONV, ch), lambda i, j: (0, 0)),
                  pl.BlockSpec((1, ch), lambda i, j: (0, 0))],
        out_specs=pl.BlockSpec((1, tm, ch), lambda i, j: (i, j, 0)),
        compiler_params=_params(("parallel", "parallel")), name="conv_silu",
    )(x, x, x, conv_w.reshape(D_CONV, ch).astype(F32), conv_b.reshape(1, ch).astype(F32))


def _ssd_kernel(*refs, reverse, finalize, d_inner, d_state, n_groups):
    u_ref, dt_ref, dtt_ref, a_ref, at_ref, bias_ref, biast_ref, init_ref = refs[:8]
    pos = 8
    if finalize:
        z_ref, yb_ref, skip_ref, gn_ref = refs[pos:pos + 4]
        pos += 4
    y_ref, fin_ref, st_ref = refs[pos:pos + 3]
    c = pl.program_id(1)

    @pl.when(c == 0)
    def _():
        st_ref[...] = init_ref[0]

    q_len = SSD_CHUNK
    row = lax.broadcasted_iota(I32, (q_len, q_len), 0)
    col = lax.broadcasted_iota(I32, (q_len, q_len), 1)
    lower = row >= col
    upper = row <= col
    causal = upper if reverse else lower
    tri_col = causal.astype(F32)
    tri_row = (lower if reverse else upper).astype(F32)
    first = _half_masks((q_len, LANES))
    heads_per_group = (d_inner // HEAD_DIM) // n_groups
    pairs_per_group = heads_per_group // 2
    n_sub = u_ref.shape[1] // q_len
    for sc in (reversed(range(n_sub)) if reverse else range(n_sub)):
        _ssd_chunk(slice(sc * q_len, (sc + 1) * q_len), refs, causal, tri_col, tri_row, first, pairs_per_group,
                   finalize, d_inner, d_state, n_groups)

    @pl.when(c == pl.num_programs(1) - 1)
    def _():
        fin_ref[0] = st_ref[...]


def _ssd_chunk(rows, refs, causal, tri_col, tri_row, first, pairs_per_group, finalize, d_inner, d_state, n_groups):
    u_ref, dt_ref, dtt_ref, a_ref, at_ref, bias_ref, biast_ref, _ = refs[:8]
    pos = 8
    if finalize:
        z_ref, yb_ref, skip_ref, gn_ref = refs[pos:pos + 4]
        pos += 4
    y_ref, _, st_ref = refs[pos:pos + 3]
    u = u_ref[0, rows, :]
    xs = u[:, :d_inner]
    dt = _softplus(dt_ref[0, rows, :] + bias_ref[...])
    dtt = _softplus(dtt_ref[0, :, rows] + biast_ref[...])
    da = dt * a_ref[...]
    dat = dtt * at_ref[...]
    q_col = jnp.dot(tri_col, da, precision=HIGHEST, preferred_element_type=F32)
    q_row = jnp.dot(dat, tri_row, precision=HIGHEST, preferred_element_type=F32)
    total = jnp.sum(da, axis=0, keepdims=True)
    y_pairs = []
    for g in range(n_groups):
        b_g = u[:, d_inner + g * d_state:d_inner + (g + 1) * d_state]
        c_g = u[:, d_inner + (n_groups + g) * d_state:d_inner + (n_groups + g + 1) * d_state]
        c_bf = c_g.astype(BF16)
        cb = _dot_nt(c_bf, b_g.astype(BF16))
        b_t = b_g.T.astype(BF16)
        for pp in range(pairs_per_group):
            p = g * pairs_per_group + pp
            h0, h1 = 2 * p, 2 * p + 1
            x_pair = xs[:, p * LANES:(p + 1) * LANES]
            dt_pair = jnp.where(first, dt[:, h0:h0 + 1], dt[:, h1:h1 + 1])
            qc_pair = jnp.where(first, q_col[:, h0:h0 + 1], q_col[:, h1:h1 + 1])
            tot_pair = jnp.where(first[:1], total[:, h0:h0 + 1], total[:, h1:h1 + 1])
            xdt = x_pair * dt_pair
            xdt_bf = xdt.astype(BF16)
            diag = []
            for h in (h0, h1):
                diff = q_col[:, h:h + 1] - q_row[h:h + 1, :]
                lmat = jnp.exp(jnp.where(causal, diff, NEG_BIG))
                diag.append(_dot((cb * lmat).astype(BF16), xdt_bf))
            y_diag = jnp.where(first, diag[0], diag[1])
            state = st_ref[p]
            y_off = _dot(c_bf, state.astype(BF16)) * jnp.exp(qc_pair)
            y_pairs.append(y_diag + y_off)
            carry_in = (xdt * jnp.exp(tot_pair - qc_pair)).astype(BF16)
            st_ref[p] = jnp.exp(tot_pair) * state + _dot(b_t, carry_in)
    y = jnp.concatenate(y_pairs, axis=1)
    if finalize:
        y = y + yb_ref[0, rows, :] + skip_ref[...] * xs
        y = y * _silu(z_ref[0, rows, :])
        gw = d_inner // n_groups
        y = jnp.concatenate(
            [y[:, g * gw:(g + 1) * gw] * lax.rsqrt(
                jnp.mean(y[:, g * gw:(g + 1) * gw] ** 2, axis=1, keepdims=True) + NORM_EPS)
             for g in range(n_groups)], axis=1)
        y = y * gn_ref[...]
    y_ref[0, rows, :] = y.astype(y_ref.dtype)


def _ssd_scan(u, dt, dtt, a, dt_bias, init, *, reverse, d_inner, d_state, n_groups, final=None):
    b, length, ch = u.shape
    heads = dt.shape[2]
    step_rows = SSD_CHUNK * math.gcd(length // SSD_CHUNK, SSD_CHUNKS_PER_STEP)
    nc = length // step_rows
    pairs = heads // 2
    cidx = (lambda j: nc - 1 - j) if reverse else (lambda j: j)
    small = lambda shape: pl.BlockSpec(shape, lambda i, j: (0,) * len(shape))
    in_specs = [pl.BlockSpec((1, step_rows, ch), lambda i, j: (i, cidx(j), 0)),
                pl.BlockSpec((1, step_rows, heads), lambda i, j: (i, cidx(j), 0)),
                pl.BlockSpec((1, heads, step_rows), lambda i, j: (i, 0, cidx(j))),
                small((1, heads)), small((heads, 1)), small((1, heads)), small((heads, 1)),
                pl.BlockSpec((1, pairs, d_state, LANES), lambda i, j: (i, 0, 0, 0))]
    args = [u, dt, dtt, a.reshape(1, heads), a.reshape(heads, 1),
            dt_bias.reshape(1, heads).astype(F32), dt_bias.reshape(heads, 1).astype(F32), init]
    out_dtype = F32
    if final is not None:
        z, yb, d_skip, gnorm_g = final
        in_specs += [pl.BlockSpec((1, step_rows, d_inner), lambda i, j: (i, cidx(j), 0)),
                     pl.BlockSpec((1, step_rows, d_inner), lambda i, j: (i, cidx(j), 0)),
                     small((1, d_inner)), small((1, d_inner))]
        args += [z, yb, jnp.repeat(d_skip.astype(F32), HEAD_DIM).reshape(1, d_inner),
                 gnorm_g.reshape(1, d_inner).astype(F32)]
        out_dtype = BF16
    kern = functools.partial(_ssd_kernel, reverse=reverse, finalize=final is not None,
                             d_inner=d_inner, d_state=d_state, n_groups=n_groups)
    return pl.pallas_call(
        kern,
        out_shape=[jax.ShapeDtypeStruct((b, length, d_inner), out_dtype),
                   jax.ShapeDtypeStruct((b, pairs, d_state, LANES), F32)],
        grid=(b, nc), in_specs=in_specs,
        out_specs=[pl.BlockSpec((1, step_rows, d_inner), lambda i, j: (i, cidx(j), 0)),
                   pl.BlockSpec((1, pairs, d_state, LANES), lambda i, j: (i, 0, 0, 0))],
        scratch_shapes=[pltpu.VMEM((pairs, d_state, LANES), F32)],
        compiler_params=_params(("parallel", "arbitrary")), name="ssd_scan",
    )(*args)


def _outproj_kernel(oa_ref, ob_ref, w_ref, x_ref, gpost_ref, gate_ref, gpre_ref, sc_ref, sh_ref, wrh_ref, wrl_ref,
                    xo_ref, h_ref, aff_ref, *, row_chunk):
    half = oa_ref.shape[2]
    for r0 in range(0, x_ref.shape[1], row_chunk):
        rows = slice(r0, r0 + row_chunk)
        mix = _dot(oa_ref[0, rows, :], w_ref[:half, :]) + _dot(ob_ref[0, rows, :], w_ref[half:, :])
        x_new = x_ref[0, rows, :] + gate_ref[0] * _rms(mix, gpost_ref[...])
        xo_ref[0, rows, :] = x_new
        h = _rms(x_new, gpre_ref[...]) * sc_ref[0] + sh_ref[0]
        h_ref[0, rows, :] = h
        h_hi = h.astype(BF16)
        h_lo = (h - h_hi.astype(F32)).astype(BF16)
        logits = _dot_nt(wrh_ref[...], h_hi) + _dot_nt(wrh_ref[...], h_lo) + _dot_nt(wrl_ref[...], h_hi)
        e = jnp.exp(logits - jnp.max(logits, axis=0, keepdims=True))
        aff_ref[0, :, rows] = e / jnp.sum(e, axis=0, keepdims=True)


def _out_project(oa, ob, w_out, x, g_post, gate, g_pre, scale1p, shift, w_router):
    b, n, d = x.shape
    half = oa.shape[2]
    n_exp = w_router.shape[1]
    tm = min(OUT_PROJ_ROWS, n)
    bm = gate.shape[0]
    mod_map = (lambda i, j: (i, 0, 0)) if bm > 1 else (lambda i, j: (0, 0, 0))
    row = pl.BlockSpec((1, d), lambda i, j: (0, 0))
    mod = pl.BlockSpec((1, 1, d), mod_map)
    tile = pl.BlockSpec((1, tm, d), lambda i, j: (i, j, 0))
    act = pl.BlockSpec((1, tm, half), lambda i, j: (i, j, 0))
    wr_spec = pl.BlockSpec((n_exp, d), lambda i, j: (0, 0))
    wr = w_router.T.astype(F32)
    wr_hi = wr.astype(BF16)
    wr_lo = (wr - wr_hi.astype(F32)).astype(BF16)
    return pl.pallas_call(
        functools.partial(_outproj_kernel, row_chunk=min(ROW_TILE, tm)),
        out_shape=[jax.ShapeDtypeStruct((b, n, d), F32), jax.ShapeDtypeStruct((b, n, d), F32),
                   jax.ShapeDtypeStruct((b, n_exp, n), F32)],
        grid=(b, n // tm),
        in_specs=[act, act, pl.BlockSpec((2 * half, d), lambda i, j: (0, 0)), tile, row, mod, row, mod, mod,
                  wr_spec, wr_spec],
        out_specs=[tile, tile, pl.BlockSpec((1, n_exp, tm), lambda i, j: (i, 0, j))],
        compiler_params=_params(("parallel", "parallel")), name="out_proj_router",
    )(oa, ob, w_out, x, g_post.reshape(1, d), gate, g_pre.reshape(1, d), scale1p, shift, wr_hi, wr_lo)


def _topk_kernel(aff_ref, idx_ref, gate_ref, work_ref, key_ref, *, cap, slot_block):
    n_exp, n = aff_ref.shape[1], aff_ref.shape[2]
    n_blk = n // LANES
    aff = aff_ref[0]

    def search(i, thr):
        cand = thr | jnp.left_shift(jnp.int32(1), 30 - i)
        cnt = jnp.sum(jnp.where(aff_ref[0] >= pltpu.bitcast(cand, F32), 1.0, 0.0), axis=1, keepdims=True)
        return jnp.where(cnt >= cap, cand, thr)

    thr = lax.fori_loop(0, 31, search, jnp.zeros((n_exp, 1), I32))
    thr_f = pltpu.bitcast(thr, F32)
    above = aff > thr_f
    tied = (aff >= thr_f) & ~above
    need = cap - jnp.sum(jnp.where(above, 1.0, 0.0), axis=1, keepdims=True)

    r = lax.broadcasted_iota(I32, (LANES, LANES), 0)
    c = lax.broadcasted_iota(I32, (LANES, LANES), 1)
    incl = (r <= c).astype(BF16)

    def exclusive_prefix(mask_f32):
        starts = [jnp.zeros((n_exp, 1), F32)]
        for j in range(n_blk):
            blk = mask_f32[:, j * LANES:(j + 1) * LANES]
            run = _dot(blk.astype(BF16), incl)
            work_ref[:, j * LANES:(j + 1) * LANES] = run - blk + starts[-1]
            starts.append(starts[-1] + jnp.sum(blk, axis=1, keepdims=True))
        return starts

    exclusive_prefix(jnp.where(tied, 1.0, 0.0))
    sel = above | (tied & (work_ref[...] < need))
    sel_f = jnp.where(sel, 1.0, 0.0)
    gate_ref[0] = jnp.where(sel, aff, 0.0)
    starts = exclusive_prefix(sel_f)
    key = jnp.where(sel, work_ref[...], -1.0)
    for j in range(n_blk):
        key_ref[j] = key[:, j * LANES:(j + 1) * LANES]

    lane = lax.broadcasted_iota(I32, (slot_block, LANES), 1).astype(F32)
    slot = lax.broadcasted_iota(I32, (slot_block, LANES), 0).astype(F32)

    for sb in range(cap // slot_block):
        first = sum(jnp.where(starts[j + 1] <= sb * slot_block, 1.0, 0.0) for j in range(n_blk))
        last = sum(jnp.where(starts[j] < (sb + 1) * slot_block, 1.0, 0.0) for j in range(n_blk))
        want = slot + float(sb * slot_block)
        for e in range(n_exp):
            def per_block(j, acc, e=e, want=want):
                key_row = key_ref[j, e:e + 1, :]
                tok = lane + lax.convert_element_type(j * LANES, F32)
                return acc + jnp.where(key_row == want, tok, 0.0)

            acc = lax.fori_loop(first[e, 0].astype(I32), last[e, 0].astype(I32), per_block,
                                jnp.zeros((slot_block, LANES), F32))
            idx_ref[0, e, sb * slot_block:(sb + 1) * slot_block, :] = (
                jnp.sum(acc, axis=1, keepdims=True).astype(I32))


def _route(aff_t):
    b, n_exp, n = aff_t.shape
    cap = CAPACITY_FACTOR * n // n_exp
    slot_block = min(LANES, cap)
    kern = functools.partial(_topk_kernel, cap=cap, slot_block=slot_block)
    idx, gate = pl.pallas_call(
        kern,
        out_shape=[jax.ShapeDtypeStruct((b, n_exp, cap, 1), I32), jax.ShapeDtypeStruct((b, n_exp, n), F32)],
        grid=(b,),
        in_specs=[pl.BlockSpec((1, n_exp, n), lambda i: (i, 0, 0))],
        out_specs=[pl.BlockSpec((1, n_exp, cap, 1), lambda i: (i, 0, 0, 0)),
                   pl.BlockSpec((1, n_exp, n), lambda i: (i, 0, 0))],
        scratch_shapes=[pltpu.VMEM((n_exp, n), F32), pltpu.VMEM((n // LANES, n_exp, LANES), F32)],
        compiler_params=_params(("parallel",)), name="expert_choice_route",
    )(aff_t)
    return idx.reshape(b * n_exp, 1, cap), gate.reshape(b * n_exp, 1, n)


def _moe_kernel(idx_ref, nxt_ref, h_ref, wg_ref, wu_ref, wd_ref, y_ref, xg_ref, *, cap, prefetch):
    sub = lax.broadcasted_iota(I32, (SUBLANES, h_ref.shape[2]), 0)

    def gather_group(src_ref, dst, base):
        tile = jnp.zeros(sub.shape, F32)
        for u in range(SUBLANES):
            t = src_ref[0, 0, base + u]
            r = t & (SUBLANES - 1)
            rows = h_ref[0, pl.ds(pl.multiple_of(t - r, SUBLANES), SUBLANES), :]
            rolled = pltpu.roll(rows, (u - r) & (SUBLANES - 1), axis=0)
            tile = jnp.where(sub == u, rolled, tile)
        dst[pl.ds(base, SUBLANES), :] = tile

    def gather_loop(dst):
        def body(i8, carry):
            gather_group(idx_ref, dst, pl.multiple_of(i8 * SUBLANES, SUBLANES))
            return carry
        lax.fori_loop(0, cap // SUBLANES, body, 0)

    if prefetch:
        e = pl.program_id(1)
        slot = e & 1

        @pl.when(e == 0)
        def _():
            gather_loop(xg_ref.at[0])
    else:
        slot = 0
        gather_loop(xg_ref.at[0])
    xg = xg_ref[slot].astype(BF16)
    gate = _dot(xg, wg_ref[0])
    up = _dot(xg, wu_ref[0])
    hid = (_silu(gate) * up).astype(BF16)
    y_ref[0, 0] = _dot(hid, wd_ref[0])
    if prefetch:
        for i8 in range(cap // SUBLANES):
            gather_group(nxt_ref, xg_ref.at[1 - slot], i8 * SUBLANES)


def _combine_kernel(idx_ref, gate_ref, y_ref, acc_ref, *, cap):
    @pl.when(pl.program_id(1) == 0)
    def _():
        acc_ref[...] = jnp.zeros_like(acc_ref)

    sub = lax.broadcasted_iota(I32, (SUBLANES, acc_ref.shape[2]), 0)

    def scatter(i8, carry):
        base = pl.multiple_of(i8 * SUBLANES, SUBLANES)
        y_tile = y_ref[0, 0, pl.ds(base, SUBLANES), :]
        for u0 in range(0, SUBLANES, 4):
            pending = []
            for u in range(u0, u0 + 4):
                t = idx_ref[0, 0, base + u]
                g = gate_ref[0, 0, t]
                r = t & (SUBLANES - 1)
                dst = acc_ref.at[0, pl.ds(pl.multiple_of(t - r, SUBLANES), SUBLANES), :]
                rolled = pltpu.roll(y_tile, (r - u) & (SUBLANES - 1), axis=0)
                pending.append((dst, dst[...] + g * rolled, sub == r))
            for dst, new, mask in pending:
                pltpu.store(dst, new, mask=mask)
        return carry

    lax.fori_loop(0, cap // SUBLANES, scatter, 0)


def _expert_ffn(h, idx, gate, wg, wu, wd):
    b, n, d = h.shape
    n_exp, _, ff = wg.shape
    cap = idx.shape[2]
    smem = lambda width: pl.BlockSpec((1, 1, width), lambda i, e: (i * n_exp + e, 0, 0), memory_space=pltpu.SMEM)
    sample_major = n * d * h.dtype.itemsize >= 3 * d * ff * wg.dtype.itemsize
    if sample_major:
        grid = (b, n_exp)
        be = lambda i, e: (i, e)
        nxt = pl.BlockSpec((1, 1, cap), lambda i, e: (i * n_exp + jnp.minimum(e + 1, n_exp - 1), 0, 0),
                           memory_space=pltpu.SMEM)
        h_spec = pl.BlockSpec((1, n, d), lambda i, e: (i, 0, 0), pipeline_mode=pl.Buffered(1))
    else:
        grid = (n_exp, b)
        be = lambda e, i: (i, e)
        nxt = pl.BlockSpec((1, 1, cap), lambda e, i: (i * n_exp + e, 0, 0), memory_space=pltpu.SMEM)
        h_spec = pl.BlockSpec((1, n, d), lambda e, i: (i, 0, 0))
    y = pl.pallas_call(
        functools.partial(_moe_kernel, cap=cap, prefetch=sample_major),
        out_shape=jax.ShapeDtypeStruct((b, n_exp, cap, d), F32),
        grid=grid,
        in_specs=[pl.BlockSpec((1, 1, cap), lambda *g: (be(*g)[0] * n_exp + be(*g)[1], 0, 0),
                               memory_space=pltpu.SMEM),
                  nxt, h_spec,
                  pl.BlockSpec((1, d, ff), lambda *g: (be(*g)[1], 0, 0)),
                  pl.BlockSpec((1, d, ff), lambda *g: (be(*g)[1], 0, 0)),
                  pl.BlockSpec((1, ff, d), lambda *g: (be(*g)[1], 0, 0))],
        out_specs=pl.BlockSpec((1, 1, cap, d), lambda *g: (*be(*g), 0, 0)),
        scratch_shapes=[pltpu.VMEM((2, cap, d), F32)],
        compiler_params=_params(("parallel", "arbitrary")), name="expert_ffn",
    )(idx, idx, h, wg, wu, wd)
    return pl.pallas_call(
        functools.partial(_combine_kernel, cap=cap),
        out_shape=jax.ShapeDtypeStruct((b, n, d), F32),
        grid=(b, n_exp),
        in_specs=[smem(cap), smem(n), pl.BlockSpec((1, 1, cap, d), lambda i, e: (i, e, 0, 0))],
        out_specs=pl.BlockSpec((1, n, d), lambda i, e: (i, 0, 0)),
        compiler_params=_params(("parallel", "arbitrary")), name="expert_combine",
    )(idx, gate, y)


def _residual_kernel(x_ref, f_ref, g_ref, gate_ref, o_ref):
    o_ref[0] = x_ref[0] + gate_ref[0] * _rms(f_ref[0], g_ref[...])


def _gated_residual(x, f, g, gate):
    b, n, d = x.shape
    tm = min(ROW_TILE, n)
    bm = gate.shape[0]
    mod_map = (lambda i, j: (i, 0, 0)) if bm > 1 else (lambda i, j: (0, 0, 0))
    tile = pl.BlockSpec((1, tm, d), lambda i, j: (i, j, 0))
    return pl.pallas_call(
        _residual_kernel, out_shape=jax.ShapeDtypeStruct((b, n, d), F32),
        grid=(b, n // tm),
        in_specs=[tile, tile, pl.BlockSpec((1, d), lambda i, j: (0, 0)), pl.BlockSpec((1, 1, d), mod_map)],
        out_specs=tile,
        compiler_params=_params(("parallel", "parallel")), name="gated_residual",
    )(x, f, g.reshape(1, d), gate)


def _rope_tables(n):
    t = jnp.arange(n, dtype=I32)
    row = (t // GRID_W).astype(F32)
    col = (t % GRID_W).astype(F32)
    n_freq = HEAD_DIM // 4
    inv_freq = ROPE_THETA ** (-jnp.arange(n_freq, dtype=F32) / n_freq)
    ang = jnp.concatenate([row[:, None] * inv_freq, col[:, None] * inv_freq], axis=-1)
    cos, sin = jnp.cos(ang), jnp.sin(ang)
    cos_t = jnp.tile(jnp.concatenate([cos, cos], axis=-1), (1, LANES // HEAD_DIM))
    sin_t = jnp.tile(jnp.concatenate([-sin, sin], axis=-1), (1, LANES // HEAD_DIM))
    return cos_t, sin_t


def _deinterleave_heads(w):
    d, width = w.shape
    w = w.reshape(d, width // HEAD_DIM, HEAD_DIM // 2, 2)
    return jnp.concatenate([w[..., 0], w[..., 1]], axis=-1).reshape(d, width)


def _dup_heads(w):
    d, width = w.shape
    w = w.reshape(d, width // HEAD_DIM, 1, HEAD_DIM)
    return jnp.concatenate([w, w], axis=2).reshape(d, 2 * width)


def _mod_split(mod_rows):
    sh1, sc1, gt1, sh2, sc2, gt2 = jnp.split(mod_rows[:, None, :], 6, axis=-1)
    return sh1, 1.0 + sc1, gt1, sh2, 1.0 + sc2, gt2


def _moe_block(x_mid, h2, aff_t, g_post_ffn, gt2, wg, wu, wd):
    b, n, d = h2.shape
    n_exp, _, ff = wg.shape
    idx, gate = _route(aff_t)
    if b > 1 and n * d * h2.dtype.itemsize < 3 * d * ff * wg.dtype.itemsize:
        cap = idx.shape[2]
        idx = idx.reshape(b, n_exp, cap) + (jnp.arange(b, dtype=I32) * n)[:, None, None]
        idx = idx.transpose(1, 0, 2).reshape(n_exp, 1, b * cap)
        gate = gate.reshape(b, n_exp, n).transpose(1, 0, 2).reshape(n_exp, 1, b * n)
        ffn = _expert_ffn(h2.reshape(1, b * n, d), idx, gate, wg, wu, wd).reshape(b, n, d)
    else:
        ffn = _expert_ffn(h2, idx, gate, wg, wu, wd)
    return _gated_residual(x_mid, ffn, g_post_ffn, gt2)


def kernel(x, c, ctx, c_ctx, l0_w_mod, l0_b_mod, l0_g_pre_mix, l0_g_post_mix, l0_g_pre_ffn, l0_g_post_ffn, l0_w_in, l0_w_out, l0_lam_q1, l0_lam_k1, l0_lam_q2, l0_lam_k2, l0_subln_g, l0_rpb, l0_w_router, l0_w_gate, l0_w_up, l0_w_down, l1_w_mod, l1_b_mod, l1_g_pre_mix, l1_g_post_mix, l1_g_pre_ffn, l1_g_post_ffn, l1_w_in, l1_w_out, l1_sinks, l1_conv_w, l1_conv_b, l1_a_log_f, l1_a_log_b, l1_dt_bias_f, l1_dt_bias_b, l1_d_skip, l1_gnorm_g, l1_w_router, l1_w_gate, l1_w_up, l1_w_down):
    b, n, d = x.shape
    m = ctx.shape[1]
    half = d // 2
    scale = HEAD_DIM ** -0.5 * math.log2(math.e)
    cos_t, sin_t = _rope_tables(n)
    cond_rows = -(-(b + 1) // SUBLANES) * SUBLANES
    cond = jnp.zeros((cond_rows, d), F32).at[:b].set(c).at[b].set(c_ctx)

    mod = _modulation(cond, l0_w_mod, l0_b_mod)
    sh1, sc1, gt1, sh2, sc2, gt2 = _mod_split(mod[:b])
    csh1, csc1, cgt1, csh2, csc2, cgt2 = _mod_split(mod[b:b + 1])

    a_qk = half
    wq, wk, rest = l0_w_in[:, :a_qk], l0_w_in[:, a_qk:2 * a_qk], l0_w_in[:, 2 * a_qk:]
    qb_cols = slice(half, 2 * half)
    rest = rest.at[:, qb_cols].multiply(scale)
    w_in0 = jnp.concatenate([_deinterleave_heads(wq) * scale, _deinterleave_heads(wk), rest], axis=1).astype(BF16)
    segs0 = [(half, BF16)] * 6
    qa, ka, va, qb, kb, vb = _project(x, l0_g_pre_mix, sc1, sh1, w_in0, segs0, rope=(2 * a_qk, cos_t, sin_t))
    qa_c, ka_c, va_c, qb_c, kb_c, vb_c = _project(ctx, l0_g_pre_mix, csc1, csh1, w_in0, segs0)

    lam_init = 0.8 - 0.6 * math.exp(-0.3 * 0)
    lam = (jnp.exp(jnp.sum(l0_lam_q1.astype(F32) * l0_lam_k1.astype(F32)))
           - jnp.exp(jnp.sum(l0_lam_q2.astype(F32) * l0_lam_k2.astype(F32))) + lam_init)
    o_a = _diff_attention(qa, [(ka, va), (ka_c, va_c)], lam, l0_subln_g, 1.0 - lam_init)
    o_b = _neighbourhood_attention(qb, kb, vb, kb_c, vb_c, l0_rpb)
    oc_a = _diff_attention(qa_c, [(ka_c, va_c)], lam, l0_subln_g, 1.0 - lam_init)
    oc_b = _context_attention(qb_c, kb_c, vb_c)

    w_out0 = l0_w_out.astype(BF16)
    wg0, wu0, wd0 = l0_w_gate.astype(BF16), l0_w_up.astype(BF16), l0_w_down.astype(BF16)
    x_mid, h2, aff_t = _out_project(o_a, o_b, w_out0, x, l0_g_post_mix, gt1, l0_g_pre_ffn, sc2, sh2, l0_w_router)
    x1 = _moe_block(x_mid, h2, aff_t, l0_g_post_ffn, gt2, wg0, wu0, wd0)
    c_mid, hc2, aff_c = _out_project(oc_a, oc_b, w_out0, ctx, l0_g_post_mix, cgt1, l0_g_pre_ffn, csc2, csh2,
                                     l0_w_router)
    ctx1 = _moe_block(c_mid, hc2, aff_c, l0_g_post_ffn, cgt2, wg0, wu0, wd0)

    mod = _modulation(cond, l1_w_mod, l1_b_mod)
    sh1, sc1, gt1, sh2, sc2, gt2 = _mod_split(mod[:b])
    csh1, csc1, _, _, _, _ = _mod_split(mod[b:b + 1])

    c_q = half
    d_inner = half
    d_heads = l1_a_log_f.shape[0]
    d_xbc = l1_conv_w.shape[2]
    c_kv = (l1_w_in.shape[1] - c_q - d_xbc - d_inner - 2 * d_heads) // 2
    n_groups = 2
    d_state = (d_xbc - d_inner) // (2 * n_groups)
    wq = l1_w_in[:, :c_q]
    wk = l1_w_in[:, c_q:c_q + c_kv]
    wv = l1_w_in[:, c_q + c_kv:c_q + 2 * c_kv]
    o_x = c_q + 2 * c_kv
    w_x = l1_w_in[:, o_x:o_x + d_xbc + d_inner]
    w_dt = l1_w_in[:, o_x + d_xbc + d_inner:]
    w_kv = jnp.concatenate([_dup_heads(_deinterleave_heads(wk)), _dup_heads(wv), w_x, w_dt], axis=1)
    w_lat = jnp.concatenate([_deinterleave_heads(wq) * scale, w_kv], axis=1).astype(BF16)
    w_ctx = w_kv.astype(BF16)
    w_dt_t = w_dt.T.astype(BF16)
    segs_lat = [(c_q, BF16), (2 * c_kv, BF16), (2 * c_kv, BF16), (d_xbc, F32), (d_inner, F32), (2 * d_heads, F32)]
    q, k2, v2, xbc, z, dt, dtt = _project(x1, l1_g_pre_mix, sc1, sh1, w_lat, segs_lat,
                                          rope=(c_q + 2 * c_kv, cos_t, sin_t), wt=w_dt_t)
    k2_c, v2_c, xbc_c, _, dt_c, dtt_c = _project(ctx1, l1_g_pre_mix, csc1, csh1, w_ctx, segs_lat[1:], wt=w_dt_t)

    o_c = _window_attention(q, k2, v2, k2_c, v2_c, l1_sinks.astype(F32) * math.log2(math.e))

    a_f = -jnp.exp(l1_a_log_f.astype(F32))
    a_b = -jnp.exp(l1_a_log_b.astype(F32))
    u = _conv_silu(xbc, l1_conv_w, l1_conv_b)
    u_c = _conv_silu(xbc_c, l1_conv_w, l1_conv_b)
    ssd = functools.partial(_ssd_scan, d_inner=d_inner, d_state=d_state, n_groups=n_groups)
    zero_state = jnp.zeros((b, d_heads // 2, d_state, LANES), F32)
    _, sc_f = ssd(u_c, dt_c[..., :d_heads], dtt_c[:, :d_heads], a_f, l1_dt_bias_f, zero_state, reverse=False)
    _, sc_b = ssd(u_c, dt_c[..., d_heads:], dtt_c[:, d_heads:], a_b, l1_dt_bias_b, zero_state, reverse=True)
    y_b, _ = ssd(u, dt[..., d_heads:], dtt[:, d_heads:], a_b, l1_dt_bias_b, sc_b, reverse=True)
    o_d, _ = ssd(u, dt[..., :d_heads], dtt[:, :d_heads], a_f, l1_dt_bias_f, sc_f, reverse=False,
                 final=(z, y_b, l1_d_skip, l1_gnorm_g))

    x_mid, h2, aff_t = _out_project(o_c, o_d, l1_w_out.astype(BF16), x1, l1_g_post_mix, gt1, l1_g_pre_ffn, sc2, sh2,
                                    l1_w_router)
    return _moe_block(x_mid, h2, aff_t, l1_g_post_ffn, gt2, l1_w_gate.astype(BF16), l1_w_up.astype(BF16),
                      l1_w_down.astype(BF16))
```

```python
import functools
import math

import jax
import jax.numpy as jnp
from jax import lax
from jax.experimental import pallas as pl
from jax.experimental.pallas import tpu as pltpu

F32 = jnp.float32
BF16 = jnp.bfloat16
I32 = jnp.int32

HEAD_DIM = 64
LANES = 128
SUBLANES = 8
GRID_W = 64
ROPE_THETA = 10000.0
NORM_EPS = 1e-6
NA_ROWS = 8
NA_COLS = 16
WINDOW = 128
Q_BLOCK = 128
SSD_CHUNK = 128
D_CONV = 5
CAPACITY_FACTOR = 2
NEG_BIG = -1e30
VMEM_LIMIT = 56 * 1024 * 1024

ROW_TILE = 512
COL_CHUNK = 512
MOD_COLS = 1024
OUT_PROJ_ROWS = 1024
ATTN_Q_TILE = 512
ATTN_KEY_CHUNK = 512
NA_BLOCKS_PER_STEP = 4
SWA_BLOCKS_PER_STEP = 8
SSD_CHUNKS_PER_STEP = 4
HIGHEST = lax.Precision.HIGHEST
NT_DIMS = (((1,), (1,)), ((), ()))


def _params(sem, vmem=VMEM_LIMIT):
    return pltpu.CompilerParams(dimension_semantics=sem, vmem_limit_bytes=vmem)


def _rms(u, g):
    return u * lax.rsqrt(jnp.mean(u * u, axis=-1, keepdims=True) + NORM_EPS) * g


def _silu(u):
    return u * (1.0 / (1.0 + jnp.exp(-u)))


def _softplus(u):
    return jnp.maximum(u, 0.0) + jnp.log(1.0 + jnp.exp(-jnp.abs(u)))


def _dot(a, b):
    return jnp.dot(a, b, preferred_element_type=F32)


def _dot_nt(a, b):
    return lax.dot_general(a, b, NT_DIMS, preferred_element_type=F32)


def _half_masks(shape):
    lane = lax.broadcasted_iota(I32, shape, 1)
    return (lane % LANES) < HEAD_DIM


def _mod_kernel(c_ref, w_ref, b_ref, o_ref):
    s = _silu(c_ref[...]).astype(BF16)
    o_ref[...] = _dot(s, w_ref[...].astype(BF16)) + b_ref[...]


def _modulation(cond, w_mod, b_mod):
    rows, d = cond.shape
    n_out = w_mod.shape[1]
    tn = MOD_COLS
    return pl.pallas_call(
        _mod_kernel,
        out_shape=jax.ShapeDtypeStruct((rows, n_out), F32),
        grid=(n_out // tn,),
        in_specs=[pl.BlockSpec((rows, d), lambda j: (0, 0)),
                  pl.BlockSpec((d, tn), lambda j: (0, j)),
                  pl.BlockSpec((1, tn), lambda j: (0, j))],
        out_specs=pl.BlockSpec((rows, tn), lambda j: (0, j)),
        compiler_params=_params(("parallel",)),
        name="modulation",
    )(cond, w_mod, b_mod.reshape(1, n_out))


def _rope_block(u, cos, sin_signed, first_half):
    width = u.shape[1]
    fwd = pltpu.roll(u, HEAD_DIM // 2, axis=1)
    bwd = pltpu.roll(u, width - HEAD_DIM // 2, axis=1)
    return u * cos + jnp.where(first_half, bwd, fwd) * sin_signed


def _proj_kernel(*refs, seg_widths, rope_cols, has_t, col_chunk):
    x_ref, g_ref, sc_ref, sh_ref, w_ref = refs[:5]
    pos = 5
    if rope_cols:
        cos_ref, sin_ref = refs[pos:pos + 2]
        pos += 2
    if has_t:
        wt_ref = refs[pos]
        pos += 1
    out_refs = refs[pos:]
    h = (_rms(x_ref[0], g_ref[...]) * sc_ref[0] + sh_ref[0]).astype(BF16)
    if rope_cols:
        cos = cos_ref[...]
        sin_signed = sin_ref[...]
        first_half = (lax.broadcasted_iota(I32, cos.shape, 1) % HEAD_DIM) < HEAD_DIM // 2
    col = 0
    for o_ref, width in zip(out_refs, seg_widths):
        for c0 in range(0, width, col_chunk):
            cw = min(col_chunk, width - c0)
            acc = _dot(h, w_ref[:, col + c0:col + c0 + cw])
            if col + c0 < rope_cols:
                acc = jnp.concatenate(
                    [_rope_block(acc[:, k:k + LANES], cos, sin_signed, first_half)
                     for k in range(0, cw, LANES)], axis=1)
            o_ref[0, :, c0:c0 + cw] = acc.astype(o_ref.dtype)
        col += width
    if has_t:
        out_refs[-1][0] = _dot_nt(wt_ref[...], h)


def _project(x, g, scale1p, shift, w, segs, rope=None, wt=None):
    b, n, d = x.shape
    tm = min(ROW_TILE, n)
    bm = scale1p.shape[0]
    mod_map = (lambda i, j: (i, 0, 0)) if bm > 1 else (lambda i, j: (0, 0, 0))
    in_specs = [pl.BlockSpec((1, tm, d), lambda i, j: (i, j, 0)),
                pl.BlockSpec((1, d), lambda i, j: (0, 0)),
                pl.BlockSpec((1, 1, d), mod_map),
                pl.BlockSpec((1, 1, d), mod_map),
                pl.BlockSpec(w.shape, lambda i, j: (0, 0))]
    args = [x, g.reshape(1, d), scale1p, shift, w]
    rope_cols = 0
    if rope is not None:
        rope_cols, cos, sin_signed = rope
        in_specs += [pl.BlockSpec((tm, LANES), lambda i, j: (j, 0))] * 2
        args += [cos, sin_signed]
    if wt is not None:
        in_specs.append(pl.BlockSpec(wt.shape, lambda i, j: (0, 0)))
        args.append(wt)
    out_shape = [jax.ShapeDtypeStruct((b, n, wd), dt) for wd, dt in segs]
    out_specs = [pl.BlockSpec((1, tm, wd), lambda i, j: (i, j, 0)) for wd, _ in segs]
    if wt is not None:
        out_shape.append(jax.ShapeDtypeStruct((b, wt.shape[0], n), F32))
        out_specs.append(pl.BlockSpec((1, wt.shape[0], tm), lambda i, j: (i, 0, j)))
    kern = functools.partial(_proj_kernel, seg_widths=tuple(wd for wd, _ in segs),
                             rope_cols=rope_cols, has_t=wt is not None, col_chunk=COL_CHUNK)
    return pl.pallas_call(
        kern, out_shape=out_shape, grid=(b, n // tm), in_specs=in_specs, out_specs=out_specs,
        compiler_params=_params(("parallel", "parallel")), name="in_proj",
    )(*args)


def _diff_kernel(*refs, seg_lens, tk, out_scale, heads, n_cast):
    n_in = 3 + 2 * len(seg_lens)
    in_refs, cast_in = refs[:n_in], refs[n_in:n_in + n_cast]
    o_ref = refs[n_in + n_cast]
    cast_out = refs[n_in + n_cast + 1:n_in + 2 * n_cast + 1]
    m_all, acc_all, s_all = refs[n_in + 2 * n_cast + 1:]
    for src, dst in zip(cast_in, cast_out):
        dst[...] = src[...].astype(dst.dtype)
    for h in range(heads):
        _diff_head((*in_refs, o_ref), m_all.at[h], acc_all.at[h], s_all.at[2 * (h % 2)],
                   s_all.at[2 * (h % 2) + 1], slice(h * LANES, (h + 1) * LANES), seg_lens, tk, out_scale)


def _diff_head(io_refs, m_ref, acc_ref, s0_ref, s1_ref, cols, seg_lens, tk, out_scale):
    lam_ref, q_ref, g_ref = io_refs[:3]
    n_seg = len(seg_lens)
    kv_refs = [r.at[:, :, cols] for r in io_refs[3:3 + 2 * n_seg]]
    o_ref = io_refs[3 + 2 * n_seg]
    qf = q_ref[0, :, cols].astype(F32)
    first = _half_masks(qf.shape)
    qs = (jnp.where(first, qf, 0.0).astype(BF16), jnp.where(first, 0.0, qf).astype(BF16))
    m_ref[...] = jnp.full(m_ref.shape, NEG_BIG, F32)
    acc_ref[...] = jnp.zeros(acc_ref.shape, F32)

    def scores(kc):
        return [_dot_nt(qs[t], kc) for t in range(2)]

    def absorb(sc, vc):
        v_ones = jnp.concatenate([vc, jnp.ones(vc.shape, BF16)], axis=1)
        for t in range(2):
            m_old = m_ref[t]
            m_new = jnp.maximum(m_old, jnp.max(sc[t], axis=1, keepdims=True))
            p = jnp.exp2(sc[t] - m_new).astype(BF16)
            acc_ref[t] = jnp.exp2(m_old - m_new) * acc_ref[t] + _dot(p, v_ones)
            m_ref[t] = m_new

    def put(buf, sc):
        buf[0] = sc[0]
        buf[1] = sc[1]

    main = max(range(n_seg), key=lambda i: seg_lens[i])
    n_chunks = seg_lens[main] // tk if seg_lens[main] >= 2 * tk else 1
    if n_chunks > 1:
        k_ref, v_ref = kv_refs[2 * main], kv_refs[2 * main + 1]
        rows = lambda ref, c: ref[0, pl.ds(pl.multiple_of(c * tk, tk), tk), :]
    singles = [s_i for s_i in range(n_seg) if s_i != main or n_chunks == 1]
    single_scores = [scores(kv_refs[2 * s_i][0]) for s_i in singles]
    if n_chunks > 1:
        put(s0_ref, scores(rows(k_ref, 0)))
    for s_i, sc in zip(singles, single_scores):
        absorb(sc, kv_refs[2 * s_i + 1][0])
    if n_chunks > 1:
        def pair(c, last):
            put(s1_ref, scores(rows(k_ref, c + 1)))
            absorb([s0_ref[0], s0_ref[1]], rows(v_ref, c))
            if not last:
                put(s0_ref, scores(rows(k_ref, c + 2)))
            absorb([s1_ref[0], s1_ref[1]], rows(v_ref, c + 1))

        def body(j, carry):
            pair(2 * j, False)
            return carry

        lax.fori_loop(0, n_chunks // 2 - 1, body, 0)
        pair(n_chunks - 2, True)
    ratio0 = acc_ref[0, :, :LANES] / acc_ref[0, :, LANES:]
    ratio1 = acc_ref[1, :, :LANES] / acc_ref[1, :, LANES:]
    o = ratio0 - lam_ref[0] * ratio1
    o_ref[0, :, cols] = (_rms(o, g_ref[...]) * out_scale).astype(o_ref.dtype)


def _diff_attention(q, kv_segs, lam, subln_g, out_scale, cast=()):
    b, nq, width = q.shape
    heads = width // LANES
    tq = min(ATTN_Q_TILE, nq)
    in_specs = [pl.BlockSpec(memory_space=pltpu.SMEM),
                pl.BlockSpec((1, tq, width), lambda i, j: (i, j, 0)),
                pl.BlockSpec((1, LANES), lambda i, j: (0, 0))]
    args = [lam.reshape(1).astype(F32), q, subln_g.reshape(1, LANES).astype(F32)]
    seg_lens = []
    for k, v in kv_segs:
        length = k.shape[1]
        seg_lens.append(length)
        in_specs += [pl.BlockSpec((1, length, width), lambda i, j: (i, 0, 0))] * 2
        args += [k, v]
    tk = ATTN_KEY_CHUNK
    longest = max(seg_lens)
    assert longest < 2 * tk or longest % (2 * tk) == 0, (longest, tk)
    steps = b * (nq // tq)
    out_shape = [jax.ShapeDtypeStruct((b, nq, width), BF16)]
    out_specs = [pl.BlockSpec((1, tq, width), lambda i, j: (i, j, 0))]
    for w in cast:
        flat = w.reshape(-1, w.shape[-1])
        slab = flat.shape[0] // steps
        assert flat.shape[0] == slab * steps and slab % (2 * SUBLANES) == 0, (flat.shape, steps)
        spec = pl.BlockSpec((slab, flat.shape[1]), lambda i, j: (i * (nq // tq) + j, 0))
        in_specs.append(spec)
        args.append(flat)
        out_shape.append(jax.ShapeDtypeStruct(flat.shape, BF16))
        out_specs.append(spec)
    kern = functools.partial(_diff_kernel, seg_lens=tuple(seg_lens), tk=tk, out_scale=out_scale, heads=heads,
                             n_cast=len(cast))
    outs = pl.pallas_call(
        kern, out_shape=out_shape, grid=(b, nq // tq), in_specs=in_specs, out_specs=out_specs,
        scratch_shapes=[pltpu.VMEM((heads, 2, tq, 1), F32), pltpu.VMEM((heads, 2, tq, 2 * LANES), F32),
                        pltpu.VMEM((4, 2, tq, tk), F32)],
        compiler_params=_params(("parallel", "parallel")), name="diff_attention",
    )(*args)
    return outs[0], [o.reshape(w.shape) for o, w in zip(outs[1:], cast)]


NA_BLOCK_ROWS = 4
NA_WIN_ROWS = NA_BLOCK_ROWS + NA_ROWS


def _na_window_start(blk, rows):
    return jnp.clip(blk * NA_BLOCK_ROWS - NA_ROWS // 2, 0, rows - NA_WIN_ROWS)


def _natten_kernel(q_ref, k_ref, v_ref, kc_ref, vc_ref, bias_ref, o_ref, *, rows):
    nq = NA_BLOCK_ROWS * GRID_W
    per_step = q_ref.shape[1] // nq
    for sb in range(per_step):
        _natten_block(q_ref, k_ref, v_ref, kc_ref, vc_ref, bias_ref, o_ref, pl.program_id(1) * per_step + sb,
                      slice(sb * nq, (sb + 1) * nq), rows)


def _natten_block(q_ref, k_ref, v_ref, kc_ref, vc_ref, bias_ref, o_ref, blk, qrows, rows):
    start = pl.multiple_of(_na_window_start(blk, rows) * GRID_W, GRID_W)
    n_keys = NA_WIN_ROWS * GRID_W
    nq = NA_BLOCK_ROWS * GRID_W
    case = jnp.where(blk == 0, 0, jnp.where(blk == rows // NA_BLOCK_ROWS - 1, 2, 1))
    q = q_ref[0, qrows, :].astype(F32)
    kwin = k_ref[0, pl.ds(start, n_keys), :]
    vwin = v_ref[0, pl.ds(start, n_keys), :]
    kc = kc_ref[0]
    vc = vc_ref[0]
    first = _half_masks((nq, LANES))
    for p in range(q.shape[1] // LANES):
        cols = slice(p * LANES, (p + 1) * LANES)
        qs = jnp.concatenate([jnp.where(first, q[:, cols], 0.0), jnp.where(first, 0.0, q[:, cols])],
                             axis=0).astype(BF16)
        s_nb = _dot_nt(qs, kwin[:, cols]) + jnp.concatenate([bias_ref[case, 2 * p], bias_ref[case, 2 * p + 1]],
                                                            axis=0)
        s_cx = _dot_nt(qs, kc[:, cols])
        m = jnp.maximum(jnp.max(s_nb, axis=1, keepdims=True), jnp.max(s_cx, axis=1, keepdims=True))
        p_nb = jnp.exp2(s_nb - m).astype(BF16)
        p_cx = jnp.exp2(s_cx - m).astype(BF16)
        v_nb = jnp.concatenate([vwin[:, cols], jnp.ones((n_keys, LANES), BF16)], axis=1)
        v_cx = jnp.concatenate([vc[:, cols], jnp.ones((vc.shape[0], LANES), BF16)], axis=1)
        acc = _dot(p_nb, v_nb) + _dot(p_cx, v_cx)
        ratio = acc[:, :LANES] / acc[:, LANES:]
        o_ref[0, qrows, cols] = jnp.where(first, ratio[:nq], ratio[nq:]).astype(o_ref.dtype)


def _na_bias_table(rpb, rows):
    heads = rpb.shape[0]
    w = jnp.arange(GRID_W)
    kc = jnp.arange(GRID_W)
    col_start = jnp.clip(w - NA_COLS // 2, 0, GRID_W - NA_COLS)
    valid = (kc[None, :] >= col_start[:, None]) & (kc[None, :] < col_start[:, None] + NA_COLS)
    pad = GRID_W - NA_COLS
    padded = jnp.pad(rpb.astype(F32) * math.log2(math.e), ((0, 0), (0, 0), (pad, pad)))
    cols = jnp.stack([padded[:, :, GRID_W - 1 - wi:2 * GRID_W - 1 - wi] for wi in range(GRID_W)], axis=2)
    cols = jnp.where(valid[None, None], cols, NEG_BIG)
    masked = jnp.full((heads, GRID_W, GRID_W), NEG_BIG, F32)
    cases = []
    for first_row in (0, NA_BLOCK_ROWS, rows - NA_BLOCK_ROWS):
        win0 = min(max(first_row - NA_ROWS // 2, 0), rows - NA_WIN_ROWS)
        per_q = []
        for qr in range(NA_BLOCK_ROWS):
            r = first_row + qr
            rs = min(max(r - NA_ROWS // 2, 0), rows - NA_ROWS)
            per_k = [cols[:, win0 + kr - r + NA_ROWS - 1] if rs <= win0 + kr < rs + NA_ROWS else masked
                     for kr in range(NA_WIN_ROWS)]
            per_q.append(jnp.stack(per_k, axis=2))
        cases.append(jnp.stack(per_q, axis=1))
    return jnp.stack(cases).reshape(3, heads, NA_BLOCK_ROWS * GRID_W, NA_WIN_ROWS * GRID_W)


def _neighbourhood_attention(q, k, v, kc, vc, rpb):
    b, n, width = q.shape
    m = kc.shape[1]
    rows = n // GRID_W
    n_blk = rows // NA_BLOCK_ROWS
    assert rows % NA_BLOCK_ROWS == 0 and rows >= NA_WIN_ROWS + NA_BLOCK_ROWS
    bias = _na_bias_table(rpb, rows)
    heads = rpb.shape[0]
    step_q = NA_BLOCK_ROWS * GRID_W * math.gcd(n_blk, NA_BLOCKS_PER_STEP)
    kern = functools.partial(_natten_kernel, rows=rows)
    return pl.pallas_call(
        kern, out_shape=jax.ShapeDtypeStruct((b, n, width), BF16),
        grid=(b, n // step_q),
        in_specs=[pl.BlockSpec((1, step_q, width), lambda i, r: (i, r, 0)),
                  pl.BlockSpec((1, n, width), lambda i, r: (i, 0, 0)),
                  pl.BlockSpec((1, n, width), lambda i, r: (i, 0, 0)),
                  pl.BlockSpec((1, m, width), lambda i, r: (i, 0, 0)),
                  pl.BlockSpec((1, m, width), lambda i, r: (i, 0, 0)),
                  pl.BlockSpec(bias.shape, lambda i, r: (0, 0, 0, 0), pipeline_mode=pl.Buffered(1))],
        out_specs=pl.BlockSpec((1, step_q, width), lambda i, r: (i, r, 0)),
        compiler_params=_params(("parallel", "parallel")), name="neighbourhood_attention",
    )(q, k, v, kc, vc, bias)


def _ctx_attn_kernel(q_ref, k_ref, v_ref, o_ref):
    q = q_ref[0].astype(F32)
    k = k_ref[0]
    v = v_ref[0]
    first = _half_masks((q.shape[0], LANES))
    for p in range(q.shape[1] // LANES):
        cols = slice(p * LANES, (p + 1) * LANES)
        outs = []
        for sub in range(2):
            qm = jnp.where(first if sub == 0 else ~first, q[:, cols], 0.0).astype(BF16)
            s = _dot_nt(qm, k[:, cols])
            e = jnp.exp2(s - jnp.max(s, axis=1, keepdims=True))
            outs.append(_dot(e.astype(BF16), v[:, cols]) / jnp.sum(e, axis=1, keepdims=True))
        o_ref[0, :, cols] = jnp.where(first, outs[0], outs[1]).astype(o_ref.dtype)


def _context_attention(q, k, v):
    b, m, width = q.shape
    spec = pl.BlockSpec((1, m, width), lambda i: (i, 0, 0))
    return pl.pallas_call(
        _ctx_attn_kernel, out_shape=jax.ShapeDtypeStruct((b, m, width), BF16),
        grid=(b,), in_specs=[spec, spec, spec], out_specs=spec,
        compiler_params=_params(("parallel",)), name="context_attention",
    )(q, k, v)


def _swa_kernel(sink_ref, q_ref, k_ref, v_ref, kc_ref, vc_ref, o_ref, *, n_blocks):
    for sb in range(q_ref.shape[1] // Q_BLOCK):
        _swa_block(sink_ref, q_ref, k_ref, v_ref, kc_ref, vc_ref, o_ref, pl.program_id(1) * (q_ref.shape[1] // Q_BLOCK) + sb,
                   slice(sb * Q_BLOCK, (sb + 1) * Q_BLOCK), n_blocks)


def _swa_block(sink_ref, q_ref, k_ref, v_ref, kc_ref, vc_ref, o_ref, blk, qrows, n_blocks):
    span = 3 * Q_BLOCK
    start = pl.multiple_of(jnp.clip(blk - 1, 0, n_blocks - 3) * Q_BLOCK, Q_BLOCK)
    q = q_ref[0, qrows, :].astype(F32)
    kwin = k_ref[0, pl.ds(start, span), :]
    vwin = v_ref[0, pl.ds(start, span), :]
    kc = kc_ref[0]
    vc = vc_ref[0]
    first = _half_masks((Q_BLOCK, LANES))
    pairs = q.shape[1] // LANES
    n_groups = k_ref.shape[2] // LANES
    pairs_per_group = pairs // n_groups
    stack = 2 * pairs_per_group
    row = lax.broadcasted_iota(I32, (stack * Q_BLOCK, span), 0)
    qpos = blk * Q_BLOCK + (row & (Q_BLOCK - 1))
    kpos = start + lax.broadcasted_iota(I32, (stack * Q_BLOCK, span), 1)
    valid = jnp.abs(qpos - kpos) <= WINDOW
    head_of_row = lax.broadcasted_iota(I32, (stack * Q_BLOCK, 1), 0) >> (Q_BLOCK.bit_length() - 1)
    for g in range(n_groups):
        gcols = slice(g * LANES, (g + 1) * LANES)
        parts = []
        for p in range(g * pairs_per_group, (g + 1) * pairs_per_group):
            q_pair = q[:, p * LANES:(p + 1) * LANES]
            parts += [jnp.where(first, q_pair, 0.0), jnp.where(first, 0.0, q_pair)]
        qs = jnp.concatenate(parts, axis=0).astype(BF16)
        sink = jnp.zeros((stack * Q_BLOCK, 1), F32)
        for i in range(stack):
            sink = jnp.where(head_of_row == i, sink_ref[g * stack + i], sink)
        s_loc = jnp.where(valid, _dot_nt(qs, kwin[:, gcols]), NEG_BIG)
        s_cx = _dot_nt(qs, kc[:, gcols])
        m = jnp.maximum(jnp.maximum(jnp.max(s_loc, axis=1, keepdims=True),
                                    jnp.max(s_cx, axis=1, keepdims=True)), sink)
        p_loc = jnp.exp2(s_loc - m).astype(BF16)
        p_cx = jnp.exp2(s_cx - m).astype(BF16)
        v_loc = jnp.concatenate([vwin[:, gcols], jnp.ones((span, LANES), BF16)], axis=1)
        v_cx = jnp.concatenate([vc[:, gcols], jnp.ones((vc.shape[0], LANES), BF16)], axis=1)
        acc = _dot(p_loc, v_loc) + _dot(p_cx, v_cx)
        ratio = acc[:, :LANES] / (acc[:, LANES:] + jnp.exp2(sink - m))
        for i in range(pairs_per_group):
            p = g * pairs_per_group + i
            lo = ratio[2 * i * Q_BLOCK:(2 * i + 1) * Q_BLOCK]
            hi = ratio[(2 * i + 1) * Q_BLOCK:(2 * i + 2) * Q_BLOCK]
            o_ref[0, qrows, p * LANES:(p + 1) * LANES] = jnp.where(first, lo, hi).astype(o_ref.dtype)


def _window_attention(q, k2, v2, kc2, vc2, sinks):
    b, n, width = q.shape
    m = kc2.shape[1]
    kvw = k2.shape[2]
    n_blocks = n // Q_BLOCK
    step_rows = Q_BLOCK * math.gcd(n_blocks, SWA_BLOCKS_PER_STEP)
    kern = functools.partial(_swa_kernel, n_blocks=n_blocks)
    return pl.pallas_call(
        kern, out_shape=jax.ShapeDtypeStruct((b, n, width), BF16),
        grid=(b, n // step_rows),
        in_specs=[pl.BlockSpec(memory_space=pltpu.SMEM),
                  pl.BlockSpec((1, step_rows, width), lambda i, j: (i, j, 0)),
                  pl.BlockSpec((1, n, kvw), lambda i, j: (i, 0, 0)),
                  pl.BlockSpec((1, n, kvw), lambda i, j: (i, 0, 0)),
                  pl.BlockSpec((1, m, kvw), lambda i, j: (i, 0, 0)),
                  pl.BlockSpec((1, m, kvw), lambda i, j: (i, 0, 0))],
        out_specs=pl.BlockSpec((1, step_rows, width), lambda i, j: (i, j, 0)),
        compiler_params=_params(("parallel", "parallel")), name="window_attention",
    )(sinks.astype(F32), q, k2, v2, kc2, vc2)


def _conv_kernel(x_ref, prev_ref, next_ref, w_ref, b_ref, o_ref):
    i = pl.program_id(1)
    tm = x_ref.shape[1]
    keep_prev = jnp.where(i > 0, 1.0, 0.0)
    keep_next = jnp.where(i < pl.num_programs(1) - 1, 1.0, 0.0)
    xe = jnp.concatenate([prev_ref[0] * keep_prev, x_ref[0], next_ref[0] * keep_next], axis=0)
    acc = jnp.zeros((tm, x_ref.shape[2]), F32) + b_ref[...]
    for k in range(D_CONV):
        off = 8 - D_CONV // 2 + k
        acc = acc + w_ref[k:k + 1, :] * xe[off:off + tm, :]
    o_ref[0] = _silu(acc)


def _conv_silu(x, conv_w, conv_b):
    b, length, ch = x.shape
    tm = min(ROW_TILE, length)
    per = tm // 8
    last = length // 8 - 1
    return pl.pallas_call(
        _conv_kernel, out_shape=jax.ShapeDtypeStruct((b, length, ch), F32),
        grid=(b, length // tm),
        in_specs=[pl.BlockSpec((1, tm, ch), lambda i, j: (i, j, 0)),
                  pl.BlockSpec((1, 8, ch), lambda i, j: (i, jnp.maximum(j * per - 1, 0), 0)),
                  pl.BlockSpec((1, 8, ch), lambda i, j: (i, jnp.minimum((j + 1) * per, last), 0)),
                  pl.BlockSpec((D_CONV, ch), lambda i, j: (0, 0)),
                  pl.BlockSpec((1, ch), lambda i, j: (0, 0))],
        out_specs=pl.BlockSpec((1, tm, ch), lambda i, j: (i, j, 0)),
        compiler_params=_params(("parallel", "parallel")), name="conv_silu",
    )(x, x, x, conv_w.reshape(D_CONV, ch).astype(F32), conv_b.reshape(1, ch).astype(F32))


def _ssd_kernel(*refs, reverse, finalize, d_inner, d_state, n_groups):
    u_ref, dt_ref, dtt_ref, a_ref, at_ref, bias_ref, biast_ref, init_ref = refs[:8]
    pos = 8
    if finalize:
        z_ref, yb_ref, skip_ref, gn_ref = refs[pos:pos + 4]
        pos += 4
    y_ref, fin_ref, st_ref = refs[pos:pos + 3]
    c = pl.program_id(1)

    @pl.when(c == 0)
    def _():
        st_ref[...] = init_ref[0]

    q_len = SSD_CHUNK
    row = lax.broadcasted_iota(I32, (q_len, q_len), 0)
    col = lax.broadcasted_iota(I32, (q_len, q_len), 1)
    lower = row >= col
    upper = row <= col
    causal = upper if reverse else lower
    tri_col = causal.astype(F32)
    tri_row = (lower if reverse else upper).astype(F32)
    first = _half_masks((q_len, LANES))
    heads_per_group = (d_inner // HEAD_DIM) // n_groups
    pairs_per_group = heads_per_group // 2
    n_sub = u_ref.shape[1] // q_len
    for sc in (reversed(range(n_sub)) if reverse else range(n_sub)):
        _ssd_chunk(slice(sc * q_len, (sc + 1) * q_len), refs, causal, tri_col, tri_row, first, pairs_per_group,
                   finalize, d_inner, d_state, n_groups)

    @pl.when(c == pl.num_programs(1) - 1)
    def _():
        fin_ref[0] = st_ref[...]


def _ssd_chunk(rows, refs, causal, tri_col, tri_row, first, pairs_per_group, finalize, d_inner, d_state, n_groups):
    u_ref, dt_ref, dtt_ref, a_ref, at_ref, bias_ref, biast_ref, _ = refs[:8]
    pos = 8
    if finalize:
        z_ref, yb_ref, skip_ref, gn_ref = refs[pos:pos + 4]
        pos += 4
    y_ref, _, st_ref = refs[pos:pos + 3]
    u = u_ref[0, rows, :]
    xs = u[:, :d_inner]
    dt = _softplus(dt_ref[0, rows, :] + bias_ref[...])
    dtt = _softplus(dtt_ref[0, :, rows] + biast_ref[...])
    da = dt * a_ref[...]
    dat = dtt * at_ref[...]
    q_col = jnp.dot(tri_col, da, precision=HIGHEST, preferred_element_type=F32)
    q_row = jnp.dot(dat, tri_row, precision=HIGHEST, preferred_element_type=F32)
    total = jnp.sum(da, axis=0, keepdims=True)
    y_pairs = []
    for g in range(n_groups):
        b_g = u[:, d_inner + g * d_state:d_inner + (g + 1) * d_state]
        c_g = u[:, d_inner + (n_groups + g) * d_state:d_inner + (n_groups + g + 1) * d_state]
        c_bf = c_g.astype(BF16)
        cb = _dot_nt(c_bf, b_g.astype(BF16))
        b_t = b_g.T.astype(BF16)
        for pp in range(pairs_per_group):
            p = g * pairs_per_group + pp
            h0, h1 = 2 * p, 2 * p + 1
            x_pair = xs[:, p * LANES:(p + 1) * LANES]
            dt_pair = jnp.where(first, dt[:, h0:h0 + 1], dt[:, h1:h1 + 1])
            qc_pair = jnp.where(first, q_col[:, h0:h0 + 1], q_col[:, h1:h1 + 1])
            tot_pair = jnp.where(first[:1], total[:, h0:h0 + 1], total[:, h1:h1 + 1])
            xdt = x_pair * dt_pair
            xdt_bf = xdt.astype(BF16)
            diag = []
            for h in (h0, h1):
                diff = q_col[:, h:h + 1] - q_row[h:h + 1, :]
                lmat = jnp.exp(jnp.where(causal, diff, NEG_BIG))
                diag.append(_dot((cb * lmat).astype(BF16), xdt_bf))
            y_diag = jnp.where(first, diag[0], diag[1])
            state = st_ref[p]
            y_off = _dot(c_bf, state.astype(BF16)) * jnp.exp(qc_pair)
            y_pairs.append(y_diag + y_off)
            carry_in = (xdt * jnp.exp(tot_pair - qc_pair)).astype(BF16)
            st_ref[p] = jnp.exp(tot_pair) * state + _dot(b_t, carry_in)
    y = jnp.concatenate(y_pairs, axis=1)
    if finalize:
        y = y + yb_ref[0, rows, :] + skip_ref[...] * xs
        y = y * _silu(z_ref[0, rows, :])
        gw = d_inner // n_groups
        y = jnp.concatenate(
            [y[:, g * gw:(g + 1) * gw] * lax.rsqrt(
                jnp.mean(y[:, g * gw:(g + 1) * gw] ** 2, axis=1, keepdims=True) + NORM_EPS)
             for g in range(n_groups)], axis=1)
        y = y * gn_ref[...]
    y_ref[0, rows, :] = y.astype(y_ref.dtype)


def _ssd_scan(u, dt, dtt, a, dt_bias, init, *, reverse, d_inner, d_state, n_groups, final=None):
    b, length, ch = u.shape
    heads = dt.shape[2]
    step_rows = SSD_CHUNK * math.gcd(length // SSD_CHUNK, SSD_CHUNKS_PER_STEP)
    nc = length // step_rows
    pairs = heads // 2
    cidx = (lambda j: nc - 1 - j) if reverse else (lambda j: j)
    small = lambda shape: pl.BlockSpec(shape, lambda i, j: (0,) * len(shape))
    in_specs = [pl.BlockSpec((1, step_rows, ch), lambda i, j: (i, cidx(j), 0)),
                pl.BlockSpec((1, step_rows, heads), lambda i, j: (i, cidx(j), 0)),
                pl.BlockSpec((1, heads, step_rows), lambda i, j: (i, 0, cidx(j))),
                small((1, heads)), small((heads, 1)), small((1, heads)), small((heads, 1)),
                pl.BlockSpec((1, pairs, d_state, LANES), lambda i, j: (i, 0, 0, 0))]
    args = [u, dt, dtt, a.reshape(1, heads), a.reshape(heads, 1),
            dt_bias.reshape(1, heads).astype(F32), dt_bias.reshape(heads, 1).astype(F32), init]
    out_dtype = F32
    if final is not None:
        z, yb, d_skip, gnorm_g = final
        in_specs += [pl.BlockSpec((1, step_rows, d_inner), lambda i, j: (i, cidx(j), 0)),
                     pl.BlockSpec((1, step_rows, d_inner), lambda i, j: (i, cidx(j), 0)),
                     small((1, d_inner)), small((1, d_inner))]
        args += [z, yb, jnp.repeat(d_skip.astype(F32), HEAD_DIM).reshape(1, d_inner),
                 gnorm_g.reshape(1, d_inner).astype(F32)]
        out_dtype = BF16
    kern = functools.partial(_ssd_kernel, reverse=reverse, finalize=final is not None,
                             d_inner=d_inner, d_state=d_state, n_groups=n_groups)
    return pl.pallas_call(
        kern,
        out_shape=[jax.ShapeDtypeStruct((b, length, d_inner), out_dtype),
                   jax.ShapeDtypeStruct((b, pairs, d_state, LANES), F32)],
        grid=(b, nc), in_specs=in_specs,
        out_specs=[pl.BlockSpec((1, step_rows, d_inner), lambda i, j: (i, cidx(j), 0)),
                   pl.BlockSpec((1, pairs, d_state, LANES), lambda i, j: (i, 0, 0, 0))],
        scratch_shapes=[pltpu.VMEM((pairs, d_state, LANES), F32)],
        compiler_params=_params(("parallel", "arbitrary")), name="ssd_scan",
    )(*args)


def _outproj_kernel(oa_ref, ob_ref, w_ref, x_ref, gpost_ref, gate_ref, gpre_ref, sc_ref, sh_ref, wrh_ref, wrl_ref,
                    xo_ref, h_ref, aff_ref, *, row_chunk):
    half = oa_ref.shape[2]
    for r0 in range(0, x_ref.shape[1], row_chunk):
        rows = slice(r0, r0 + row_chunk)
        mix = _dot(oa_ref[0, rows, :], w_ref[:half, :]) + _dot(ob_ref[0, rows, :], w_ref[half:, :])
        x_new = x_ref[0, rows, :] + gate_ref[0] * _rms(mix, gpost_ref[...])
        xo_ref[0, rows, :] = x_new
        h = _rms(x_new, gpre_ref[...]) * sc_ref[0] + sh_ref[0]
        h_ref[0, rows, :] = h
        h_hi = h.astype(BF16)
        h_lo = (h - h_hi.astype(F32)).astype(BF16)
        logits = _dot_nt(wrh_ref[...], h_hi) + _dot_nt(wrh_ref[...], h_lo) + _dot_nt(wrl_ref[...], h_hi)
        e = jnp.exp(logits - jnp.max(logits, axis=0, keepdims=True))
        aff_ref[0, :, rows] = e / jnp.sum(e, axis=0, keepdims=True)


def _out_project(oa, ob, w_out, x, g_post, gate, g_pre, scale1p, shift, w_router):
    b, n, d = x.shape
    half = oa.shape[2]
    n_exp = w_router.shape[1]
    tm = min(OUT_PROJ_ROWS, n)
    bm = gate.shape[0]
    mod_map = (lambda i, j: (i, 0, 0)) if bm > 1 else (lambda i, j: (0, 0, 0))
    row = pl.BlockSpec((1, d), lambda i, j: (0, 0))
    mod = pl.BlockSpec((1, 1, d), mod_map)
    tile = pl.BlockSpec((1, tm, d), lambda i, j: (i, j, 0))
    act = pl.BlockSpec((1, tm, half), lambda i, j: (i, j, 0))
    wr_spec = pl.BlockSpec((n_exp, d), lambda i, j: (0, 0))
    wr = w_router.T.astype(F32)
    wr_hi = wr.astype(BF16)
    wr_lo = (wr - wr_hi.astype(F32)).astype(BF16)
    return pl.pallas_call(
        functools.partial(_outproj_kernel, row_chunk=min(ROW_TILE, tm)),
        out_shape=[jax.ShapeDtypeStruct((b, n, d), F32), jax.ShapeDtypeStruct((b, n, d), F32),
                   jax.ShapeDtypeStruct((b, n_exp, n), F32)],
        grid=(b, n // tm),
        in_specs=[act, act, pl.BlockSpec((2 * half, d), lambda i, j: (0, 0)), tile, row, mod, row, mod, mod,
                  wr_spec, wr_spec],
        out_specs=[tile, tile, pl.BlockSpec((1, n_exp, tm), lambda i, j: (i, 0, j))],
        compiler_params=_params(("parallel", "parallel")), name="out_proj_router",
    )(oa, ob, w_out, x, g_post.reshape(1, d), gate, g_pre.reshape(1, d), scale1p, shift, wr_hi, wr_lo)


def _topk_kernel(aff_ref, idx_ref, gate_ref, work_ref, key_ref, *, cap, slot_block):
    n_exp, n = aff_ref.shape[1], aff_ref.shape[2]
    n_blk = n // LANES
    aff = aff_ref[0]

    def search(i, thr):
        cand = thr | jnp.left_shift(jnp.int32(1), 30 - i)
        cnt = jnp.sum(jnp.where(aff_ref[0] >= pltpu.bitcast(cand, F32), 1.0, 0.0), axis=1, keepdims=True)
        return jnp.where(cnt >= cap, cand, thr)

    thr = lax.fori_loop(0, 31, search, jnp.zeros((n_exp, 1), I32))
    thr_f = pltpu.bitcast(thr, F32)
    above = aff > thr_f
    tied = (aff >= thr_f) & ~above
    need = cap - jnp.sum(jnp.where(above, 1.0, 0.0), axis=1, keepdims=True)

    r = lax.broadcasted_iota(I32, (LANES, LANES), 0)
    c = lax.broadcasted_iota(I32, (LANES, LANES), 1)
    incl = (r <= c).astype(BF16)

    def exclusive_prefix(mask_f32):
        starts = [jnp.zeros((n_exp, 1), F32)]
        for j in range(n_blk):
            blk = mask_f32[:, j * LANES:(j + 1) * LANES]
            run = _dot(blk.astype(BF16), incl)
            work_ref[:, j * LANES:(j + 1) * LANES] = run - blk + starts[-1]
            starts.append(starts[-1] + jnp.sum(blk, axis=1, keepdims=True))
        return starts

    exclusive_prefix(jnp.where(tied, 1.0, 0.0))
    sel = above | (tied & (work_ref[...] < need))
    sel_f = jnp.where(sel, 1.0, 0.0)
    gate_ref[0] = jnp.where(sel, aff, 0.0)
    starts = exclusive_prefix(sel_f)
    key = jnp.where(sel, work_ref[...], -1.0)
    for j in range(n_blk):
        key_ref[j] = key[:, j * LANES:(j + 1) * LANES]

    lane = lax.broadcasted_iota(I32, (slot_block, LANES), 1).astype(F32)
    slot = lax.broadcasted_iota(I32, (slot_block, LANES), 0).astype(F32)

    for sb in range(cap // slot_block):
        first = sum(jnp.where(starts[j + 1] <= sb * slot_block, 1.0, 0.0) for j in range(n_blk))
        last = sum(jnp.where(starts[j] < (sb + 1) * slot_block, 1.0, 0.0) for j in range(n_blk))
        want = slot + float(sb * slot_block)
        for e in range(n_exp):
            def per_block(j, acc, e=e, want=want):
                key_row = key_ref[j, e:e + 1, :]
                tok = lane + lax.convert_element_type(j * LANES, F32)
                return acc + jnp.where(key_row == want, tok, 0.0)

            acc = lax.fori_loop(first[e, 0].astype(I32), last[e, 0].astype(I32), per_block,
                                jnp.zeros((slot_block, LANES), F32))
            idx_ref[0, e, sb * slot_block:(sb + 1) * slot_block, :] = (
                jnp.sum(acc, axis=1, keepdims=True).astype(I32))


def _route(aff_t):
    b, n_exp, n = aff_t.shape
    cap = CAPACITY_FACTOR * n // n_exp
    slot_block = min(LANES, cap)
    kern = functools.partial(_topk_kernel, cap=cap, slot_block=slot_block)
    idx, gate = pl.pallas_call(
        kern,
        out_shape=[jax.ShapeDtypeStruct((b, n_exp, cap, 1), I32), jax.ShapeDtypeStruct((b, n_exp, n), F32)],
        grid=(b,),
        in_specs=[pl.BlockSpec((1, n_exp, n), lambda i: (i, 0, 0))],
        out_specs=[pl.BlockSpec((1, n_exp, cap, 1), lambda i: (i, 0, 0, 0)),
                   pl.BlockSpec((1, n_exp, n), lambda i: (i, 0, 0))],
        scratch_shapes=[pltpu.VMEM((n_exp, n), F32), pltpu.VMEM((n // LANES, n_exp, LANES), F32)],
        compiler_params=_params(("parallel",)), name="expert_choice_route",
    )(aff_t)
    return idx.reshape(b * n_exp, 1, cap), gate.reshape(b * n_exp, 1, n)


def _moe_kernel(idx_ref, nxt_ref, h_ref, wg_ref, wu_ref, wd_ref, y_ref, xg_ref, *, cap, prefetch):
    sub = lax.broadcasted_iota(I32, (SUBLANES, h_ref.shape[2]), 0)

    def gather_group(src_ref, dst, base):
        tile = jnp.zeros(sub.shape, F32)
        for u in range(SUBLANES):
            t = src_ref[0, 0, base + u]
            r = t & (SUBLANES - 1)
            rows = h_ref[0, pl.ds(pl.multiple_of(t - r, SUBLANES), SUBLANES), :]
            rolled = pltpu.roll(rows, (u - r) & (SUBLANES - 1), axis=0)
            tile = jnp.where(sub == u, rolled, tile)
        dst[pl.ds(base, SUBLANES), :] = tile

    def gather_loop(dst):
        def body(i8, carry):
            gather_group(idx_ref, dst, pl.multiple_of(i8 * SUBLANES, SUBLANES))
            return carry
        lax.fori_loop(0, cap // SUBLANES, body, 0)

    if prefetch:
        e = pl.program_id(1)
        slot = e & 1

        @pl.when(e == 0)
        def _():
            gather_loop(xg_ref.at[0])
    else:
        slot = 0
        gather_loop(xg_ref.at[0])
    xg = xg_ref[slot].astype(BF16)
    gate = _dot(xg, wg_ref[0])
    up = _dot(xg, wu_ref[0])
    hid = (_silu(gate) * up).astype(BF16)
    y_ref[0, 0] = _dot(hid, wd_ref[0])
    if prefetch:
        for i8 in range(cap // SUBLANES):
            gather_group(nxt_ref, xg_ref.at[1 - slot], i8 * SUBLANES)


def _combine_kernel(idx_ref, gate_ref, y_ref, acc_ref, *, cap):
    @pl.when(pl.program_id(1) == 0)
    def _():
        acc_ref[...] = jnp.zeros_like(acc_ref)

    sub = lax.broadcasted_iota(I32, (SUBLANES, acc_ref.shape[2]), 0)

    def scatter(i8, carry):
        base = pl.multiple_of(i8 * SUBLANES, SUBLANES)
        y_tile = y_ref[0, 0, pl.ds(base, SUBLANES), :]
        for u0 in range(0, SUBLANES, 4):
            pending = []
            for u in range(u0, u0 + 4):
                t = idx_ref[0, 0, base + u]
                g = gate_ref[0, 0, t]
                r = t & (SUBLANES - 1)
                dst = acc_ref.at[0, pl.ds(pl.multiple_of(t - r, SUBLANES), SUBLANES), :]
                rolled = pltpu.roll(y_tile, (r - u) & (SUBLANES - 1), axis=0)
                pending.append((dst, dst[...] + g * rolled, sub == r))
            for dst, new, mask in pending:
                pltpu.store(dst, new, mask=mask)
        return carry

    lax.fori_loop(0, cap // SUBLANES, scatter, 0)


def _expert_ffn(h, idx, gate, wg, wu, wd):
    b, n, d = h.shape
    n_exp, _, ff = wg.shape
    cap = idx.shape[2]
    smem = lambda width: pl.BlockSpec((1, 1, width), lambda i, e: (i * n_exp + e, 0, 0), memory_space=pltpu.SMEM)
    sample_major = n * d * h.dtype.itemsize >= 3 * d * ff * wg.dtype.itemsize
    if sample_major:
        grid = (b, n_exp)
        be = lambda i, e: (i, e)
        nxt = pl.BlockSpec((1, 1, cap), lambda i, e: (i * n_exp + jnp.minimum(e + 1, n_exp - 1), 0, 0),
                           memory_space=pltpu.SMEM)
        h_spec = pl.BlockSpec((1, n, d), lambda i, e: (i, 0, 0), pipeline_mode=pl.Buffered(1))
    else:
        grid = (n_exp, b)
        be = lambda e, i: (i, e)
        nxt = pl.BlockSpec((1, 1, cap), lambda e, i: (i * n_exp + e, 0, 0), memory_space=pltpu.SMEM)
        h_spec = pl.BlockSpec((1, n, d), lambda e, i: (i, 0, 0))
    y = pl.pallas_call(
        functools.partial(_moe_kernel, cap=cap, prefetch=sample_major),
        out_shape=jax.ShapeDtypeStruct((b, n_exp, cap, d), F32),
        grid=grid,
        in_specs=[pl.BlockSpec((1, 1, cap), lambda *g: (be(*g)[0] * n_exp + be(*g)[1], 0, 0),
                               memory_space=pltpu.SMEM),
                  nxt, h_spec,
                  pl.BlockSpec((1, d, ff), lambda *g: (be(*g)[1], 0, 0)),
                  pl.BlockSpec((1, d, ff), lambda *g: (be(*g)[1], 0, 0)),
                  pl.BlockSpec((1, ff, d), lambda *g: (be(*g)[1], 0, 0))],
        out_specs=pl.BlockSpec((1, 1, cap, d), lambda *g: (*be(*g), 0, 0)),
        scratch_shapes=[pltpu.VMEM((2, cap, d), F32)],
        compiler_params=_params(("parallel", "arbitrary")), name="expert_ffn",
    )(idx, idx, h, wg, wu, wd)
    return pl.pallas_call(
        functools.partial(_combine_kernel, cap=cap),
        out_shape=jax.ShapeDtypeStruct((b, n, d), F32),
        grid=(b, n_exp),
        in_specs=[smem(cap), smem(n), pl.BlockSpec((1, 1, cap, d), lambda i, e: (i, e, 0, 0))],
        out_specs=pl.BlockSpec((1, n, d), lambda i, e: (i, 0, 0)),
        compiler_params=_params(("parallel", "arbitrary")), name="expert_combine",
    )(idx, gate, y)


def _residual_kernel(x_ref, f_ref, g_ref, gate_ref, o_ref):
    o_ref[0] = x_ref[0] + gate_ref[0] * _rms(f_ref[0], g_ref[...])


def _gated_residual(x, f, g, gate):
    b, n, d = x.shape
    tm = min(ROW_TILE, n)
    bm = gate.shape[0]
    mod_map = (lambda i, j: (i, 0, 0)) if bm > 1 else (lambda i, j: (0, 0, 0))
    tile = pl.BlockSpec((1, tm, d), lambda i, j: (i, j, 0))
    return pl.pallas_call(
        _residual_kernel, out_shape=jax.ShapeDtypeStruct((b, n, d), F32),
        grid=(b, n // tm),
        in_specs=[tile, tile, pl.BlockSpec((1, d), lambda i, j: (0, 0)), pl.BlockSpec((1, 1, d), mod_map)],
        out_specs=tile,
        compiler_params=_params(("parallel", "parallel")), name="gated_residual",
    )(x, f, g.reshape(1, d), gate)


def _rope_tables(n):
    t = jnp.arange(n, dtype=I32)
    row = (t // GRID_W).astype(F32)
    col = (t % GRID_W).astype(F32)
    n_freq = HEAD_DIM // 4
    inv_freq = ROPE_THETA ** (-jnp.arange(n_freq, dtype=F32) / n_freq)
    ang = jnp.concatenate([row[:, None] * inv_freq, col[:, None] * inv_freq], axis=-1)
    cos, sin = jnp.cos(ang), jnp.sin(ang)
    cos_t = jnp.tile(jnp.concatenate([cos, cos], axis=-1), (1, LANES // HEAD_DIM))
    sin_t = jnp.tile(jnp.concatenate([-sin, sin], axis=-1), (1, LANES // HEAD_DIM))
    return cos_t, sin_t


def _deinterleave_heads(w):
    d, width = w.shape
    w = w.reshape(d, width // HEAD_DIM, HEAD_DIM // 2, 2)
    return jnp.concatenate([w[..., 0], w[..., 1]], axis=-1).reshape(d, width)


def _dup_heads(w):
    d, width = w.shape
    w = w.reshape(d, width // HEAD_DIM, 1, HEAD_DIM)
    return jnp.concatenate([w, w], axis=2).reshape(d, 2 * width)


def _mod_split(mod_rows):
    sh1, sc1, gt1, sh2, sc2, gt2 = jnp.split(mod_rows[:, None, :], 6, axis=-1)
    return sh1, 1.0 + sc1, gt1, sh2, 1.0 + sc2, gt2


def _moe_block(x_mid, h2, aff_t, g_post_ffn, gt2, wg, wu, wd):
    b, n, d = h2.shape
    n_exp, _, ff = wg.shape
    idx, gate = _route(aff_t)
    if b > 1 and n * d * h2.dtype.itemsize < 3 * d * ff * wg.dtype.itemsize:
        cap = idx.shape[2]
        idx = idx.reshape(b, n_exp, cap) + (jnp.arange(b, dtype=I32) * n)[:, None, None]
        idx = idx.transpose(1, 0, 2).reshape(n_exp, 1, b * cap)
        gate = gate.reshape(b, n_exp, n).transpose(1, 0, 2).reshape(n_exp, 1, b * n)
        ffn = _expert_ffn(h2.reshape(1, b * n, d), idx, gate, wg, wu, wd).reshape(b, n, d)
    else:
        ffn = _expert_ffn(h2, idx, gate, wg, wu, wd)
    return _gated_residual(x_mid, ffn, g_post_ffn, gt2)


def kernel(x, c, ctx, c_ctx, l0_w_mod, l0_b_mod, l0_g_pre_mix, l0_g_post_mix, l0_g_pre_ffn, l0_g_post_ffn, l0_w_in, l0_w_out, l0_lam_q1, l0_lam_k1, l0_lam_q2, l0_lam_k2, l0_subln_g, l0_rpb, l0_w_router, l0_w_gate, l0_w_up, l0_w_down, l1_w_mod, l1_b_mod, l1_g_pre_mix, l1_g_post_mix, l1_g_pre_ffn, l1_g_post_ffn, l1_w_in, l1_w_out, l1_sinks, l1_conv_w, l1_conv_b, l1_a_log_f, l1_a_log_b, l1_dt_bias_f, l1_dt_bias_b, l1_d_skip, l1_gnorm_g, l1_w_router, l1_w_gate, l1_w_up, l1_w_down):
    b, n, d = x.shape
    m = ctx.shape[1]
    half = d // 2
    scale = HEAD_DIM ** -0.5 * math.log2(math.e)
    cos_t, sin_t = _rope_tables(n)
    cond_rows = -(-(b + 1) // SUBLANES) * SUBLANES
    cond = jnp.zeros((cond_rows, d), F32).at[:b].set(c).at[b].set(c_ctx)

    mod = _modulation(cond, l0_w_mod, l0_b_mod)
    sh1, sc1, gt1, sh2, sc2, gt2 = _mod_split(mod[:b])
    csh1, csc1, cgt1, csh2, csc2, cgt2 = _mod_split(mod[b:b + 1])

    a_qk = half
    wq, wk, rest = l0_w_in[:, :a_qk], l0_w_in[:, a_qk:2 * a_qk], l0_w_in[:, 2 * a_qk:]
    qb_cols = slice(half, 2 * half)
    rest = rest.at[:, qb_cols].multiply(scale)
    w_in0 = jnp.concatenate([_deinterleave_heads(wq) * scale, _deinterleave_heads(wk), rest], axis=1).astype(BF16)
    segs0 = [(half, BF16)] * 6
    qa, ka, va, qb, kb, vb = _project(x, l0_g_pre_mix, sc1, sh1, w_in0, segs0, rope=(2 * a_qk, cos_t, sin_t))
    qa_c, ka_c, va_c, qb_c, kb_c, vb_c = _project(ctx, l0_g_pre_mix, csc1, csh1, w_in0, segs0)

    lam_init = 0.8 - 0.6 * math.exp(-0.3 * 0)
    lam = (jnp.exp(jnp.sum(l0_lam_q1.astype(F32) * l0_lam_k1.astype(F32)))
           - jnp.exp(jnp.sum(l0_lam_q2.astype(F32) * l0_lam_k2.astype(F32))) + lam_init)
    o_a, (wg0, wu0, wd0, wg1, wu1, wd1) = _diff_attention(
        qa, [(ka, va), (ka_c, va_c)], lam, l0_subln_g, 1.0 - lam_init,
        cast=(l0_w_gate, l0_w_up, l0_w_down, l1_w_gate, l1_w_up, l1_w_down))
    o_b = _neighbourhood_attention(qb, kb, vb, kb_c, vb_c, l0_rpb)
    oc_a, _ = _diff_attention(qa_c, [(ka_c, va_c)], lam, l0_subln_g, 1.0 - lam_init)
    oc_b = _context_attention(qb_c, kb_c, vb_c)

    w_out0 = l0_w_out.astype(BF16)
    x_mid, h2, aff_t = _out_project(o_a, o_b, w_out0, x, l0_g_post_mix, gt1, l0_g_pre_ffn, sc2, sh2, l0_w_router)
    x1 = _moe_block(x_mid, h2, aff_t, l0_g_post_ffn, gt2, wg0, wu0, wd0)
    c_mid, hc2, aff_c = _out_project(oc_a, oc_b, w_out0, ctx, l0_g_post_mix, cgt1, l0_g_pre_ffn, csc2, csh2,
                                     l0_w_router)
    ctx1 = _moe_block(c_mid, hc2, aff_c, l0_g_post_ffn, cgt2, wg0, wu0, wd0)

    mod = _modulation(cond, l1_w_mod, l1_b_mod)
    sh1, sc1, gt1, sh2, sc2, gt2 = _mod_split(mod[:b])
    csh1, csc1, _, _, _, _ = _mod_split(mod[b:b + 1])

    c_q = half
    d_inner = half
    d_heads = l1_a_log_f.shape[0]
    d_xbc = l1_conv_w.shape[2]
    c_kv = (l1_w_in.shape[1] - c_q - d_xbc - d_inner - 2 * d_heads) // 2
    n_groups = 2
    d_state = (d_xbc - d_inner) // (2 * n_groups)
    wq = l1_w_in[:, :c_q]
    wk = l1_w_in[:, c_q:c_q + c_kv]
    wv = l1_w_in[:, c_q + c_kv:c_q + 2 * c_kv]
    o_x = c_q + 2 * c_kv
    w_x = l1_w_in[:, o_x:o_x + d_xbc + d_inner]
    w_dt = l1_w_in[:, o_x + d_xbc + d_inner:]
    w_kv = jnp.concatenate([_dup_heads(_deinterleave_heads(wk)), _dup_heads(wv), w_x, w_dt], axis=1)
    w_lat = jnp.concatenate([_deinterleave_heads(wq) * scale, w_kv], axis=1).astype(BF16)
    w_ctx = w_kv.astype(BF16)
    w_dt_t = w_dt.T.astype(BF16)
    segs_lat = [(c_q, BF16), (2 * c_kv, BF16), (2 * c_kv, BF16), (d_xbc, F32), (d_inner, F32), (2 * d_heads, F32)]
    q, k2, v2, xbc, z, dt, dtt = _project(x1, l1_g_pre_mix, sc1, sh1, w_lat, segs_lat,
                                          rope=(c_q + 2 * c_kv, cos_t, sin_t), wt=w_dt_t)
    k2_c, v2_c, xbc_c, _, dt_c, dtt_c = _project(ctx1, l1_g_pre_mix, csc1, csh1, w_ctx, segs_lat[1:], wt=w_dt_t)

    o_c = _window_attention(q, k2, v2, k2_c, v2_c, l1_sinks.astype(F32) * math.log2(math.e))

    a_f = -jnp.exp(l1_a_log_f.astype(F32))
    a_b = -jnp.exp(l1_a_log_b.astype(F32))
    u = _conv_silu(xbc, l1_conv_w, l1_conv_b)
    u_c = _conv_silu(xbc_c, l1_conv_w, l1_conv_b)
    ssd = functools.partial(_ssd_scan, d_inner=d_inner, d_state=d_state, n_groups=n_groups)
    zero_state = jnp.zeros((b, d_heads // 2, d_state, LANES), F32)
    _, sc_f = ssd(u_c, dt_c[..., :d_heads], dtt_c[:, :d_heads], a_f, l1_dt_bias_f, zero_state, reverse=False)
    _, sc_b = ssd(u_c, dt_c[..., d_heads:], dtt_c[:, d_heads:], a_b, l1_dt_bias_b, zero_state, reverse=True)
    y_b, _ = ssd(u, dt[..., d_heads:], dtt[:, d_heads:], a_b, l1_dt_bias_b, sc_b, reverse=True)
    o_d, _ = ssd(u, dt[..., :d_heads], dtt[:, :d_heads], a_f, l1_dt_bias_f, sc_f, reverse=False,
                 final=(z, y_b, l1_d_skip, l1_gnorm_g))

    x_mid, h2, aff_t = _out_project(o_c, o_d, l1_w_out.astype(BF16), x1, l1_g_post_mix, gt1, l1_g_pre_ffn, sc2, sh2,
                                    l1_w_router)
    return _moe_block(x_mid, h2, aff_t, l1_g_post_ffn, gt2, wg1, wu1, wd1)
```
